```python
import math
import jax, jax.numpy as jnp
from jax import lax
import numpy as np

D_MODEL = 1024
BATCH = 8
SEQ = 2048
DEPTH = 4
DEC_BATCH = 128
DEC_SEQ = 4
PAST_LEN = 16384
PAGE_SIZE = 128

D_MIX = D_MODEL
RWKV_WIDTH = D_MIX // 2
RWKV_HEAD = 64
RWKV_HEADS = RWKV_WIDTH // RWKV_HEAD
W_LORA = 64
A_LORA = 64
V_LORA = 32
G_LORA = 128
RWKV_COLS = 3 * RWKV_WIDTH + W_LORA + A_LORA + G_LORA
S5_WIDTH = D_MIX - RWKV_WIDTH
S5_GROUP = 16
S5_GROUPS = S5_WIDTH // S5_GROUP
S5_STATE = 64
IN_COLS = RWKV_COLS + S5_WIDTH
N_MEM = 256
X_HEADS = 4
X_HEAD_DIM = D_MODEL // X_HEADS
D_FF = 2816
CONV_W = 3
RMS_EPS = 1e-6
LNX_EPS = 64e-5
DT_MIN = 1e-3
DT_MAX = 1e-1

kernel_name = 'hybrid_rwkv7_s5_memxattn_convffn_step'


def rms_norm(x, g):
    xf = x.astype(jnp.float32)
    y = xf * lax.rsqrt(jnp.mean(xf * xf, axis=-1, keepdims=True) + RMS_EPS)
    return (y * g.astype(jnp.float32)).astype(x.dtype)


def rwkv7_recurrence(r, w, k, v, kk, a, s0):
    def step(s, inp):
        r_t, w_t, k_t, v_t, kk_t, a_t = inp
        sa = jnp.einsum('bhij,bhj->bhi', s, -kk_t)
        s = (s * w_t[:, :, None, :] + sa[..., None] * (kk_t * a_t)[:, :, None, :]
             + v_t[..., None] * k_t[:, :, None, :])
        o = jnp.einsum('bhij,bhj->bhi', s, r_t)
        return s, o
    xs = tuple(jnp.swapaxes(t, 0, 1) for t in (r, w, k, v, kk, a))
    s, o = lax.scan(step, s0.astype(jnp.float32), xs)
    return jnp.swapaxes(o, 0, 1), s


def rwkv7_group(p, shift_prev, s0, v_first, l, prm):
    B, T, _ = p.shape
    f32 = jnp.float32
    RW = RWKV_WIDTH
    p_prev = jnp.concatenate([shift_prev[:, None, :].astype(p.dtype), p[:, :-1]], axis=1)
    q = p + (p_prev - p) * prm['shift_mu'][l]
    new_shift = p[:, -1]
    o_w = 3 * RW
    o_a = o_w + W_LORA
    o_g = o_a + A_LORA
    r, k, v = q[..., :RW], q[..., RW:2 * RW], q[..., 2 * RW:3 * RW]
    xw, xa, xg = q[..., o_w:o_a], q[..., o_a:o_g], q[..., o_g:]
    w = -jax.nn.softplus(-(prm['rwkv_w0'][l] + jnp.tanh(xw) @ prm['rwkv_w_w2'][l])) - 0.5
    decay = jnp.exp(-jnp.exp(w.astype(f32)))
    a = jax.nn.sigmoid(prm['rwkv_a0'][l] + xa @ prm['rwkv_w_a2'][l])
    g = jax.nn.sigmoid(xg) @ prm['rwkv_w_g2'][l]
    v_own = v
    if v_first is not None:
        j = l - 1
        mix = jax.nn.sigmoid(prm['rwkv_v0'][j] + (v @ prm['rwkv_w_v1'][j]) @ prm['rwkv_w_v2'][j])
        v = v + (v_first - v) * mix
    heads = lambda t: t.astype(f32).reshape(B, T, RWKV_HEADS, RWKV_HEAD)
    kk = heads(k * prm['rwkv_k_k'][l])
    kk = kk * lax.rsqrt(jnp.maximum(jnp.sum(kk * kk, axis=-1, keepdims=True), 1e-24))
    k = k * (1.0 + (a - 1.0) * prm['rwkv_k_a'][l])
    rh, kh, vh, ah, wh = heads(r), heads(k), heads(v), heads(a), heads(decay)
    o, s = rwkv7_recurrence(rh, wh, kh, vh, kk, ah, s0)
    mean = jnp.mean(o, axis=-1, keepdims=True)
    var = jnp.mean(jnp.square(o - mean), axis=-1, keepdims=True)
    on = ((o - mean) * lax.rsqrt(var + LNX_EPS)).reshape(B, T, RW)
    on = on * prm['rwkv_lnx_w'][l] + prm['rwkv_lnx_b'][l]
    bonus = (jnp.sum(rh * kh * prm['rwkv_r_k'][l].astype(f32), axis=-1, keepdims=True) * vh).reshape(B, T, RW)
    out = ((on + bonus) * g.astype(f32)).astype(p.dtype)
    return out, v_own, s, new_shift


def s5_group(u, h0_re, h0_im, l, prm):
    B, T, _ = u.shape
    f32 = jnp.float32
    uf = u.astype(f32).reshape(B, T, S5_GROUPS, S5_GROUP)
    a_re = prm['s5_a_re'][l].astype(f32)
    a_im = prm['s5_a_im'][l].astype(f32)
    dt = jnp.exp(prm['s5_log_dt'][l].astype(f32))[:, None]
    mag = jnp.exp(a_re * dt)
    ab_re = mag * jnp.cos(a_im * dt)
    ab_im = mag * jnp.sin(a_im * dt)
    den = a_re * a_re + a_im * a_im
    nr, ni = ab_re - 1.0, ab_im
    cf_re = (nr * a_re + ni * a_im) / den
    cf_im = (ni * a_re - nr * a_im) / den
    b_re = prm['s5_b_re'][l].astype(f32)
    b_im = prm['s5_b_im'][l].astype(f32)
    bb_re = cf_re[..., None] * b_re - cf_im[..., None] * b_im
    bb_im = cf_re[..., None] * b_im + cf_im[..., None] * b_re
    bu_re = jnp.einsum('gph,btgh->btgp', bb_re, uf)
    bu_im = jnp.einsum('gph,btgh->btgp', bb_im, uf)
    h0r = h0_re.astype(f32)
    h0i = h0_im.astype(f32)
    bu_re = bu_re.at[:, 0].add(ab_re * h0r - ab_im * h0i)
    bu_im = bu_im.at[:, 0].add(ab_re * h0i + ab_im * h0r)
    ar = jnp.broadcast_to(ab_re[None, None], (1, T, S5_GROUPS, S5_STATE))
    ai = jnp.broadcast_to(ab_im[None, None], (1, T, S5_GROUPS, S5_STATE))

    def combine(e1, e2):
        a1r, a1i, b1r, b1i = e1
        a2r, a2i, b2r, b2i = e2
        return (a2r * a1r - a2i * a1i, a2r * a1i + a2i * a1r,
                a2r * b1r - a2i * b1i + b2r, a2r * b1i + a2i * b1r + b2i)

    _, _, hr, hi = lax.associative_scan(combine, (ar, ai, bu_re, bu_im), axis=1)
    c_re = prm['s5_c_re'][l].astype(f32)
    c_im = prm['s5_c_im'][l].astype(f32)
    y = (jnp.einsum('gnp,btgp->btgn', c_re, hr) - jnp.einsum('gnp,btgp->btgn', c_im, hi)
         + prm['s5_d'][l].astype(f32) * uf)
    y = jax.nn.gelu(y.reshape(B, T, S5_WIDTH))
    y = y * jax.nn.sigmoid(y @ prm['s5_w_glu'][l].astype(f32) + prm['s5_b_glu'][l].astype(f32))
    return y.astype(u.dtype), hr[:, -1], hi[:, -1]


def cross_attend(h, mk, mv, wq, wo):
    B, T, _ = h.shape
    q = (h @ wq).reshape(B, T, X_HEADS, X_HEAD_DIM)
    s = jnp.einsum('bthd,bmhd->bhtm', q, mk.astype(h.dtype)).astype(jnp.float32) * (X_HEAD_DIM ** -0.5)
    pr = jax.nn.softmax(s, axis=-1).astype(h.dtype)
    o = jnp.einsum('bhtm,bmhd->bthd', pr, mv.astype(h.dtype)).reshape(B, T, D_MODEL)
    return o @ wo


def conv_ffn(h, buf, w_gate, w_up, conv_w, conv_b, w_down):
    T = h.shape[1]
    gt = h @ w_gate
    padded = jnp.concatenate([buf.astype(gt.dtype), gt], axis=1)
    conv = conv_b + sum(conv_w[i] * padded[:, i:i + T] for i in range(CONV_W))
    y = (jax.nn.silu(conv) * (h @ w_up)) @ w_down
    return y, padded[:, -(CONV_W - 1):]


def run_trunk(x, mem_k, mem_v, rw_state, shift_state, s5_re, s5_im, conv_state, prm):
    v_first = None
    new_rw, new_shift, new_re, new_im, new_conv = [], [], [], [], []
    for l in range(DEPTH):
        h = rms_norm(x, prm['norm_mix'][l])
        proj = h @ prm['w_in'][l]
        o_rw, v_l, s_rw, sh = rwkv7_group(proj[..., :RWKV_COLS], shift_state[l], rw_state[l], v_first, l, prm)
        if l == 0:
            v_first = v_l
        o_s5, hr, hi = s5_group(proj[..., RWKV_COLS:], s5_re[l], s5_im[l], l, prm)
        x = x + jnp.concatenate([o_rw, o_s5], axis=-1) @ prm['w_out'][l]
        h = rms_norm(x, prm['norm_cross'][l])
        x = x + cross_attend(h, mem_k[l], mem_v[l], prm['w_cq'][l], prm['w_co'][l])
        h = rms_norm(x, prm['norm_ffn'][l])
        f, cb = conv_ffn(h, conv_state[l], prm['w_gate'][l], prm['w_up'][l],
                         prm['ffn_conv_w'][l], prm['ffn_conv_b'][l], prm['w_down'][l])
        x = x + f
        new_rw.append(s_rw)
        new_shift.append(sh)
        new_re.append(hr)
        new_im.append(hi)
        new_conv.append(cb)
    y = rms_norm(x, prm['norm_final'])
    return (y, jnp.stack(new_rw), jnp.stack(new_shift), jnp.stack(new_re),
            jnp.stack(new_im), jnp.stack(new_conv))


def setup_inputs(seed: int = 0) -> dict:
    key = jax.random.key(seed)
    ks = iter(jax.random.split(key, 64))
    f32 = jnp.float32
    L = DEPTH
    RW = RWKV_WIDTH

    def nrm(shape, scale=1.0):
        return jax.random.normal(next(ks), shape, f32) * scale

    def gain(shape):
        return 1.0 + nrm(shape, 0.01)

    n_idx = jnp.arange(S5_STATE, dtype=f32)
    inp = {}
    inp['x_prompt'] = nrm((BATCH, SEQ, D_MODEL))
    inp['x_sample'] = nrm((DEC_BATCH, DEC_SEQ, D_MODEL))
    inp['mem_prompt'] = nrm((BATCH, N_MEM, D_MODEL))
    inp['state_rwkv'] = nrm((L, DEC_BATCH, RWKV_HEADS, RWKV_HEAD, RWKV_HEAD), 0.3)
    inp['state_shift'] = nrm((L, DEC_BATCH, RWKV_COLS))
    inp['state_s5_re'] = nrm((L, DEC_BATCH, S5_GROUPS, S5_STATE), 0.5)
    inp['state_s5_im'] = nrm((L, DEC_BATCH, S5_GROUPS, S5_STATE), 0.5)
    inp['state_ffn_conv'] = nrm((L, DEC_BATCH, CONV_W - 1, D_FF))
    inp['cache_mem_k'] = nrm((L, DEC_BATCH, N_MEM, X_HEADS, X_HEAD_DIM))
    inp['cache_mem_v'] = nrm((L, DEC_BATCH, N_MEM, X_HEADS, X_HEAD_DIM))
    inp['norm_mix'] = gain((L, D_MODEL))
    inp['w_in'] = nrm((L, D_MODEL, IN_COLS), D_MODEL ** -0.5)
    inp['shift_mu'] = jax.random.uniform(next(ks), (L, RWKV_COLS), f32)
    inp['rwkv_w0'] = -1.0 + nrm((L, RW), 0.3)
    inp['rwkv_w_w2'] = nrm((L, W_LORA, RW), 0.5 * W_LORA ** -0.5)
    inp['rwkv_a0'] = nrm((L, RW), 0.3)
    inp['rwkv_w_a2'] = nrm((L, A_LORA, RW), 0.5 * A_LORA ** -0.5)
    inp['rwkv_v0'] = nrm((L - 1, RW), 0.3)
    inp['rwkv_w_v1'] = nrm((L - 1, RW, V_LORA), RW ** -0.5)
    inp['rwkv_w_v2'] = nrm((L - 1, V_LORA, RW), 0.5 * V_LORA ** -0.5)
    inp['rwkv_w_g2'] = nrm((L, G_LORA, RW), G_LORA ** -0.5)
    inp['rwkv_k_k'] = 0.85 + nrm((L, RW), 0.05)
    inp['rwkv_k_a'] = 1.0 + nrm((L, RW), 0.05)
    inp['rwkv_r_k'] = nrm((L, RWKV_HEADS, RWKV_HEAD), 0.1)
    inp['rwkv_lnx_w'] = gain((L, RW))
    inp['rwkv_lnx_b'] = nrm((L, RW), 0.01)
    inp['s5_a_re'] = -0.5 + nrm((L, S5_GROUPS, S5_STATE), 0.01)
    inp['s5_a_im'] = math.pi * n_idx + nrm((L, S5_GROUPS, S5_STATE), 0.01)
    inp['s5_log_dt'] = jax.random.uniform(next(ks), (L, S5_GROUPS), f32, math.log(DT_MIN), math.log(DT_MAX))
    inp['s5_b_re'] = nrm((L, S5_GROUPS, S5_STATE, S5_GROUP), (2.0 * S5_GROUP) ** -0.5)
    inp['s5_b_im'] = nrm((L, S5_GROUPS, S5_STATE, S5_GROUP), (2.0 * S5_GROUP) ** -0.5)
    inp['s5_c_re'] = nrm((L, S5_GROUPS, S5_GROUP, S5_STATE), (2.0 * S5_STATE) ** -0.5)
    inp['s5_c_im'] = nrm((L, S5_GROUPS, S5_GROUP, S5_STATE), (2.0 * S5_STATE) ** -0.5)
    inp['s5_d'] = nrm((L, S5_GROUPS, S5_GROUP))
    inp['s5_w_glu'] = nrm((L, S5_WIDTH, S5_WIDTH), S5_WIDTH ** -0.5)
    inp['s5_b_glu'] = nrm((L, S5_WIDTH), 0.01)
    inp['w_out'] = nrm((L, D_MIX, D_MODEL), D_MIX ** -0.5)
    inp['norm_cross'] = gain((L, D_MODEL))
    inp['w_cq'] = nrm((L, D_MODEL, D_MODEL), D_MODEL ** -0.5)
    inp['w_ck'] = nrm((L, D_MODEL, D_MODEL), D_MODEL ** -0.5)
    inp['w_cv'] = nrm((L, D_MODEL, D_MODEL), D_MODEL ** -0.5)
    inp['w_co'] = nrm((L, D_MODEL, D_MODEL), D_MODEL ** -0.5)
    inp['norm_ffn'] = gain((L, D_MODEL))
    inp['w_gate'] = nrm((L, D_MODEL, D_FF), D_MODEL ** -0.5)
    inp['w_up'] = nrm((L, D_MODEL, D_FF), D_MODEL ** -0.5)
    inp['ffn_conv_w'] = nrm((L, CONV_W, D_FF), CONV_W ** -0.5)
    inp['ffn_conv_b'] = nrm((L, D_FF), 0.01)
    inp['w_down'] = nrm((L, D_FF, D_MODEL), D_FF ** -0.5)
    inp['norm_final'] = gain((D_MODEL,))
    return inp


def reference(x_prompt, x_sample, mem_prompt, state_rwkv, state_shift, state_s5_re, state_s5_im,
              state_ffn_conv, cache_mem_k, cache_mem_v, norm_mix, w_in, shift_mu, rwkv_w0, rwkv_w_w2,
              rwkv_a0, rwkv_w_a2, rwkv_v0, rwkv_w_v1, rwkv_w_v2, rwkv_w_g2, rwkv_k_k, rwkv_k_a, rwkv_r_k,
              rwkv_lnx_w, rwkv_lnx_b, s5_a_re, s5_a_im, s5_log_dt, s5_b_re, s5_b_im, s5_c_re, s5_c_im,
              s5_d, s5_w_glu, s5_b_glu, w_out, norm_cross, w_cq, w_ck, w_cv, w_co, norm_ffn, w_gate,
              w_up, ffn_conv_w, ffn_conv_b, w_down, norm_final):
    prm = dict(norm_mix=norm_mix, w_in=w_in, shift_mu=shift_mu, rwkv_w0=rwkv_w0, rwkv_w_w2=rwkv_w_w2,
               rwkv_a0=rwkv_a0, rwkv_w_a2=rwkv_w_a2, rwkv_v0=rwkv_v0, rwkv_w_v1=rwkv_w_v1,
               rwkv_w_v2=rwkv_w_v2, rwkv_w_g2=rwkv_w_g2, rwkv_k_k=rwkv_k_k, rwkv_k_a=rwkv_k_a,
               rwkv_r_k=rwkv_r_k, rwkv_lnx_w=rwkv_lnx_w, rwkv_lnx_b=rwkv_lnx_b, s5_a_re=s5_a_re,
               s5_a_im=s5_a_im, s5_log_dt=s5_log_dt, s5_b_re=s5_b_re, s5_b_im=s5_b_im, s5_c_re=s5_c_re,
               s5_c_im=s5_c_im, s5_d=s5_d, s5_w_glu=s5_w_glu, s5_b_glu=s5_b_glu, w_out=w_out,
               norm_cross=norm_cross, w_cq=w_cq, w_co=w_co, norm_ffn=norm_ffn, w_gate=w_gate,
               w_up=w_up, ffn_conv_w=ffn_conv_w, ffn_conv_b=ffn_conv_b, w_down=w_down,
               norm_final=norm_final)
    B = x_prompt.shape[0]
    p_mem_k = jnp.einsum('bmd,lde->lbme', mem_prompt, w_ck).reshape(DEPTH, B, N_MEM, X_HEADS, X_HEAD_DIM)
    p_mem_v = jnp.einsum('bmd,lde->lbme', mem_prompt, w_cv).reshape(DEPTH, B, N_MEM, X_HEADS, X_HEAD_DIM)
    z_rw = jnp.zeros((DEPTH, B, RWKV_HEADS, RWKV_HEAD, RWKV_HEAD), jnp.float32)
    z_shift = jnp.zeros((DEPTH, B, RWKV_COLS), x_prompt.dtype)
    z_s5 = jnp.zeros((DEPTH, B, S5_GROUPS, S5_STATE), jnp.float32)
    z_conv = jnp.zeros((DEPTH, B, CONV_W - 1, D_FF), x_prompt.dtype)
    y_prompt, p_rwkv, p_shift, p_s5_re, p_s5_im, p_ffn_conv = run_trunk(
        x_prompt, p_mem_k, p_mem_v, z_rw, z_shift, z_s5, z_s5, z_conv, prm)
    y_sample, s_rwkv, s_shift, s_s5_re, s_s5_im, s_ffn_conv = run_trunk(
        x_sample, cache_mem_k, cache_mem_v, state_rwkv, state_shift, state_s5_re, state_s5_im,
        state_ffn_conv, prm)
    return (y_prompt, y_sample, p_rwkv, p_shift, p_s5_re, p_s5_im, p_ffn_conv, p_mem_k, p_mem_v,
            s_rwkv, s_shift, s_s5_re, s_s5_im, s_ffn_conv)
```

```python
import functools
import math

import jax
import jax.numpy as jnp
from jax import lax
from jax.experimental import pallas as pl
from jax.experimental.pallas import tpu as pltpu

F32 = jnp.float32
BF16 = jnp.bfloat16

D_MODEL = 1024
DEPTH = 4
RWKV_WIDTH = 512
RWKV_HEAD = 64
RWKV_HEADS = 8
HEAD_PAIRS = RWKV_HEADS // 2
PAIR_W = 2 * RWKV_HEAD
LORA_PAD = 128
RWKV_COLS = 3 * RWKV_WIDTH + 64 + 64 + 128
S5_WIDTH = 512
S5_GROUP = 16
S5_GROUPS = 32
S5_STATE = 64
S5_LANES = S5_GROUPS * S5_STATE
S5_HALF_W = S5_WIDTH // 2
S5_HALF_L = S5_LANES // 2
IN_COLS = RWKV_COLS + S5_WIDTH
N_MEM = 256
X_HEADS = 4
X_HEAD_DIM = 256
D_FF = 2816
CONV_W = 3
RMS_EPS = 1e-6
LNX_EPS = 64e-5

ROW_TILE = 512
FFN_ROW_TILE = 256
PROMPT_CHUNK = 64
SHORT_CHUNK = 8
VMEM_LIMIT = 56 * 1024 * 1024


def _cparams(*sem):
    return pltpu.CompilerParams(dimension_semantics=sem, vmem_limit_bytes=VMEM_LIMIT)


def _const_spec(shape):
    nd = len(shape)
    return pl.BlockSpec(shape, lambda *_: (0,) * nd)


def _bdot(a, b):
    return jnp.dot(a.astype(BF16), b.astype(BF16), preferred_element_type=F32)


def _bdot_nt(a, b):
    return lax.dot_general(a.astype(BF16), b.astype(BF16), (((1,), (1,)), ((), ())),
                           preferred_element_type=F32)


def _bdot_tn(a, b):
    return lax.dot_general(a.astype(BF16), b.astype(BF16), (((0,), (0,)), ((), ())),
                           preferred_element_type=F32)


def _split3(x):
    hi = x.astype(BF16)
    r1 = x - hi.astype(F32)
    mid = r1.astype(BF16)
    lo = (r1 - mid.astype(F32)).astype(BF16)
    return hi, mid, lo


def _dot_exact_rhs(x, m_bf16):
    hi, mid, lo = _split3(x)
    d = functools.partial(jnp.dot, preferred_element_type=F32)
    return d(hi, m_bf16) + d(mid, m_bf16) + d(lo, m_bf16)


def _dot_exact_lhs(m_bf16, x):
    hi, mid, lo = _split3(x)
    d = functools.partial(jnp.dot, preferred_element_type=F32)
    return d(m_bf16, hi) + d(m_bf16, mid) + d(m_bf16, lo)


def _rms(x, gain):
    return x * lax.rsqrt(jnp.mean(x * x, axis=-1, keepdims=True) + RMS_EPS) * gain


def _rowmm_kernel(*refs, n_x, has_gain, has_resid):
    xs = refs[:n_x]
    ws = refs[n_x:2 * n_x]
    pos = 2 * n_x
    gain = refs[pos] if has_gain else None
    pos += int(has_gain)
    resid = refs[pos] if has_resid else None
    pos += int(has_resid)
    o_ref = refs[pos]
    acc = None
    for x_ref, w_ref in zip(xs, ws):
        x = x_ref[...]
        if has_gain:
            x = _rms(x, gain[...])
        d = jnp.dot(x.astype(BF16), w_ref[...], preferred_element_type=F32)
        acc = d if acc is None else acc + d
    if has_resid:
        acc = acc + resid[...]
    o_ref[...] = acc


def _rowmm(xs, ws, gain=None, resid=None, name="rowmm"):
    rows = xs[0].shape[0]
    n_out = ws[0].shape[1]
    tm = min(ROW_TILE, rows)
    in_specs = [pl.BlockSpec((tm, x.shape[1]), lambda i: (i, 0)) for x in xs]
    in_specs += [_const_spec(w.shape) for w in ws]
    args = list(xs) + list(ws)
    if gain is not None:
        in_specs.append(_const_spec(gain.shape))
        args.append(gain)
    if resid is not None:
        in_specs.append(pl.BlockSpec((tm, n_out), lambda i: (i, 0)))
        args.append(resid)
    return pl.pallas_call(
        functools.partial(_rowmm_kernel, n_x=len(xs), has_gain=gain is not None,
                          has_resid=resid is not None),
        grid=(rows // tm,),
        in_specs=in_specs,
        out_specs=pl.BlockSpec((tm, n_out), lambda i: (i, 0)),
        out_shape=jax.ShapeDtypeStruct((rows, n_out), F32),
        compiler_params=_cparams("parallel"),
        name=name,
    )(*args)


def _rownorm_kernel(x_ref, g_ref, o_ref):
    o_ref[...] = _rms(x_ref[...], g_ref[...])


def _rownorm(x, gain):
    rows, d = x.shape
    tm = min(ROW_TILE, rows)
    return pl.pallas_call(
        _rownorm_kernel,
        grid=(rows // tm,),
        in_specs=[pl.BlockSpec((tm, d), lambda i: (i, 0)), _const_spec(gain.shape)],
        out_specs=pl.BlockSpec((tm, d), lambda i: (i, 0)),
        out_shape=jax.ShapeDtypeStruct((rows, d), F32),
        compiler_params=_cparams("parallel"),
        name="final_norm",
    )(x, gain)


def _softplus(z):
    return jnp.maximum(z, 0.0) + jnp.log1p(jnp.exp(-jnp.abs(z)))


def _rwkv_prep_kernel(*refs, batch, has_vfirst):
    (p_ref, shift0_ref, mu_ref, w0_ref, ww2_ref, a0_ref, wa2_ref, wg2_ref, kk_ref, ka_ref,
     ones_ref) = refs[:11]
    pos = 11
    if has_vfirst:
        vf_ref, v0_ref, wv1_ref, wv2_ref = refs[pos:pos + 4]
        pos += 4
    (r_out, lw_out, k_out, v_out, kk_out, a_out, g_out, shift_out, carry) = refs[pos:pos + 9]

    @pl.when(pl.program_id(0) == 0)
    def _():
        carry[...] = shift0_ref[...]

    p = p_ref[...]
    tm = p.shape[0]
    if tm > batch:
        p_prev = jnp.concatenate([carry[...], p[:tm - batch]], axis=0)
    else:
        p_prev = carry[...]
    new_carry = p[tm - batch:]
    carry[...] = new_carry
    shift_out[...] = new_carry
    q = p + (p_prev - p) * mu_ref[...]

    rw = RWKV_WIDTH
    r = q[:, 0:rw]
    k = q[:, rw:2 * rw]
    v = q[:, 2 * rw:3 * rw]
    x_wa = q[:, 3 * rw:3 * rw + LORA_PAD]
    x_g = q[:, 3 * rw + LORA_PAD:3 * rw + 2 * LORA_PAD]

    w = -_softplus(-(w0_ref[...] + _bdot(jnp.tanh(x_wa), ww2_ref[...]))) - 0.5
    lw_out[...] = -jnp.exp(w)
    a = jax.nn.sigmoid(a0_ref[...] + _bdot(x_wa, wa2_ref[...]))
    g_out[...] = _bdot(jax.nn.sigmoid(x_g), wg2_ref[...])
    if has_vfirst:
        mix = jax.nn.sigmoid(v0_ref[...] + _bdot(_bdot(v, wv1_ref[...]), wv2_ref[...]))
        v = v + (vf_ref[...] - v) * mix
    kk = k * kk_ref[...]
    ss = _dot_exact_rhs(kk * kk, ones_ref[...])
    kk_out[...] = kk * lax.rsqrt(jnp.maximum(ss, 1e-24))
    k_out[...] = k * (1.0 + (a - 1.0) * ka_ref[...])
    r_out[...] = r
    v_out[...] = v
    a_out[...] = a


def _rwkv_prep(proj, shift0, v_first, lw, layer, batch):
    rows = proj.shape[0]
    tm = min(ROW_TILE, rows)
    row_spec = pl.BlockSpec((tm, RWKV_WIDTH), lambda i: (i, 0))
    args = [proj, shift0, lw["shift_mu"][layer], lw["rwkv_w0"][layer], lw["rwkv_w_w2p"][layer],
            lw["rwkv_a0"][layer], lw["rwkv_w_a2p"][layer], lw["rwkv_w_g2"][layer],
            lw["rwkv_k_k"][layer], lw["rwkv_k_a"][layer], lw["head_ones"]]
    in_specs = [pl.BlockSpec((tm, RWKV_COLS), lambda i: (i, 0))] + [_const_spec(a.shape) for a in args[1:]]
    if v_first is not None:
        extra = [v_first, lw["rwkv_v0"][layer - 1], lw["rwkv_w_v1p"][layer - 1], lw["rwkv_w_v2p"][layer - 1]]
        args += extra
        in_specs += [row_spec] + [_const_spec(a.shape) for a in extra[1:]]
    out_row = jax.ShapeDtypeStruct((rows, RWKV_WIDTH), F32)
    outs = pl.pallas_call(
        functools.partial(_rwkv_prep_kernel, batch=batch, has_vfirst=v_first is not None),
        grid=(rows // tm,),
        in_specs=in_specs,
        out_specs=[row_spec] * 7 + [_const_spec((batch, RWKV_COLS))],
        out_shape=[out_row] * 7 + [jax.ShapeDtypeStruct((batch, RWKV_COLS), F32)],
        scratch_shapes=[pltpu.VMEM((batch, RWKV_COLS), F32)],
        compiler_params=_cparams("arbitrary"),
        name="rwkv_prep",
    )(*args)
    return outs


def _rwkv_rec_kernel(r_ref, lw_ref, k_ref, v_ref, kk_ref, a_ref, g_ref, s0_ref, lnw_ref, lnb_ref,
                     rk_ref, ones_ref, o_ref, s_out_ref, s_scr, o_scr, *, chunk):
    c = pl.program_id(1)
    cc = chunk
    c2 = 2 * cc

    @pl.when(c == 0)
    def _():
        s_scr[...] = s0_ref[...]

    lw = lw_ref[...]
    row = lax.broadcasted_iota(jnp.int32, (cc, cc), 0)
    col = lax.broadcasted_iota(jnp.int32, (cc, cc), 1)
    tril = (col <= row).astype(BF16)
    cum = _dot_exact_lhs(tril, lw)
    g_in = jnp.exp(cum)
    g_ex = jnp.exp(cum - lw)
    g_inv = jnp.exp(-cum)
    cum_end = cum[cc - 1:cc, :]
    g_end = jnp.exp(cum_end)
    g_tail = jnp.exp(cum_end - cum)
    r = r_ref[...]
    k = k_ref[...]
    v = v_ref[...]
    kk = kk_ref[...]
    kka = kk * a_ref[...]
    a_hat = -kk * g_ex
    b_hat = kka * g_inv
    k_hat = k * g_inv
    r_hat = r * g_in
    b_tail = kka * g_tail
    k_tail = k * g_tail

    lane = lax.broadcasted_iota(jnp.int32, (1, PAIR_W), 1)
    first = lane < RWKV_HEAD
    row2 = lax.broadcasted_iota(jnp.int32, (c2, c2), 0)
    col2 = lax.broadcasted_iota(jnp.int32, (c2, c2), 1)
    same = (row2 >= cc) == (col2 >= cc)
    rr = jnp.where(row2 >= cc, row2 - cc, row2)
    cl = jnp.where(col2 >= cc, col2 - cc, col2)
    strict = same & (cl < rr)
    incl = same & (cl <= rr)
    eye = (row2 == col2).astype(F32)
    n_factors = max(1, math.ceil(math.log2(cc)))

    for p in range(HEAD_PAIRS):
        sl = slice(p * PAIR_W, (p + 1) * PAIR_W)

        def stack(x):
            xs = x[:, sl]
            return jnp.concatenate([jnp.where(first, xs, 0.0), jnp.where(first, 0.0, xs)], axis=0)

        a_s, r_s, b_s, k_s, v_s = stack(a_hat), stack(r_hat), stack(b_hat), stack(k_hat), stack(v)
        bt_s, kt_s = stack(b_tail), stack(k_tail)
        s_pair = s_scr[p]
        ar_s = jnp.concatenate([a_s, r_s], axis=0)
        if c2 % 128 == 0:
            gram = _bdot_nt(ar_s, jnp.concatenate([b_s, k_s], axis=0))
            g_ab, g_ak = gram[:c2, :c2], gram[:c2, c2:]
            g_rb, g_rk = gram[c2:, :c2], gram[c2:, c2:]
        else:
            g_ab, g_ak = _bdot_nt(a_s, b_s), _bdot_nt(a_s, k_s)
            g_rb, g_rk = _bdot_nt(r_s, b_s), _bdot_nt(r_s, k_s)
        l_ab = jnp.where(strict, g_ab, 0.0)
        l_ak = jnp.where(strict, g_ak, 0.0)
        m_rb = jnp.where(incl, g_rb, 0.0)
        m_rk = jnp.where(incl, g_rk, 0.0)
        inv = eye + l_ab
        pw = l_ab
        for _ in range(n_factors - 1):
            pw = _bdot(pw, pw)
            inv = inv + _bdot(inv, pw)
        ar_state = _bdot_nt(ar_s, s_pair)
        u_s = _bdot(inv, ar_state[:c2] + _bdot(l_ak, v_s))
        if c2 % 128 == 0:
            o_st = ar_state[c2:] + _bdot(jnp.concatenate([m_rb, m_rk], axis=1),
                                         jnp.concatenate([u_s, v_s], axis=0))
            upd = _bdot_tn(jnp.concatenate([u_s, v_s], axis=0), jnp.concatenate([bt_s, kt_s], axis=0))
        else:
            o_st = ar_state[c2:] + _bdot(m_rb, u_s) + _bdot(m_rk, v_s)
            upd = _bdot_tn(u_s, bt_s) + _bdot_tn(v_s, kt_s)
        o_scr[:, sl] = o_st[:cc] + o_st[cc:]
        s_scr[p] = s_pair * g_end[:, sl] + upd

    o = o_scr[...]
    ones = ones_ref[...]
    inv_n = 1.0 / RWKV_HEAD
    mean = _dot_exact_rhs(o, ones) * inv_n
    d = o - mean
    var = _dot_exact_rhs(d * d, ones) * inv_n
    on = d * lax.rsqrt(var + LNX_EPS) * lnw_ref[...] + lnb_ref[...]
    bonus = _dot_exact_rhs(r * k * rk_ref[...], ones) * v
    o_ref[...] = (on + bonus) * g_ref[...]

    @pl.when(c == pl.num_programs(1) - 1)
    def _():
        s_out_ref[...] = s_scr[...]


def _rwkv_rec(seqs, s0_bd, lw, layer, steps, batch, chunk):
    seq_spec = pl.BlockSpec((chunk, RWKV_WIDTH), lambda b, c: (c, b))
    st_spec = pl.BlockSpec((None, HEAD_PAIRS, PAIR_W, PAIR_W), lambda b, c: (b, 0, 0, 0))
    consts = [lw["rwkv_lnx_w"][layer], lw["rwkv_lnx_b"][layer], lw["rwkv_r_k"][layer], lw["head_ones"]]
    return pl.pallas_call(
        functools.partial(_rwkv_rec_kernel, chunk=chunk),
        grid=(batch, steps // chunk),
        in_specs=[seq_spec] * 7 + [st_spec] + [_const_spec(a.shape) for a in consts],
        out_specs=[seq_spec, st_spec],
        out_shape=[jax.ShapeDtypeStruct((steps, batch * RWKV_WIDTH), F32),
                   jax.ShapeDtypeStruct(s0_bd.shape, F32)],
        scratch_shapes=[pltpu.VMEM((HEAD_PAIRS, PAIR_W, PAIR_W), F32),
                        pltpu.VMEM((chunk, RWKV_WIDTH), F32)],
        compiler_params=_cparams("parallel", "arbitrary"),
        name="rwkv_rec",
    )(*seqs, s0_bd, *consts)


def _pair_blockdiag(state):
    b = state.shape[0]
    s = state.reshape(b, HEAD_PAIRS, 2, RWKV_HEAD, RWKV_HEAD)
    eye2 = jnp.eye(2, dtype=state.dtype)
    bd = s[:, :, :, :, None, :] * eye2[None, None, :, None, :, None]
    return bd.reshape(b, HEAD_PAIRS, PAIR_W, PAIR_W)


def _pair_unblock(bd):
    b = bd.shape[0]
    s = bd.reshape(b, HEAD_PAIRS, 2, RWKV_HEAD, 2, RWKV_HEAD)
    out = jnp.stack([s[:, :, 0, :, 0, :], s[:, :, 1, :, 1, :]], axis=2)
    return out.reshape(b, RWKV_HEADS, RWKV_HEAD, RWKV_HEAD)


def _s5_kernel(u0_ref, u1_ref, h0r_ref, h0i_ref, are_ref, aim_ref, ldt_ref, bre_ref, bim_ref,
               cre_ref, cim_ref, d_ref, wglu_ref, bglu_ref,
               o_ref, hr_out, hi_out, hr_c, hi_c, hre, him, *, batch):
    @pl.when(pl.program_id(0) == 0)
    def _():
        hr_c[...] = h0r_ref[...]
        hi_c[...] = h0i_ref[...]

    a_re = are_ref[...]
    a_im = aim_ref[...]
    dt = jnp.exp(ldt_ref[...])
    mag = jnp.exp(a_re * dt)
    ab_re = mag * jnp.cos(a_im * dt)
    ab_im = mag * jnp.sin(a_im * dt)
    den = a_re * a_re + a_im * a_im
    nr = ab_re - 1.0
    cf_re = (nr * a_re + ab_im * a_im) / den
    cf_im = (ab_im * a_re - nr * a_im) / den

    us = (u0_ref[...], u1_ref[...])
    tm = us[0].shape[0]
    for hf in range(2):
        ls = slice(hf * S5_HALF_L, (hf + 1) * S5_HALF_L)
        ub = us[hf].astype(BF16)
        pr = jnp.dot(ub, bre_ref[hf], preferred_element_type=F32)
        pi = jnp.dot(ub, bim_ref[hf], preferred_element_type=F32)
        hre[:, ls] = cf_re[:, ls] * pr - cf_im[:, ls] * pi
        him[:, ls] = cf_re[:, ls] * pi + cf_im[:, ls] * pr

    n_steps = tm // batch
    if n_steps <= 8:
        hr = hr_c[...]
        hi = hi_c[...]
        for s in range(n_steps):
            rows = slice(s * batch, (s + 1) * batch)
            nhr = ab_re * hr - ab_im * hi + hre[rows, :]
            nhi = ab_re * hi + ab_im * hr + him[rows, :]
            hre[rows, :] = nhr
            him[rows, :] = nhi
            hr, hi = nhr, nhi
        hr_c[...] = hr
        hi_c[...] = hi
    else:
        lane_w = 512
        for lc in range(S5_LANES // lane_w):
            ls = slice(lc * lane_w, (lc + 1) * lane_w)
            abr = jnp.broadcast_to(ab_re[:, ls], (batch, lane_w))
            abi = jnp.broadcast_to(ab_im[:, ls], (batch, lane_w))

            def body(s, carry, ls=ls, abr=abr, abi=abi):
                hr, hi = carry
                rows = pl.ds(pl.multiple_of(s * batch, batch), batch)
                nhr = abr * hr - abi * hi + hre[rows, ls]
                nhi = abr * hi + abi * hr + him[rows, ls]
                hre[rows, ls] = nhr
                him[rows, ls] = nhi
                return nhr, nhi

            hr, hi = lax.fori_loop(0, n_steps, body, (hr_c[:, ls], hi_c[:, ls]), unroll=8)
            hr_c[:, ls] = hr
            hi_c[:, ls] = hi

    hr_out[...] = hr_c[...]
    hi_out[...] = hi_c[...]

    for hf in range(2):
        ls = slice(hf * S5_HALF_L, (hf + 1) * S5_HALF_L)
        cs = slice(hf * S5_HALF_W, (hf + 1) * S5_HALF_W)
        y = (jnp.dot(hre[:, ls].astype(BF16), cre_ref[hf], preferred_element_type=F32)
             - jnp.dot(him[:, ls].astype(BF16), cim_ref[hf], preferred_element_type=F32)
             + d_ref[:, cs] * us[hf])
        o_ref[:, cs] = jax.nn.gelu(y, approximate=True)
    y = o_ref[...]
    o_ref[...] = y * jax.nn.sigmoid(_bdot(y, wglu_ref[...]) + bglu_ref[...])


def _s5(proj, h0r, h0i, lw, layer, batch):
    rows = proj.shape[0]
    tm = min(ROW_TILE, rows)
    u_blk = RWKV_COLS // S5_HALF_W
    consts = [h0r, h0i, lw["s5_a_re"][layer], lw["s5_a_im"][layer], lw["s5_log_dt"][layer],
              lw["s5_b_re"][layer], lw["s5_b_im"][layer], lw["s5_c_re"][layer], lw["s5_c_im"][layer],
              lw["s5_d"][layer], lw["s5_w_glu"][layer], lw["s5_b_glu"][layer]]
    st_shape = jax.ShapeDtypeStruct((batch, S5_LANES), F32)
    return pl.pallas_call(
        functools.partial(_s5_kernel, batch=batch),
        grid=(rows // tm,),
        in_specs=[pl.BlockSpec((tm, S5_HALF_W), lambda i: (i, u_blk)),
                  pl.BlockSpec((tm, S5_HALF_W), lambda i: (i, u_blk + 1))]
                 + [_const_spec(a.shape) for a in consts],
        out_specs=[pl.BlockSpec((tm, S5_WIDTH), lambda i: (i, 0)),
                   _const_spec((batch, S5_LANES)), _const_spec((batch, S5_LANES))],
        out_shape=[jax.ShapeDtypeStruct((rows, S5_WIDTH), F32), st_shape, st_shape],
        scratch_shapes=[pltpu.VMEM((batch, S5_LANES), F32), pltpu.VMEM((batch, S5_LANES), F32),
                        pltpu.VMEM((tm, S5_LANES), F32), pltpu.VMEM((tm, S5_LANES), F32)],
        compiler_params=_cparams("arbitrary"),
        name="s5",
    )(proj, proj, *consts)


def _attn_kernel(q_ref, k_ref, v_ref, o_ref):
    scale = X_HEAD_DIM ** -0.5
    for h in range(X_HEADS):
        sl = slice(h * X_HEAD_DIM, (h + 1) * X_HEAD_DIM)
        s = _bdot_nt(q_ref[:, sl], k_ref[:, sl]) * scale
        e = jnp.exp(s - jnp.max(s, axis=-1, keepdims=True))
        pr = e / jnp.sum(e, axis=-1, keepdims=True)
        o_ref[:, sl] = _bdot(pr, v_ref[:, sl])


def _attn(q, mem_k, mem_v, steps, batch):
    tq = min(ROW_TILE, steps)
    q_spec = pl.BlockSpec((tq, D_MODEL), lambda b, i: (i, b))
    m_spec = pl.BlockSpec((None, N_MEM, D_MODEL), lambda b, i: (b, 0, 0))
    return pl.pallas_call(
        _attn_kernel,
        grid=(batch, steps // tq),
        in_specs=[q_spec, m_spec, m_spec],
        out_specs=q_spec,
        out_shape=jax.ShapeDtypeStruct((steps, batch * D_MODEL), F32),
        compiler_params=_cparams("parallel", "parallel"),
        name="mem_attn",
    )(q, mem_k, mem_v)


def _ffn_kernel(x_ref, buf0_ref, gain_ref, wg_ref, wu_ref, cw_ref, cb_ref, wd_ref,
                o_ref, buf_out, carry, *, batch):
    @pl.when(pl.program_id(0) == 0)
    def _():
        carry[...] = buf0_ref[...]

    x = x_ref[...]
    tm = x.shape[0]
    h = _rms(x, gain_ref[...]).astype(BF16)
    gt = jnp.dot(h, wg_ref[...], preferred_element_type=F32)
    up = jnp.dot(h, wu_ref[...], preferred_element_type=F32)
    padded = jnp.concatenate([carry[...], gt], axis=0)
    conv = cb_ref[...]
    for i in range(CONV_W):
        conv = conv + cw_ref[i:i + 1, :] * padded[i * batch:i * batch + tm]
    new_carry = padded[tm:]
    carry[...] = new_carry
    buf_out[...] = new_carry
    act = jax.nn.silu(conv) * up
    o_ref[...] = x + jnp.dot(act.astype(BF16), wd_ref[...], preferred_element_type=F32)


def _ffn(x, buf0, lw, layer, batch):
    rows = x.shape[0]
    tm = min(FFN_ROW_TILE, rows)
    tm = max(tm, (CONV_W - 1) * batch)
    consts = [buf0, lw["norm_ffn"][layer], lw["w_gate"][layer], lw["w_up"][layer],
              lw["ffn_conv_w"][layer], lw["ffn_conv_b"][layer], lw["w_down"][layer]]
    nbuf = (CONV_W - 1) * batch
    return pl.pallas_call(
        functools.partial(_ffn_kernel, batch=batch),
        grid=(rows // tm,),
        in_specs=[pl.BlockSpec((tm, D_MODEL), lambda i: (i, 0))] + [_const_spec(a.shape) for a in consts],
        out_specs=[pl.BlockSpec((tm, D_MODEL), lambda i: (i, 0)), _const_spec((nbuf, D_FF))],
        out_shape=[jax.ShapeDtypeStruct((rows, D_MODEL), F32), jax.ShapeDtypeStruct((nbuf, D_FF), F32)],
        scratch_shapes=[pltpu.VMEM((nbuf, D_FF), F32)],
        compiler_params=_cparams("arbitrary"),
        name="conv_ffn",
    )(x, *consts)


def _s5_in_blockdiag(b):
    l = b.shape[0]
    b = b.reshape(l, 2, S5_GROUPS // 2, S5_STATE, S5_GROUP)
    eye = jnp.eye(S5_GROUPS // 2, dtype=b.dtype)
    m = jnp.einsum('lfgph,gk->lfghkp', b, eye)
    return m.reshape(l, 2, S5_HALF_W, S5_HALF_L).astype(BF16)


def _s5_out_blockdiag(c):
    l = c.shape[0]
    c = c.reshape(l, 2, S5_GROUPS // 2, S5_GROUP, S5_STATE)
    eye = jnp.eye(S5_GROUPS // 2, dtype=c.dtype)
    m = jnp.einsum('lfgnp,gk->lfgpkn', c, eye)
    return m.reshape(l, 2, S5_HALF_L, S5_HALF_W).astype(BF16)


def _prep_weights(p):
    l = DEPTH
    row = lambda a: a.reshape(a.shape[0], 1, -1)
    lw = {}
    for name in ("norm_mix", "shift_mu", "rwkv_w0", "rwkv_a0", "rwkv_v0", "rwkv_k_k", "rwkv_k_a",
                 "rwkv_r_k", "rwkv_lnx_w", "rwkv_lnx_b", "s5_d", "s5_b_glu", "norm_cross", "norm_ffn",
                 "ffn_conv_b", "s5_a_re", "s5_a_im"):
        lw[name] = row(p[name])
    lw["s5_log_dt"] = row(jnp.repeat(p["s5_log_dt"], S5_STATE, axis=-1))
    lw["norm_final"] = p["norm_final"].reshape(1, -1)
    lw["ffn_conv_w"] = p["ffn_conv_w"]
    for name in ("w_in", "rwkv_w_g2", "s5_w_glu", "w_cq", "w_ck", "w_cv", "w_co", "w_gate", "w_up",
                 "w_down"):
        lw[name] = p[name].astype(BF16)
    w_out = p["w_out"].astype(BF16)
    lw["w_out_rw"] = w_out[:, :RWKV_WIDTH]
    lw["w_out_s5"] = w_out[:, RWKV_WIDTH:]
    z64 = jnp.zeros((l, LORA_PAD - 64, RWKV_WIDTH), F32)
    lw["rwkv_w_w2p"] = jnp.concatenate([p["rwkv_w_w2"], z64], axis=1).astype(BF16)
    lw["rwkv_w_a2p"] = jnp.concatenate([z64, p["rwkv_w_a2"]], axis=1).astype(BF16)
    v_lora = p["rwkv_w_v1"].shape[-1]
    lw["rwkv_w_v1p"] = jnp.pad(p["rwkv_w_v1"], ((0, 0), (0, 0), (0, LORA_PAD - v_lora))).astype(BF16)
    lw["rwkv_w_v2p"] = jnp.pad(p["rwkv_w_v2"], ((0, 0), (0, LORA_PAD - v_lora), (0, 0))).astype(BF16)
    head = jnp.arange(RWKV_WIDTH) // RWKV_HEAD
    lw["head_ones"] = (head[:, None] == head[None, :]).astype(BF16)
    lw["s5_b_re"] = _s5_in_blockdiag(p["s5_b_re"])
    lw["s5_b_im"] = _s5_in_blockdiag(p["s5_b_im"])
    lw["s5_c_re"] = _s5_out_blockdiag(p["s5_c_re"])
    lw["s5_c_im"] = _s5_out_blockdiag(p["s5_c_im"])
    return lw


def _run_trunk(x, mem_k, mem_v, st_rwkv, st_shift, st_re, st_im, st_conv, lw, steps, batch):
    rows = steps * batch
    if steps >= PROMPT_CHUNK:
        chunk, steps_pad = PROMPT_CHUNK, steps
    else:
        chunk = SHORT_CHUNK
        steps_pad = -(-steps // chunk) * chunk
    v_first = None
    new_rw, new_shift, new_re, new_im, new_conv = [], [], [], [], []
    for l in range(DEPTH):
        proj = _rowmm([x], [lw["w_in"][l]], gain=lw["norm_mix"][l], name="in_proj")
        *seqs, sh = _rwkv_prep(proj, st_shift[l], v_first, lw, l, batch)
        if l == 0:
            v_first = seqs[3]
        seqs = [s.reshape(steps, batch * RWKV_WIDTH) for s in seqs]
        if steps_pad != steps:
            seqs = [jnp.pad(s, ((0, steps_pad - steps), (0, 0))) for s in seqs]
        o_rw, s_rw = _rwkv_rec(seqs, _pair_blockdiag(st_rwkv[l]), lw, l, steps_pad, batch, chunk)
        o_rw = o_rw[:steps].reshape(rows, RWKV_WIDTH)
        o_s5, hr, hi = _s5(proj, st_re[l], st_im[l], lw, l, batch)
        x = _rowmm([o_rw, o_s5], [lw["w_out_rw"][l], lw["w_out_s5"][l]], resid=x, name="mix_out")
        q = _rowmm([x], [lw["w_cq"][l]], gain=lw["norm_cross"][l], name="cross_q")
        att = _attn(q.reshape(steps, batch * D_MODEL), mem_k[l], mem_v[l], steps, batch)
        x = _rowmm([att.reshape(rows, D_MODEL)], [lw["w_co"][l]], resid=x, name="cross_o")
        x, cb = _ffn(x, st_conv[l], lw, l, batch)
        new_rw.append(_pair_unblock(s_rw))
        new_shift.append(sh)
        new_re.append(hr)
        new_im.append(hi)
        new_conv.append(cb)
    y = _rownorm(x, lw["norm_final"])
    return (y, jnp.stack(new_rw), jnp.stack(new_shift), jnp.stack(new_re), jnp.stack(new_im),
            jnp.stack(new_conv))


def _group(x, mem_k, mem_v, st_rwkv, st_shift, st_re, st_im, st_conv, lw):
    b, t, _ = x.shape
    xt = jnp.swapaxes(x, 0, 1).reshape(t * b, D_MODEL)
    conv_t = jnp.swapaxes(st_conv, 1, 2).reshape(DEPTH, (CONV_W - 1) * b, D_FF)
    y, rw, sh, re, im, cv = _run_trunk(
        xt, mem_k.reshape(DEPTH, b, N_MEM, D_MODEL), mem_v.reshape(DEPTH, b, N_MEM, D_MODEL),
        st_rwkv, st_shift, st_re.reshape(DEPTH, b, S5_LANES), st_im.reshape(DEPTH, b, S5_LANES),
        conv_t, lw, t, b)
    y = jnp.swapaxes(y.reshape(t, b, D_MODEL), 0, 1)
    cv = jnp.swapaxes(cv.reshape(DEPTH, CONV_W - 1, b, D_FF), 1, 2)
    return (y, rw, sh, re.reshape(DEPTH, b, S5_GROUPS, S5_STATE),
            im.reshape(DEPTH, b, S5_GROUPS, S5_STATE), cv)


def kernel(x_prompt, x_sample, mem_prompt, state_rwkv, state_shift, state_s5_re, state_s5_im, state_ffn_conv, cache_mem_k, cache_mem_v, norm_mix, w_in, shift_mu, rwkv_w0, rwkv_w_w2, rwkv_a0, rwkv_w_a2, rwkv_v0, rwkv_w_v1, rwkv_w_v2, rwkv_w_g2, rwkv_k_k, rwkv_k_a, rwkv_r_k, rwkv_lnx_w, rwkv_lnx_b, s5_a_re, s5_a_im, s5_log_dt, s5_b_re, s5_b_im, s5_c_re, s5_c_im, s5_d, s5_w_glu, s5_b_glu, w_out, norm_cross, w_cq, w_ck, w_cv, w_co, norm_ffn, w_gate, w_up, ffn_conv_w, ffn_conv_b, w_down, norm_final):
    lw = _prep_weights(dict(
        norm_mix=norm_mix, w_in=w_in, shift_mu=shift_mu, rwkv_w0=rwkv_w0, rwkv_w_w2=rwkv_w_w2,
        rwkv_a0=rwkv_a0, rwkv_w_a2=rwkv_w_a2, rwkv_v0=rwkv_v0, rwkv_w_v1=rwkv_w_v1,
        rwkv_w_v2=rwkv_w_v2, rwkv_w_g2=rwkv_w_g2, rwkv_k_k=rwkv_k_k, rwkv_k_a=rwkv_k_a,
        rwkv_r_k=rwkv_r_k, rwkv_lnx_w=rwkv_lnx_w, rwkv_lnx_b=rwkv_lnx_b, s5_a_re=s5_a_re,
        s5_a_im=s5_a_im, s5_log_dt=s5_log_dt, s5_b_re=s5_b_re, s5_b_im=s5_b_im, s5_c_re=s5_c_re,
        s5_c_im=s5_c_im, s5_d=s5_d, s5_w_glu=s5_w_glu, s5_b_glu=s5_b_glu, w_out=w_out,
        norm_cross=norm_cross, w_cq=w_cq, w_ck=w_ck, w_cv=w_cv, w_co=w_co, norm_ffn=norm_ffn,
        w_gate=w_gate, w_up=w_up, ffn_conv_w=ffn_conv_w, ffn_conv_b=ffn_conv_b, w_down=w_down,
        norm_final=norm_final))
    bp = x_prompt.shape[0]
    mem_rows = mem_prompt.reshape(bp * N_MEM, D_MODEL)
    p_mem_k = jnp.stack([_rowmm([mem_rows], [lw["w_ck"][l]], name="mem_k") for l in range(DEPTH)])
    p_mem_v = jnp.stack([_rowmm([mem_rows], [lw["w_cv"][l]], name="mem_v") for l in range(DEPTH)])
    p_mem_k = p_mem_k.reshape(DEPTH, bp, N_MEM, X_HEADS, X_HEAD_DIM)
    p_mem_v = p_mem_v.reshape(DEPTH, bp, N_MEM, X_HEADS, X_HEAD_DIM)
    z_rw = jnp.zeros((DEPTH, bp, RWKV_HEADS, RWKV_HEAD, RWKV_HEAD), F32)
    z_shift = jnp.zeros((DEPTH, bp, RWKV_COLS), F32)
    z_s5 = jnp.zeros((DEPTH, bp, S5_GROUPS, S5_STATE), F32)
    z_conv = jnp.zeros((DEPTH, bp, CONV_W - 1, D_FF), F32)
    y_prompt, p_rwkv, p_shift, p_re, p_im, p_conv = _group(
        x_prompt, p_mem_k, p_mem_v, z_rw, z_shift, z_s5, z_s5, z_conv, lw)
    y_sample, s_rwkv, s_shift, s_re, s_im, s_conv = _group(
        x_sample, cache_mem_k, cache_mem_v, state_rwkv, state_shift, state_s5_re, state_s5_im,
        state_ffn_conv, lw)
    return (y_prompt, y_sample, p_rwkv, p_shift, p_re, p_im, p_conv, p_mem_k, p_mem_v,
            s_rwkv, s_shift, s_re, s_im, s_conv)
```

```python
import functools
import math

import jax
import jax.numpy as jnp
from jax import lax
from jax.experimental import pallas as pl
from jax.experimental.pallas import tpu as pltpu

F32 = jnp.float32
BF16 = jnp.bfloat16

D_MODEL = 1024
DEPTH = 4
RWKV_WIDTH = 512
RWKV_HEAD = 64
RWKV_HEADS = 8
HEAD_PAIRS = RWKV_HEADS // 2
LANES = 128
PAIR_W = 2 * RWKV_HEAD
LORA_PAD = 128
RWKV_COLS = 3 * RWKV_WIDTH + 64 + 64 + 128
S5_WIDTH = 512
S5_GROUP = 16
S5_GROUPS = 32
S5_STATE = 64
S5_LANES = S5_GROUPS * S5_STATE
S5_HALF_W = S5_WIDTH // 2
S5_HALF_L = S5_LANES // 2
IN_COLS = RWKV_COLS + S5_WIDTH
N_MEM = 256
X_HEADS = 4
X_HEAD_DIM = 256
D_FF = 2816
CONV_W = 3
RMS_EPS = 1e-6
LNX_EPS = 64e-5

ROW_TILE = 512
FFN_ROW_TILE = 256
ATTN_ROW_TILE = 1024
LONG_SEQ = 64
LONG_CHUNK = 64
SHORT_CHUNK = 8
SHORT_GROUP = 8
VMEM_LIMIT = 56 * 1024 * 1024


def _cparams(*sem):
    return pltpu.CompilerParams(dimension_semantics=sem, vmem_limit_bytes=VMEM_LIMIT)


def _const_spec(shape):
    nd = len(shape)
    return pl.BlockSpec(shape, lambda *_: (0,) * nd, pipeline_mode=pl.Buffered(1))


def _bdot(a, b):
    return jnp.dot(a.astype(BF16), b.astype(BF16), preferred_element_type=F32)


def _bdot_nt(a, b):
    return lax.dot_general(a.astype(BF16), b.astype(BF16), (((1,), (1,)), ((), ())),
                           preferred_element_type=F32)


def _bdot_tn(a, b):
    return lax.dot_general(a.astype(BF16), b.astype(BF16), (((0,), (0,)), ((), ())),
                           preferred_element_type=F32)


def _split3(x):
    hi = x.astype(BF16)
    r1 = x - hi.astype(F32)
    mid = r1.astype(BF16)
    lo = (r1 - mid.astype(F32)).astype(BF16)
    return hi, mid, lo


def _dot_exact_rhs(x, m_bf16):
    hi, mid, lo = _split3(x)
    d = functools.partial(jnp.dot, preferred_element_type=F32)
    return d(hi, m_bf16) + d(mid, m_bf16) + d(lo, m_bf16)


def _dot_exact_lhs(m_bf16, x):
    hi, mid, lo = _split3(x)
    d = functools.partial(jnp.dot, preferred_element_type=F32)
    return d(m_bf16, hi) + d(m_bf16, mid) + d(m_bf16, lo)


def _rms(x, gain):
    return x * lax.rsqrt(jnp.mean(x * x, axis=-1, keepdims=True) + RMS_EPS) * gain


def _seq_rows(seq, steps, batch):
    return pl.ds(seq, steps, stride=batch)


def _lane_tiles(width):
    return [slice(j * LANES, (j + 1) * LANES) for j in range(width // LANES)]


def _stage(scr, x):
    for j, sl in enumerate(_lane_tiles(x.shape[1])):
        scr[j] = x[:, sl]


def _to_time_major_kernel(x_ref, o_ref, scr, *, batch, steps):
    tiles = _lane_tiles(x_ref.shape[2])
    for b in range(batch):
        for j, sl in enumerate(tiles):
            scr[j, _seq_rows(b, steps, batch), :] = x_ref[b, :, sl]
    for j, sl in enumerate(tiles):
        o_ref[:, sl] = scr[j]


def _to_time_major(x):
    b, t, d = x.shape
    tt = ROW_TILE // b
    return pl.pallas_call(
        functools.partial(_to_time_major_kernel, batch=b, steps=tt),
        grid=(t // tt,),
        in_specs=[pl.BlockSpec((b, tt, d), lambda i: (0, i, 0))],
        out_specs=pl.BlockSpec((tt * b, d), lambda i: (i, 0)),
        out_shape=jax.ShapeDtypeStruct((t * b, d), F32),
        scratch_shapes=[pltpu.VMEM((d // LANES, tt * b, LANES), F32)],
        compiler_params=_cparams("parallel"),
        name="to_time_major",
    )(x)


def _final_norm_long_kernel(x_ref, g_ref, o_ref, scr, *, batch, steps):
    _stage(scr, _rms(x_ref[...], g_ref[...]))
    for b in range(batch):
        for j, sl in enumerate(_lane_tiles(x_ref.shape[1])):
            o_ref[b, :, sl] = scr[j, _seq_rows(b, steps, batch), :]


def _final_norm_long(x, gain, batch):
    rows, d = x.shape
    t = rows // batch
    tt = ROW_TILE // batch
    return pl.pallas_call(
        functools.partial(_final_norm_long_kernel, batch=batch, steps=tt),
        grid=(t // tt,),
        in_specs=[pl.BlockSpec((tt * batch, d), lambda i: (i, 0)), _const_spec(gain.shape)],
        out_specs=pl.BlockSpec((batch, tt, d), lambda i: (0, i, 0)),
        out_shape=jax.ShapeDtypeStruct((batch, t, d), F32),
        scratch_shapes=[pltpu.VMEM((d // LANES, tt * batch, LANES), F32)],
        compiler_params=_cparams("parallel"),
        name="final_norm",
    )(x, gain)


def _rowmm_kernel(*refs, n_x, has_gain, has_resid):
    xs = refs[:n_x]
    ws = refs[n_x:2 * n_x]
    pos = 2 * n_x
    gain = refs[pos] if has_gain else None
    pos += int(has_gain)
    resid = refs[pos] if has_resid else None
    pos += int(has_resid)
    o_ref = refs[pos]
    acc = None
    for x_ref, w_ref in zip(xs, ws):
        x = x_ref[...]
        if has_gain:
            x = _rms(x, gain[...])
        d = jnp.dot(x.astype(BF16), w_ref[...], preferred_element_type=F32)
        acc = d if acc is None else acc + d
    if has_resid:
        acc = acc + resid[...]
    o_ref[...] = acc


def _rowmm(xs, ws, gain=None, resid=None, name="rowmm"):
    rows = xs[0].shape[0]
    n_out = ws[0].shape[1]
    tm = min(ROW_TILE, rows)
    in_specs = [pl.BlockSpec((tm, x.shape[1]), lambda i: (i, 0)) for x in xs]
    in_specs += [_const_spec(w.shape) for w in ws]
    args = list(xs) + list(ws)
    if gain is not None:
        in_specs.append(_const_spec(gain.shape))
        args.append(gain)
    if resid is not None:
        in_specs.append(pl.BlockSpec((tm, n_out), lambda i: (i, 0)))
        args.append(resid)
    return pl.pallas_call(
        functools.partial(_rowmm_kernel, n_x=len(xs), has_gain=gain is not None,
                          has_resid=resid is not None),
        grid=(rows // tm,),
        in_specs=in_specs,
        out_specs=pl.BlockSpec((tm, n_out), lambda i: (i, 0)),
        out_shape=jax.ShapeDtypeStruct((rows, n_out), F32),
        compiler_params=_cparams("parallel"),
        name=name,
    )(*args)


def _rownorm_kernel(x_ref, g_ref, o_ref):
    o_ref[...] = _rms(x_ref[...], g_ref[...])


def _rownorm(x, gain):
    rows, d = x.shape
    tm = min(ROW_TILE, rows)
    return pl.pallas_call(
        _rownorm_kernel,
        grid=(rows // tm,),
        in_specs=[pl.BlockSpec((tm, d), lambda i: (i, 0)), _const_spec(gain.shape)],
        out_specs=pl.BlockSpec((tm, d), lambda i: (i, 0)),
        out_shape=jax.ShapeDtypeStruct((rows, d), F32),
        compiler_params=_cparams("parallel"),
        name="final_norm",
    )(x, gain)


def _softplus(z):
    return jnp.maximum(z, 0.0) + jnp.log1p(jnp.exp(-jnp.abs(z)))


def _rwkv_prep_kernel(*refs, batch, has_vfirst):
    (p_ref, shift0_ref, mu_ref, w0_ref, ww2_ref, a0_ref, wa2_ref, wg2_ref, kk_ref, ka_ref,
     ones_ref) = refs[:11]
    pos = 11
    if has_vfirst:
        vf_ref, v0_ref, wv1_ref, wv2_ref = refs[pos:pos + 4]
        pos += 4
    (r_out, lw_out, k_out, v_out, kk_out, a_out, g_out, shift_out, carry) = refs[pos:pos + 9]

    @pl.when(pl.program_id(0) == 0)
    def _():
        carry[...] = shift0_ref[...]

    p = p_ref[...]
    tm = p.shape[0]
    if tm > batch:
        p_prev = jnp.concatenate([carry[...], p[:tm - batch]], axis=0)
    else:
        p_prev = carry[...]
    new_carry = p[tm - batch:]
    carry[...] = new_carry
    shift_out[...] = new_carry
    q = p + (p_prev - p) * mu_ref[...]

    rw = RWKV_WIDTH
    r = q[:, 0:rw]
    k = q[:, rw:2 * rw]
    v = q[:, 2 * rw:3 * rw]
    x_wa = q[:, 3 * rw:3 * rw + LORA_PAD]
    x_g = q[:, 3 * rw + LORA_PAD:3 * rw + 2 * LORA_PAD]

    w = -_softplus(-(w0_ref[...] + _bdot(jnp.tanh(x_wa), ww2_ref[...]))) - 0.5
    lw_out[...] = -jnp.exp(w)
    a = jax.nn.sigmoid(a0_ref[...] + _bdot(x_wa, wa2_ref[...]))
    g_out[...] = _bdot(jax.nn.sigmoid(x_g), wg2_ref[...])
    if has_vfirst:
        mix = jax.nn.sigmoid(v0_ref[...] + _bdot(_bdot(v, wv1_ref[...]), wv2_ref[...]))
        v = v + (vf_ref[...] - v) * mix
    kk = k * kk_ref[...]
    ss = _dot_exact_rhs(kk * kk, ones_ref[...])
    kk_out[...] = kk * lax.rsqrt(jnp.maximum(ss, 1e-24))
    k_out[...] = k * (1.0 + (a - 1.0) * ka_ref[...])
    r_out[...] = r
    v_out[...] = v
    a_out[...] = a


def _rwkv_prep(proj, shift0, v_first, lw, layer, batch):
    rows = proj.shape[0]
    tm = min(ROW_TILE, rows)
    row_spec = pl.BlockSpec((tm, RWKV_WIDTH), lambda i: (i, 0))
    args = [proj, shift0, lw["shift_mu"][layer], lw["rwkv_w0"][layer], lw["rwkv_w_w2p"][layer],
            lw["rwkv_a0"][layer], lw["rwkv_w_a2p"][layer], lw["rwkv_w_g2"][layer],
            lw["rwkv_k_k"][layer], lw["rwkv_k_a"][layer], lw["head_ones"]]
    in_specs = [pl.BlockSpec((tm, RWKV_COLS), lambda i: (i, 0))] + [_const_spec(a.shape) for a in args[1:]]
    if v_first is not None:
        extra = [v_first, lw["rwkv_v0"][layer - 1], lw["rwkv_w_v1p"][layer - 1], lw["rwkv_w_v2p"][layer - 1]]
        args += extra
        in_specs += [row_spec] + [_const_spec(a.shape) for a in extra[1:]]
    out_row = jax.ShapeDtypeStruct((rows, RWKV_WIDTH), F32)
    outs = pl.pallas_call(
        functools.partial(_rwkv_prep_kernel, batch=batch, has_vfirst=v_first is not None),
        grid=(rows // tm,),
        in_specs=in_specs,
        out_specs=[row_spec] * 7 + [pl.BlockSpec((batch, RWKV_COLS), lambda i: (0, 0))],
        out_shape=[out_row] * 7 + [jax.ShapeDtypeStruct((batch, RWKV_COLS), F32)],
        scratch_shapes=[pltpu.VMEM((batch, RWKV_COLS), F32)],
        compiler_params=_cparams("arbitrary"),
        name="rwkv_prep",
    )(*args)
    return outs


def _rwkv_rec_kernel(r_ref, lw_ref, k_ref, v_ref, kk_ref, a_ref, g_ref, s0_ref, lnw_ref, lnb_ref,
                     rk_ref, ones_ref, o_ref, s_out_ref,
                     s_scr, ar_scr, bk_scr, v_scr, tl_scr, p_scr, lak_scr, mrb_scr, mrk_scr,
                     rb_scr, r32_scr, aro_scr, o_scr, gend_scr, stage_scr, *, chunk, nseq, by_rows):
    c = pl.program_id(1)
    cc = chunk
    c2 = 2 * cc
    fused = c2 % 128 == 0

    @pl.when(c == 0)
    def _():
        s_scr[...] = s0_ref[...]

    lane = lax.broadcasted_iota(jnp.int32, (1, PAIR_W), 1)
    first = lane < RWKV_HEAD
    row2 = lax.broadcasted_iota(jnp.int32, (c2, c2), 0)
    col2 = lax.broadcasted_iota(jnp.int32, (c2, c2), 1)
    same = (row2 >= cc) == (col2 >= cc)
    rr = jnp.where(row2 >= cc, row2 - cc, row2)
    cl = jnp.where(col2 >= cc, col2 - cc, col2)
    strict = same & (cl < rr)
    incl = same & (cl <= rr)
    n_factors = max(1, math.ceil(math.log2(cc)))
    problems = [(b, p) for b in range(nseq) for p in range(HEAD_PAIRS)]

    def stack(xs):
        return jnp.concatenate([jnp.where(first, xs, 0.0), jnp.where(first, 0.0, xs)],
                               axis=0).astype(BF16)

    lw = lw_ref[...]
    k = k_ref[...]
    v = v_ref[...]
    if by_rows:
        cum = lw
        shift = nseq
        while shift < cc * nseq:
            cum = cum + jnp.concatenate([jnp.zeros((shift, RWKV_WIDTH), F32), cum[:-shift]], axis=0)
            shift *= 2
        cum_end = cum[(cc - 1) * nseq:]
        for b in range(nseq):
            gend_scr[b] = jnp.exp(cum_end[b:b + 1])
        cum_end = jnp.broadcast_to(cum_end[None], (cc, nseq, RWKV_WIDTH)).reshape(cc * nseq, RWKV_WIDTH)
    else:
        row = lax.broadcasted_iota(jnp.int32, (cc, cc), 0)
        col = lax.broadcasted_iota(jnp.int32, (cc, cc), 1)
        cum = _dot_exact_lhs((col <= row).astype(BF16), lw)
        cum_end = cum[cc - 1:cc, :]
        g_end = jnp.exp(cum_end)
        for b in range(nseq):
            gend_scr[b] = g_end[:, b * RWKV_WIDTH:(b + 1) * RWKV_WIDTH]
    g_inv = jnp.exp(-cum)
    g_tail = jnp.exp(cum_end - cum)
    kk = kk_ref[...]
    kka = kk * a_ref[...]
    operands = [-kk * jnp.exp(cum - lw), r_ref[...] * jnp.exp(cum),
                kka * g_inv, k * g_inv,
                kka * g_tail, k * g_tail, v]
    if by_rows:
        for i, x in enumerate(operands):
            _stage(stage_scr.at[i], x)

    def operand(i, b, p):
        if by_rows:
            return stage_scr[i, p, _seq_rows(b, cc, nseq), :]
        lo = b * RWKV_WIDTH + p * PAIR_W
        return operands[i][:, lo:lo + PAIR_W]

    for b, p in problems:
        ar_scr[b, p, :c2] = stack(operand(0, b, p))
        ar_scr[b, p, c2:] = stack(operand(1, b, p))
        bk_scr[b, p, :c2] = stack(operand(2, b, p))
        bk_scr[b, p, c2:] = stack(operand(3, b, p))
        tl_scr[b, p, :c2] = stack(operand(4, b, p))
        tl_scr[b, p, c2:] = stack(operand(5, b, p))
        v_scr[b, p] = stack(operand(6, b, p))

    for b, p in problems:
        ar = ar_scr[b, p]
        bk = bk_scr[b, p]
        if fused:
            gram = _bdot_nt(ar, bk)
            g_ab, g_ak = gram[:c2, :c2], gram[:c2, c2:]
            g_rb, g_rk = gram[c2:, :c2], gram[c2:, c2:]
        else:
            g_ab, g_ak = _bdot_nt(ar[:c2], bk[:c2]), _bdot_nt(ar[:c2], bk[c2:])
            g_rb, g_rk = _bdot_nt(ar[c2:], bk[:c2]), _bdot_nt(ar[c2:], bk[c2:])
        p_scr[b, p] = jnp.where(strict, g_ab, 0.0).astype(BF16)
        lak_scr[b, p] = jnp.where(strict, g_ak, 0.0).astype(BF16)
        mrb_scr[b, p] = jnp.where(incl, g_rb, 0.0).astype(BF16)
        mrk_scr[b, p] = jnp.where(incl, g_rk, 0.0).astype(BF16)
        ar_state = _bdot_nt(ar, s_scr[b, p])
        r32_scr[b, p] = ar_state[:c2]
        aro_scr[b, p] = ar_state[c2:]

    for b, p in problems:
        rhs = r32_scr[b, p] + jnp.dot(lak_scr[b, p], v_scr[b, p], preferred_element_type=F32)
        r32_scr[b, p] = rhs
        rb_scr[b, p] = rhs.astype(BF16)

    for m in range(n_factors):
        last = m == n_factors - 1
        for b, p in problems:
            pw = p_scr[b, p]
            rb = rb_scr[b, p]
            if last:
                delta = jnp.dot(pw, rb, preferred_element_type=F32)
            elif fused:
                both = jnp.dot(pw, jnp.concatenate([pw, rb], axis=1), preferred_element_type=F32)
                p_scr[b, p] = both[:, :c2].astype(BF16)
                delta = both[:, c2:]
            else:
                p_scr[b, p] = jnp.dot(pw, pw, preferred_element_type=F32).astype(BF16)
                delta = jnp.dot(pw, rb, preferred_element_type=F32)
            rhs = r32_scr[b, p] + delta
            r32_scr[b, p] = rhs
            rb_scr[b, p] = rhs.astype(BF16)

    for b, p in problems:
        sl = slice(p * PAIR_W, (p + 1) * PAIR_W)
        u_s = rb_scr[b, p]
        v_s = v_scr[b, p]
        tl = tl_scr[b, p]
        if fused:
            uv = jnp.concatenate([u_s, v_s], axis=0)
            mm = jnp.concatenate([mrb_scr[b, p], mrk_scr[b, p]], axis=1)
            o_st = aro_scr[b, p] + jnp.dot(mm, uv, preferred_element_type=F32)
            upd = _bdot_tn(uv, tl)
        else:
            o_st = (aro_scr[b, p] + jnp.dot(mrb_scr[b, p], u_s, preferred_element_type=F32)
                    + jnp.dot(mrk_scr[b, p], v_s, preferred_element_type=F32))
            upd = _bdot_tn(u_s, tl[:c2]) + _bdot_tn(v_s, tl[c2:])
        o_pair = o_st[:cc] + o_st[cc:]
        if by_rows:
            o_scr[p, _seq_rows(b, cc, nseq), :] = o_pair
        else:
            lo = b * RWKV_WIDTH + p * PAIR_W
            o_scr[:, lo:lo + PAIR_W] = o_pair
        s_scr[b, p] = s_scr[b, p] * gend_scr[b][:, sl] + upd

    ones = ones_ref[...]
    inv_n = 1.0 / RWKV_HEAD

    def finish(o, r, k, v, g):
        mean = _dot_exact_rhs(o, ones) * inv_n
        d = o - mean
        var = _dot_exact_rhs(d * d, ones) * inv_n
        on = d * lax.rsqrt(var + LNX_EPS) * lnw_ref[...] + lnb_ref[...]
        bonus = _dot_exact_rhs(r * k * rk_ref[...], ones) * v
        return (on + bonus) * g

    if by_rows:
        o = jnp.concatenate([o_scr[j] for j in range(HEAD_PAIRS)], axis=1)
        o_ref[...] = finish(o, r_ref[...], k, v, g_ref[...])
    else:
        for b in range(nseq):
            sb = slice(b * RWKV_WIDTH, (b + 1) * RWKV_WIDTH)
            o_ref[:, sb] = finish(o_scr[:, sb], r_ref[:, sb], k[:, sb], v[:, sb], g_ref[:, sb])

    @pl.when(c == pl.num_programs(1) - 1)
    def _():
        s_out_ref[...] = s_scr[...]


def _rwkv_rec(seqs, s0_bd, lw, layer, steps, batch, chunk, by_rows):
    if by_rows:
        nseq = batch
        seq_spec = pl.BlockSpec((chunk * batch, RWKV_WIDTH), lambda j, c: (c, 0))
        out_shape = jax.ShapeDtypeStruct((steps * batch, RWKV_WIDTH), F32)
    else:
        nseq = SHORT_GROUP
        seq_spec = pl.BlockSpec((chunk, nseq * RWKV_WIDTH), lambda j, c: (c, j))
        out_shape = jax.ShapeDtypeStruct((steps, batch * RWKV_WIDTH), F32)
    c2 = 2 * chunk
    st_spec = pl.BlockSpec((nseq, HEAD_PAIRS, PAIR_W, PAIR_W), lambda j, c: (j, 0, 0, 0))
    consts = [lw["rwkv_lnx_w"][layer], lw["rwkv_lnx_b"][layer], lw["rwkv_r_k"][layer], lw["head_ones"]]
    per = (nseq, HEAD_PAIRS)
    return pl.pallas_call(
        functools.partial(_rwkv_rec_kernel, chunk=chunk, nseq=nseq, by_rows=by_rows),
        grid=(batch // nseq, steps // chunk),
        in_specs=[seq_spec] * 7 + [st_spec] + [_const_spec(a.shape) for a in consts],
        out_specs=[seq_spec, st_spec],
        out_shape=[out_shape, jax.ShapeDtypeStruct(s0_bd.shape, F32)],
        scratch_shapes=[
            pltpu.VMEM(per + (PAIR_W, PAIR_W), F32),
            pltpu.VMEM(per + (2 * c2, PAIR_W), BF16),
            pltpu.VMEM(per + (2 * c2, PAIR_W), BF16),
            pltpu.VMEM(per + (c2, PAIR_W), BF16),
            pltpu.VMEM(per + (2 * c2, PAIR_W), BF16),
            pltpu.VMEM(per + (c2, c2), BF16),
            pltpu.VMEM(per + (c2, c2), BF16),
            pltpu.VMEM(per + (c2, c2), BF16),
            pltpu.VMEM(per + (c2, c2), BF16),
            pltpu.VMEM(per + (c2, PAIR_W), BF16),
            pltpu.VMEM(per + (c2, PAIR_W), F32),
            pltpu.VMEM(per + (c2, PAIR_W), F32),
            pltpu.VMEM((HEAD_PAIRS, chunk * nseq, PAIR_W) if by_rows
                       else (chunk, nseq * RWKV_WIDTH), F32),
            pltpu.VMEM((nseq, 1, RWKV_WIDTH), F32),
            pltpu.VMEM((7, HEAD_PAIRS, chunk * nseq, PAIR_W) if by_rows else (1, 1, 8, PAIR_W), F32),
        ],
        compiler_params=_cparams("parallel", "arbitrary"),
        name="rwkv_rec",
    )(*seqs, s0_bd, *consts)


def _pair_blockdiag(state):
    b = state.shape[0]
    s = state.reshape(b, HEAD_PAIRS, 2, RWKV_HEAD, RWKV_HEAD)
    eye2 = jnp.eye(2, dtype=state.dtype)
    bd = s[:, :, :, :, None, :] * eye2[None, None, :, None, :, None]
    return bd.reshape(b, HEAD_PAIRS, PAIR_W, PAIR_W)


def _pair_unblock(bd):
    b = bd.shape[0]
    s = bd.reshape(b, HEAD_PAIRS, 2, RWKV_HEAD, 2, RWKV_HEAD)
    out = jnp.stack([s[:, :, 0, :, 0, :], s[:, :, 1, :, 1, :]], axis=2)
    return out.reshape(b, RWKV_HEADS, RWKV_HEAD, RWKV_HEAD)


def _s5_kernel(u0_ref, u1_ref, h0r_ref, h0i_ref, are_ref, aim_ref, ldt_ref, bre_ref, bim_ref,
               cre_ref, cim_ref, d_ref, wglu_ref, bglu_ref,
               o_ref, hr_out, hi_out, hr_c, hi_c, hre, him, *, batch):
    @pl.when(pl.program_id(0) == 0)
    def _():
        hr_c[...] = h0r_ref[...]
        hi_c[...] = h0i_ref[...]

    a_re = are_ref[...]
    a_im = aim_ref[...]
    dt = jnp.exp(ldt_ref[...])
    mag = jnp.exp(a_re * dt)
    ab_re = mag * jnp.cos(a_im * dt)
    ab_im = mag * jnp.sin(a_im * dt)
    den = a_re * a_re + a_im * a_im
    nr = ab_re - 1.0
    cf_re = (nr * a_re + ab_im * a_im) / den
    cf_im = (ab_im * a_re - nr * a_im) / den

    us = (u0_ref[...], u1_ref[...])
    tm = us[0].shape[0]
    for hf in range(2):
        ls = slice(hf * S5_HALF_L, (hf + 1) * S5_HALF_L)
        ub = us[hf].astype(BF16)
        pr = jnp.dot(ub, bre_ref[hf], preferred_element_type=F32)
        pi = jnp.dot(ub, bim_ref[hf], preferred_element_type=F32)
        hre[:, ls] = cf_re[:, ls] * pr - cf_im[:, ls] * pi
        him[:, ls] = cf_re[:, ls] * pi + cf_im[:, ls] * pr

    n_steps = tm // batch
    if n_steps <= 8:
        hr = hr_c[...]
        hi = hi_c[...]
        for s in range(n_steps):
            rows = slice(s * batch, (s + 1) * batch)
            nhr = ab_re * hr - ab_im * hi + hre[rows, :]
            nhi = ab_re * hi + ab_im * hr + him[rows, :]
            hre[rows, :] = nhr
            him[rows, :] = nhi
            hr, hi = nhr, nhi
        hr_c[...] = hr
        hi_c[...] = hi
    else:
        lane_w = 512
        for lc in range(S5_LANES // lane_w):
            ls = slice(lc * lane_w, (lc + 1) * lane_w)
            abr = jnp.broadcast_to(ab_re[:, ls], (batch, lane_w))
            abi = jnp.broadcast_to(ab_im[:, ls], (batch, lane_w))

            def body(s, carry, ls=ls, abr=abr, abi=abi):
                hr, hi = carry
                rows = pl.ds(pl.multiple_of(s * batch, batch), batch)
                nhr = abr * hr - abi * hi + hre[rows, ls]
                nhi = abr * hi + abi * hr + him[rows, ls]
                hre[rows, ls] = nhr
                him[rows, ls] = nhi
                return nhr, nhi

            hr, hi = lax.fori_loop(0, n_steps, body, (hr_c[:, ls], hi_c[:, ls]), unroll=8)
            hr_c[:, ls] = hr
            hi_c[:, ls] = hi

    hr_out[...] = hr_c[...]
    hi_out[...] = hi_c[...]

    for hf in range(2):
        ls = slice(hf * S5_HALF_L, (hf + 1) * S5_HALF_L)
        cs = slice(hf * S5_HALF_W, (hf + 1) * S5_HALF_W)
        y = (jnp.dot(hre[:, ls].astype(BF16), cre_ref[hf], preferred_element_type=F32)
             - jnp.dot(him[:, ls].astype(BF16), cim_ref[hf], preferred_element_type=F32)
             + d_ref[:, cs] * us[hf])
        o_ref[:, cs] = jax.nn.gelu(y, approximate=True)
    y = o_ref[...]
    o_ref[...] = y * jax.nn.sigmoid(_bdot(y, wglu_ref[...]) + bglu_ref[...])


def _s5(proj, h0r, h0i, lw, layer, batch):
    rows = proj.shape[0]
    tm = min(ROW_TILE, rows)
    u_blk = RWKV_COLS // S5_HALF_W
    consts = [h0r, h0i, lw["s5_a_re"][layer], lw["s5_a_im"][layer], lw["s5_log_dt"][layer],
              lw["s5_b_re"][layer], lw["s5_b_im"][layer], lw["s5_c_re"][layer], lw["s5_c_im"][layer],
              lw["s5_d"][layer], lw["s5_w_glu"][layer], lw["s5_b_glu"][layer]]
    st_shape = jax.ShapeDtypeStruct((batch, S5_LANES), F32)
    st_spec = pl.BlockSpec((batch, S5_LANES), lambda i: (0, 0))
    return pl.pallas_call(
        functools.partial(_s5_kernel, batch=batch),
        grid=(rows // tm,),
        in_specs=[pl.BlockSpec((tm, S5_HALF_W), lambda i: (i, u_blk)),
                  pl.BlockSpec((tm, S5_HALF_W), lambda i: (i, u_blk + 1))]
                 + [_const_spec(a.shape) for a in consts],
        out_specs=[pl.BlockSpec((tm, S5_WIDTH), lambda i: (i, 0)), st_spec, st_spec],
        out_shape=[jax.ShapeDtypeStruct((rows, S5_WIDTH), F32), st_shape, st_shape],
        scratch_shapes=[pltpu.VMEM((batch, S5_LANES), F32), pltpu.VMEM((batch, S5_LANES), F32),
                        pltpu.VMEM((tm, S5_LANES), F32), pltpu.VMEM((tm, S5_LANES), F32)],
        compiler_params=_cparams("arbitrary"),
        name="s5",
    )(proj, proj, *consts)


def _attend(q, k, v):
    s = _bdot_nt(q, k) * (X_HEAD_DIM ** -0.5)
    e = jnp.exp(s - jnp.max(s, axis=-1, keepdims=True))
    return _bdot(e / jnp.sum(e, axis=-1, keepdims=True), v)


def _attn_kernel(q_ref, k_ref, v_ref, o_ref):
    for h in range(X_HEADS):
        sl = slice(h * X_HEAD_DIM, (h + 1) * X_HEAD_DIM)
        o_ref[:, sl] = _attend(q_ref[:, sl], k_ref[:, sl], v_ref[:, sl])


def _attn(q, mem_k, mem_v, steps, batch):
    q_spec = pl.BlockSpec((steps, D_MODEL), lambda b: (0, b))
    m_spec = pl.BlockSpec((None, N_MEM, D_MODEL), lambda b: (b, 0, 0))
    return pl.pallas_call(
        _attn_kernel,
        grid=(batch,),
        in_specs=[q_spec, m_spec, m_spec],
        out_specs=q_spec,
        out_shape=jax.ShapeDtypeStruct((steps, batch * D_MODEL), F32),
        compiler_params=_cparams("parallel"),
        name="mem_attn",
    )(q, mem_k, mem_v)


def _cross_long_kernel(x_ref, gain_ref, wq_ref, k_ref, v_ref, wo_ref, o_ref, q_scr, att_scr, *, batch):
    x = x_ref[...]
    steps = x.shape[0] // batch
    _stage(q_scr, jnp.dot(_rms(x, gain_ref[...]).astype(BF16), wq_ref[...],
                          preferred_element_type=F32))
    tiles_per_head = X_HEAD_DIM // LANES
    n_tiles = D_MODEL // LANES
    for b in range(batch):
        rows = _seq_rows(b, steps, batch)
        for h in range(X_HEADS):
            sl = slice(h * X_HEAD_DIM, (h + 1) * X_HEAD_DIM)
            tiles = range(h * tiles_per_head, (h + 1) * tiles_per_head)
            q = jnp.concatenate([q_scr[j, rows, :] for j in tiles], axis=1)
            o = _attend(q, k_ref[b, :, sl], v_ref[b, :, sl])
            for i, j in enumerate(tiles):
                att_scr[j, rows, :] = o[:, i * LANES:(i + 1) * LANES]
    att = jnp.concatenate([att_scr[j] for j in range(n_tiles)], axis=1)
    o_ref[...] = x + jnp.dot(att.astype(BF16), wo_ref[...], preferred_element_type=F32)


def _cross_long(x, mem_k, mem_v, lw, layer, batch):
    rows = x.shape[0]
    tm = min(ATTN_ROW_TILE, rows)
    consts = [lw["norm_cross"][layer], lw["w_cq"][layer], mem_k, mem_v, lw["w_co"][layer]]
    return pl.pallas_call(
        functools.partial(_cross_long_kernel, batch=batch),
        grid=(rows // tm,),
        in_specs=[pl.BlockSpec((tm, D_MODEL), lambda i: (i, 0))] + [_const_spec(a.shape) for a in consts],
        out_specs=pl.BlockSpec((tm, D_MODEL), lambda i: (i, 0)),
        out_shape=jax.ShapeDtypeStruct((rows, D_MODEL), F32),
        scratch_shapes=[pltpu.VMEM((D_MODEL // LANES, tm, LANES), F32),
                        pltpu.VMEM((D_MODEL // LANES, tm, LANES), F32)],
        compiler_params=_cparams("parallel"),
        name="cross_attn",
    )(x, *consts)


def _ffn_kernel(x_ref, buf0_ref, gain_ref, wg_ref, wu_ref, cw_ref, cb_ref, wd_ref,
                o_ref, buf_out, carry, *, batch):
    @pl.when(pl.program_id(0) == 0)
    def _():
        carry[...] = buf0_ref[...]

    x = x_ref[...]
    tm = x.shape[0]
    h = _rms(x, gain_ref[...]).astype(BF16)
    gt = jnp.dot(h, wg_ref[...], preferred_element_type=F32)
    up = jnp.dot(h, wu_ref[...], preferred_element_type=F32)
    padded = jnp.concatenate([carry[...], gt], axis=0)
    conv = cb_ref[...]
    for i in range(CONV_W):
        conv = conv + cw_ref[i:i + 1, :] * padded[i * batch:i * batch + tm]
    new_carry = padded[tm:]
    carry[...] = new_carry
    buf_out[...] = new_carry
    act = jax.nn.silu(conv) * up
    o_ref[...] = x + jnp.dot(act.astype(BF16), wd_ref[...], preferred_element_type=F32)


def _ffn(x, buf0, lw, layer, batch):
    rows = x.shape[0]
    tm = min(FFN_ROW_TILE, rows)
    tm = max(tm, (CONV_W - 1) * batch)
    consts = [buf0, lw["norm_ffn"][layer], lw["w_gate"][layer], lw["w_up"][layer],
              lw["ffn_conv_w"][layer], lw["ffn_conv_b"][layer], lw["w_down"][layer]]
    nbuf = (CONV_W - 1) * batch
    return pl.pallas_call(
        functools.partial(_ffn_kernel, batch=batch),
        grid=(rows // tm,),
        in_specs=[pl.BlockSpec((tm, D_MODEL), lambda i: (i, 0))] + [_const_spec(a.shape) for a in consts],
        out_specs=[pl.BlockSpec((tm, D_MODEL), lambda i: (i, 0)),
                   pl.BlockSpec((nbuf, D_FF), lambda i: (0, 0))],
        out_shape=[jax.ShapeDtypeStruct((rows, D_MODEL), F32), jax.ShapeDtypeStruct((nbuf, D_FF), F32)],
        scratch_shapes=[pltpu.VMEM((nbuf, D_FF), F32)],
        compiler_params=_cparams("arbitrary"),
        name="conv_ffn",
    )(x, *consts)


def _s5_in_blockdiag(b):
    l = b.shape[0]
    b = b.reshape(l, 2, S5_GROUPS // 2, S5_STATE, S5_GROUP)
    eye = jnp.eye(S5_GROUPS // 2, dtype=b.dtype)
    m = jnp.einsum('lfgph,gk->lfghkp', b, eye)
    return m.reshape(l, 2, S5_HALF_W, S5_HALF_L).astype(BF16)


def _s5_out_blockdiag(c):
    l = c.shape[0]
    c = c.reshape(l, 2, S5_GROUPS // 2, S5_GROUP, S5_STATE)
    eye = jnp.eye(S5_GROUPS // 2, dtype=c.dtype)
    m = jnp.einsum('lfgnp,gk->lfgpkn', c, eye)
    return m.reshape(l, 2, S5_HALF_L, S5_HALF_W).astype(BF16)


def _prep_weights(p):
    l = DEPTH
    row = lambda a: a.reshape(a.shape[0], 1, -1)
    lw = {}
    for name in ("norm_mix", "shift_mu", "rwkv_w0", "rwkv_a0", "rwkv_v0", "rwkv_k_k", "rwkv_k_a",
                 "rwkv_r_k", "rwkv_lnx_w", "rwkv_lnx_b", "s5_d", "s5_b_glu", "norm_cross", "norm_ffn",
                 "ffn_conv_b", "s5_a_re", "s5_a_im"):
        lw[name] = row(p[name])
    lw["s5_log_dt"] = row(jnp.repeat(p["s5_log_dt"], S5_STATE, axis=-1))
    lw["norm_final"] = p["norm_final"].reshape(1, -1)
    lw["ffn_conv_w"] = p["ffn_conv_w"]
    for name in ("w_in", "rwkv_w_g2", "s5_w_glu", "w_cq", "w_ck", "w_cv", "w_co", "w_gate", "w_up",
                 "w_down"):
        lw[name] = p[name].astype(BF16)
    w_out = p["w_out"].astype(BF16)
    lw["w_out_rw"] = w_out[:, :RWKV_WIDTH]
    lw["w_out_s5"] = w_out[:, RWKV_WIDTH:]
    z64 = jnp.zeros((l, LORA_PAD - 64, RWKV_WIDTH), F32)
    lw["rwkv_w_w2p"] = jnp.concatenate([p["rwkv_w_w2"], z64], axis=1).astype(BF16)
    lw["rwkv_w_a2p"] = jnp.concatenate([z64, p["rwkv_w_a2"]], axis=1).astype(BF16)
    v_lora = p["rwkv_w_v1"].shape[-1]
    lw["rwkv_w_v1p"] = jnp.pad(p["rwkv_w_v1"], ((0, 0), (0, 0), (0, LORA_PAD - v_lora))).astype(BF16)
    lw["rwkv_w_v2p"] = jnp.pad(p["rwkv_w_v2"], ((0, 0), (0, LORA_PAD - v_lora), (0, 0))).astype(BF16)
    head = jnp.arange(RWKV_WIDTH) // RWKV_HEAD
    lw["head_ones"] = (head[:, None] == head[None, :]).astype(BF16)
    lw["s5_b_re"] = _s5_in_blockdiag(p["s5_b_re"])
    lw["s5_b_im"] = _s5_in_blockdiag(p["s5_b_im"])
    lw["s5_c_re"] = _s5_out_blockdiag(p["s5_c_re"])
    lw["s5_c_im"] = _s5_out_blockdiag(p["s5_c_im"])
    return lw


def _run_trunk(x, mem_k, mem_v, st_rwkv, st_shift, st_re, st_im, st_conv, lw, steps, batch):
    rows = steps * batch
    long_seq = steps >= LONG_SEQ
    steps_pad = steps if long_seq else -(-steps // SHORT_CHUNK) * SHORT_CHUNK
    if long_seq:
        mem_k = mem_k.astype(BF16)
        mem_v = mem_v.astype(BF16)
    v_first = None
    new_rw, new_shift, new_re, new_im, new_conv = [], [], [], [], []
    for l in range(DEPTH):
        proj = _rowmm([x], [lw["w_in"][l]], gain=lw["norm_mix"][l], name="in_proj")
        *seqs, sh = _rwkv_prep(proj, st_shift[l], v_first, lw, l, batch)
        if l == 0:
            v_first = seqs[3]
        s0_bd = _pair_blockdiag(st_rwkv[l])
        if long_seq:
            o_rw, s_rw = _rwkv_rec(seqs, s0_bd, lw, l, steps, batch, LONG_CHUNK, True)
        else:
            seqs = [jnp.pad(s.reshape(steps, batch * RWKV_WIDTH), ((0, steps_pad - steps), (0, 0)))
                    for s in seqs]
            o_rw, s_rw = _rwkv_rec(seqs, s0_bd, lw, l, steps_pad, batch, SHORT_CHUNK, False)
            o_rw = o_rw[:steps].reshape(rows, RWKV_WIDTH)
        o_s5, hr, hi = _s5(proj, st_re[l], st_im[l], lw, l, batch)
        x = _rowmm([o_rw, o_s5], [lw["w_out_rw"][l], lw["w_out_s5"][l]], resid=x, name="mix_out")
        if long_seq:
            x = _cross_long(x, mem_k[l], mem_v[l], lw, l, batch)
        else:
            q = _rowmm([x], [lw["w_cq"][l]], gain=lw["norm_cross"][l], name="cross_q")
            att = _attn(q.reshape(steps, batch * D_MODEL), mem_k[l], mem_v[l], steps, batch)
            x = _rowmm([att.reshape(rows, D_MODEL)], [lw["w_co"][l]], resid=x, name="cross_o")
        x, cb = _ffn(x, st_conv[l], lw, l, batch)
        new_rw.append(_pair_unblock(s_rw))
        new_shift.append(sh)
        new_re.append(hr)
        new_im.append(hi)
        new_conv.append(cb)
    return (x, jnp.stack(new_rw), jnp.stack(new_shift), jnp.stack(new_re), jnp.stack(new_im),
            jnp.stack(new_conv))


def _group(x, mem_k, mem_v, st_rwkv, st_shift, st_re, st_im, st_conv, lw):
    b, t, _ = x.shape
    long_seq = t >= LONG_SEQ
    if long_seq:
        xt = _to_time_major(x)
    else:
        xt = jnp.swapaxes(x, 0, 1).reshape(t * b, D_MODEL)
    conv_t = jnp.swapaxes(st_conv, 1, 2).reshape(DEPTH, (CONV_W - 1) * b, D_FF)
    y, rw, sh, re, im, cv = _run_trunk(
        xt, mem_k.reshape(DEPTH, b, N_MEM, D_MODEL), mem_v.reshape(DEPTH, b, N_MEM, D_MODEL),
        st_rwkv, st_shift, st_re.reshape(DEPTH, b, S5_LANES), st_im.reshape(DEPTH, b, S5_LANES),
        conv_t, lw, t, b)
    if long_seq:
        y = _final_norm_long(y, lw["norm_final"], b)
    else:
        y = jnp.swapaxes(_rownorm(y, lw["norm_final"]).reshape(t, b, D_MODEL), 0, 1)
    cv = jnp.swapaxes(cv.reshape(DEPTH, CONV_W - 1, b, D_FF), 1, 2)
    return (y, rw, sh, re.reshape(DEPTH, b, S5_GROUPS, S5_STATE),
            im.reshape(DEPTH, b, S5_GROUPS, S5_STATE), cv)


def kernel(x_prompt, x_sample, mem_prompt, state_rwkv, state_shift, state_s5_re, state_s5_im, state_ffn_conv, cache_mem_k, cache_mem_v, norm_mix, w_in, shift_mu, rwkv_w0, rwkv_w_w2, rwkv_a0, rwkv_w_a2, rwkv_v0, rwkv_w_v1, rwkv_w_v2, rwkv_w_g2, rwkv_k_k, rwkv_k_a, rwkv_r_k, rwkv_lnx_w, rwkv_lnx_b, s5_a_re, s5_a_im, s5_log_dt, s5_b_re, s5_b_im, s5_c_re, s5_c_im, s5_d, s5_w_glu, s5_b_glu, w_out, norm_cross, w_cq, w_ck, w_cv, w_co, norm_ffn, w_gate, w_up, ffn_conv_w, ffn_conv_b, w_down, norm_final):
    lw = _prep_weights(dict(
        norm_mix=norm_mix, w_in=w_in, shift_mu=shift_mu, rwkv_w0=rwkv_w0, rwkv_w_w2=rwkv_w_w2,
        rwkv_a0=rwkv_a0, rwkv_w_a2=rwkv_w_a2, rwkv_v0=rwkv_v0, rwkv_w_v1=rwkv_w_v1,
        rwkv_w_v2=rwkv_w_v2, rwkv_w_g2=rwkv_w_g2, rwkv_k_k=rwkv_k_k, rwkv_k_a=rwkv_k_a,
        rwkv_r_k=rwkv_r_k, rwkv_lnx_w=rwkv_lnx_w, rwkv_lnx_b=rwkv_lnx_b, s5_a_re=s5_a_re,
        s5_a_im=s5_a_im, s5_log_dt=s5_log_dt, s5_b_re=s5_b_re, s5_b_im=s5_b_im, s5_c_re=s5_c_re,
        s5_c_im=s5_c_im, s5_d=s5_d, s5_w_glu=s5_w_glu, s5_b_glu=s5_b_glu, w_out=w_out,
        norm_cross=norm_cross, w_cq=w_cq, w_ck=w_ck, w_cv=w_cv, w_co=w_co, norm_ffn=norm_ffn,
        w_gate=w_gate, w_up=w_up, ffn_conv_w=ffn_conv_w, ffn_conv_b=ffn_conv_b, w_down=w_down,
        norm_final=norm_final))
    bp = x_prompt.shape[0]
    mem_rows = mem_prompt.reshape(bp * N_MEM, D_MODEL)
    p_mem_k = jnp.stack([_rowmm([mem_rows], [lw["w_ck"][l]], name="mem_k") for l in range(DEPTH)])
    p_mem_v = jnp.stack([_rowmm([mem_rows], [lw["w_cv"][l]], name="mem_v") for l in range(DEPTH)])
    p_mem_k = p_mem_k.reshape(DEPTH, bp, N_MEM, X_HEADS, X_HEAD_DIM)
    p_mem_v = p_mem_v.reshape(DEPTH, bp, N_MEM, X_HEADS, X_HEAD_DIM)
    z_rw = jnp.zeros((DEPTH, bp, RWKV_HEADS, RWKV_HEAD, RWKV_HEAD), F32)
    z_shift = jnp.zeros((DEPTH, bp, RWKV_COLS), F32)
    z_s5 = jnp.zeros((DEPTH, bp, S5_GROUPS, S5_STATE), F32)
    z_conv = jnp.zeros((DEPTH, bp, CONV_W - 1, D_FF), F32)
    y_prompt, p_rwkv, p_shift, p_re, p_im, p_conv = _group(
        x_prompt, p_mem_k, p_mem_v, z_rw, z_shift, z_s5, z_s5, z_conv, lw)
    y_sample, s_rwkv, s_shift, s_re, s_im, s_conv = _group(
        x_sample, cache_mem_k, cache_mem_v, state_rwkv, state_shift, state_s5_re, state_s5_im,
        state_ffn_conv, lw)
    return (y_prompt, y_sample, p_rwkv, p_shift, p_re, p_im, p_conv, p_mem_k, p_mem_v,
            s_rwkv, s_shift, s_re, s_im, s_conv)
```

```python
import functools
import math

import jax
import jax.numpy as jnp
from jax import lax
from jax.experimental import pallas as pl
from jax.experimental.pallas import tpu as pltpu

F32 = jnp.float32
BF16 = jnp.bfloat16

D_MODEL = 1024
DEPTH = 4
RWKV_WIDTH = 512
RWKV_HEAD = 64
RWKV_HEADS = 8
HEAD_PAIRS = RWKV_HEADS // 2
LANES = 128
PAIR_W = 2 * RWKV_HEAD
LORA_PAD = 128
RWKV_COLS = 3 * RWKV_WIDTH + 64 + 64 + 128
S5_WIDTH = 512
S5_GROUP = 16
S5_GROUPS = 32
S5_STATE = 64
S5_LANES = S5_GROUPS * S5_STATE
S5_HALF_W = S5_WIDTH // 2
S5_HALF_L = S5_LANES // 2
IN_COLS = RWKV_COLS + S5_WIDTH
N_MEM = 256
X_HEADS = 4
X_HEAD_DIM = 256
D_FF = 2816
CONV_W = 3
RMS_EPS = 1e-6
LNX_EPS = 64e-5

ROW_TILE = 512
FFN_ROW_TILE = 256
ATTN_ROW_TILE = 1024
LONG_SEQ = 64
LONG_CHUNK = 64
SHORT_CHUNK = 8
SHORT_GROUP = 8
VMEM_LIMIT = 56 * 1024 * 1024


def _cparams(*sem):
    return pltpu.CompilerParams(dimension_semantics=sem, vmem_limit_bytes=VMEM_LIMIT)


def _const_spec(shape):
    nd = len(shape)
    return pl.BlockSpec(shape, lambda *_: (0,) * nd, pipeline_mode=pl.Buffered(1))


def _bdot(a, b):
    return jnp.dot(a.astype(BF16), b.astype(BF16), preferred_element_type=F32)


def _bdot_nt(a, b):
    return lax.dot_general(a.astype(BF16), b.astype(BF16), (((1,), (1,)), ((), ())),
                           preferred_element_type=F32)


def _bdot_tn(a, b):
    return lax.dot_general(a.astype(BF16), b.astype(BF16), (((0,), (0,)), ((), ())),
                           preferred_element_type=F32)


def _split(x, parts):
    out = []
    for _ in range(parts - 1):
        hi = x.astype(BF16)
        out.append(hi)
        x = x - hi.astype(F32)
    out.append(x.astype(BF16))
    return out


def _dot_exact_lhs(m_bf16, x):
    return sum(jnp.dot(m_bf16, part, preferred_element_type=F32) for part in _split(x, 3))


def _head_sums(x, pair_ones):
    outs = []
    for sl in _lane_tiles(x.shape[1]):
        outs.append(sum(jnp.dot(part, pair_ones, preferred_element_type=F32)
                        for part in _split(x[:, sl], 2)))
    return jnp.concatenate(outs, axis=1)


def _rms(x, gain):
    return x * lax.rsqrt(jnp.mean(x * x, axis=-1, keepdims=True) + RMS_EPS) * gain


def _seq_rows(seq, steps, batch):
    return pl.ds(seq, steps, stride=batch)


def _lane_tiles(width):
    return [slice(j * LANES, (j + 1) * LANES) for j in range(width // LANES)]


def _stage(scr, x):
    for j, sl in enumerate(_lane_tiles(x.shape[1])):
        scr[j] = x[:, sl]


def _to_time_major_kernel(x_ref, o_ref, scr, *, batch, steps):
    tiles = _lane_tiles(x_ref.shape[2])
    for b in range(batch):
        for j, sl in enumerate(tiles):
            scr[j, _seq_rows(b, steps, batch), :] = x_ref[b, :, sl]
    for j, sl in enumerate(tiles):
        o_ref[:, sl] = scr[j]


def _to_time_major(x):
    b, t, d = x.shape
    tt = ROW_TILE // b
    return pl.pallas_call(
        functools.partial(_to_time_major_kernel, batch=b, steps=tt),
        grid=(t // tt,),
        in_specs=[pl.BlockSpec((b, tt, d), lambda i: (0, i, 0))],
        out_specs=pl.BlockSpec((tt * b, d), lambda i: (i, 0)),
        out_shape=jax.ShapeDtypeStruct((t * b, d), F32),
        scratch_shapes=[pltpu.VMEM((d // LANES, tt * b, LANES), F32)],
        compiler_params=_cparams("parallel"),
        name="to_time_major",
    )(x)


def _final_norm_long_kernel(x_ref, g_ref, o_ref, scr, *, batch, steps):
    _stage(scr, _rms(x_ref[...], g_ref[...]))
    for b in range(batch):
        for j, sl in enumerate(_lane_tiles(x_ref.shape[1])):
            o_ref[b, :, sl] = scr[j, _seq_rows(b, steps, batch), :]


def _final_norm_long(x, gain, batch):
    rows, d = x.shape
    t = rows // batch
    tt = ROW_TILE // batch
    return pl.pallas_call(
        functools.partial(_final_norm_long_kernel, batch=batch, steps=tt),
        grid=(t // tt,),
        in_specs=[pl.BlockSpec((tt * batch, d), lambda i: (i, 0)), _const_spec(gain.shape)],
        out_specs=pl.BlockSpec((batch, tt, d), lambda i: (0, i, 0)),
        out_shape=jax.ShapeDtypeStruct((batch, t, d), F32),
        scratch_shapes=[pltpu.VMEM((d // LANES, tt * batch, LANES), F32)],
        compiler_params=_cparams("parallel"),
        name="final_norm",
    )(x, gain)


def _rowmm_kernel(*refs, n_x, has_gain, has_resid):
    xs = refs[:n_x]
    ws = refs[n_x:2 * n_x]
    pos = 2 * n_x
    gain = refs[pos] if has_gain else None
    pos += int(has_gain)
    resid = refs[pos] if has_resid else None
    pos += int(has_resid)
    o_ref = refs[pos]
    acc = None
    for x_ref, w_ref in zip(xs, ws):
        x = x_ref[...]
        if has_gain:
            x = _rms(x, gain[...])
        d = jnp.dot(x.astype(BF16), w_ref[...], preferred_element_type=F32)
        acc = d if acc is None else acc + d
    if has_resid:
        acc = acc + resid[...]
    o_ref[...] = acc


def _rowmm(xs, ws, gain=None, resid=None, name="rowmm"):
    rows = xs[0].shape[0]
    n_out = ws[0].shape[1]
    tm = min(ROW_TILE, rows)
    in_specs = [pl.BlockSpec((tm, x.shape[1]), lambda i: (i, 0)) for x in xs]
    in_specs += [_const_spec(w.shape) for w in ws]
    args = list(xs) + list(ws)
    if gain is not None:
        in_specs.append(_const_spec(gain.shape))
        args.append(gain)
    if resid is not None:
        in_specs.append(pl.BlockSpec((tm, n_out), lambda i: (i, 0)))
        args.append(resid)
    return pl.pallas_call(
        functools.partial(_rowmm_kernel, n_x=len(xs), has_gain=gain is not None,
                          has_resid=resid is not None),
        grid=(rows // tm,),
        in_specs=in_specs,
        out_specs=pl.BlockSpec((tm, n_out), lambda i: (i, 0)),
        out_shape=jax.ShapeDtypeStruct((rows, n_out), F32),
        compiler_params=_cparams("parallel"),
        name=name,
    )(*args)


def _rownorm_kernel(x_ref, g_ref, o_ref):
    o_ref[...] = _rms(x_ref[...], g_ref[...])


def _rownorm(x, gain):
    rows, d = x.shape
    tm = min(ROW_TILE, rows)
    return pl.pallas_call(
        _rownorm_kernel,
        grid=(rows // tm,),
        in_specs=[pl.BlockSpec((tm, d), lambda i: (i, 0)), _const_spec(gain.shape)],
        out_specs=pl.BlockSpec((tm, d), lambda i: (i, 0)),
        out_shape=jax.ShapeDtypeStruct((rows, d), F32),
        compiler_params=_cparams("parallel"),
        name="final_norm",
    )(x, gain)


def _softplus(z):
    return jnp.maximum(z, 0.0) + jnp.log1p(jnp.exp(-jnp.abs(z)))


def _prep_consts(lw, layer, has_vfirst):
    names = ["shift_mu", "rwkv_w0", "rwkv_w_w2p", "rwkv_a0", "rwkv_w_a2p", "rwkv_w_g2", "rwkv_k_k",
             "rwkv_k_a"]
    consts = [lw[n][layer] for n in names] + [lw["pair_ones"]]
    if has_vfirst:
        consts += [lw[n][layer - 1] for n in ("rwkv_v0", "rwkv_w_v1p", "rwkv_w_v2p")]
    return consts


def _token_shift(p, carry, batch):
    tm = p.shape[0]
    prev = jnp.concatenate([carry, p[:tm - batch]], axis=0) if tm > batch else carry
    return prev, p[tm - batch:]


def _prep_math(p, p_prev, v_first, consts):
    mu, w0, ww2, a0, wa2, wg2, k_k, k_a, pair_ones = [c[...] for c in consts[:9]]
    q = p + (p_prev - p) * mu
    rw = RWKV_WIDTH
    r = q[:, 0:rw]
    k = q[:, rw:2 * rw]
    v = q[:, 2 * rw:3 * rw]
    x_wa = q[:, 3 * rw:3 * rw + LORA_PAD]
    x_g = q[:, 3 * rw + LORA_PAD:3 * rw + 2 * LORA_PAD]
    w = -_softplus(-(w0 + _bdot(jnp.tanh(x_wa), ww2))) - 0.5
    log_decay = -jnp.exp(w)
    a = jax.nn.sigmoid(a0 + _bdot(x_wa, wa2))
    g = _bdot(jax.nn.sigmoid(x_g), wg2)
    if v_first is not None:
        v0, wv1, wv2 = [c[...] for c in consts[9:12]]
        mix = jax.nn.sigmoid(v0 + _bdot(_bdot(v, wv1), wv2))
        v = v + (v_first - v) * mix
    kk = k * k_k
    kk = kk * lax.rsqrt(jnp.maximum(_head_sums(kk * kk, pair_ones), 1e-24))
    k = k * (1.0 + (a - 1.0) * k_a)
    return r, log_decay, k, v, kk, a, g


def _rwkv_prep_kernel(*refs, batch, has_vfirst, n_consts):
    p_ref, shift0_ref = refs[:2]
    consts = refs[2:2 + n_consts]
    pos = 2 + n_consts
    vf_ref = refs[pos] if has_vfirst else None
    pos += int(has_vfirst)
    outs = refs[pos:pos + 7]
    shift_out, carry = refs[pos + 7:pos + 9]

    @pl.when(pl.program_id(0) == 0)
    def _():
        carry[...] = shift0_ref[...]

    p = p_ref[...]
    p_prev, new_carry = _token_shift(p, carry[...], batch)
    carry[...] = new_carry
    shift_out[...] = new_carry
    vals = _prep_math(p, p_prev, vf_ref[...] if has_vfirst else None, consts)
    for o_ref, val in zip(outs, vals):
        o_ref[...] = val


def _rwkv_prep(proj, shift0, v_first, lw, layer, batch):
    rows = proj.shape[0]
    tm = min(ROW_TILE, rows)
    row_spec = pl.BlockSpec((tm, RWKV_WIDTH), lambda i: (i, 0))
    has_vfirst = v_first is not None
    consts = _prep_consts(lw, layer, has_vfirst)
    args = [proj, shift0] + consts + ([v_first] if has_vfirst else [])
    in_specs = ([pl.BlockSpec((tm, RWKV_COLS), lambda i: (i, 0)), _const_spec(shift0.shape)]
                + [_const_spec(a.shape) for a in consts] + ([row_spec] if has_vfirst else []))
    out_row = jax.ShapeDtypeStruct((rows, RWKV_WIDTH), F32)
    return pl.pallas_call(
        functools.partial(_rwkv_prep_kernel, batch=batch, has_vfirst=has_vfirst, n_consts=len(consts)),
        grid=(rows // tm,),
        in_specs=in_specs,
        out_specs=[row_spec] * 7 + [pl.BlockSpec((batch, RWKV_COLS), lambda i: (0, 0))],
        out_shape=[out_row] * 7 + [jax.ShapeDtypeStruct((batch, RWKV_COLS), F32)],
        scratch_shapes=[pltpu.VMEM((batch, RWKV_COLS), F32)],
        compiler_params=_cparams("arbitrary"),
        name="rwkv_prep",
    )(*args)


def _rec_scratch(chunk, nseq, by_rows):
    c2 = 2 * chunk
    per = (nseq, HEAD_PAIRS)
    return [
        pltpu.VMEM(per + (PAIR_W, PAIR_W), F32),
        pltpu.VMEM(per + (2 * c2, PAIR_W), BF16),
        pltpu.VMEM(per + (2 * c2, PAIR_W), BF16),
        pltpu.VMEM(per + (c2, PAIR_W), BF16),
        pltpu.VMEM(per + (2 * c2, PAIR_W), BF16),
        pltpu.VMEM(per + (c2, c2), BF16),
        pltpu.VMEM(per + (c2, c2), BF16),
        pltpu.VMEM(per + (c2, c2), BF16),
        pltpu.VMEM(per + (c2, c2), BF16),
        pltpu.VMEM(per + (c2, PAIR_W), BF16),
        pltpu.VMEM(per + (c2, PAIR_W), F32),
        pltpu.VMEM(per + (c2, PAIR_W), F32),
        pltpu.VMEM((HEAD_PAIRS, chunk * nseq, PAIR_W) if by_rows else (chunk, nseq * RWKV_WIDTH), F32),
        pltpu.VMEM((nseq, 1, RWKV_WIDTH), F32),
        pltpu.VMEM((7, HEAD_PAIRS, chunk * nseq, PAIR_W) if by_rows else (1, 1, 8, PAIR_W), F32),
    ]


def _rec_phases(vals, s0_ref, post_consts, o_ref, s_out_ref, scratch, *, chunk, nseq, by_rows):
    (s_scr, ar_scr, bk_scr, v_scr, tl_scr, p_scr, lak_scr, mrb_scr, mrk_scr, rb_scr, r32_scr,
     aro_scr, o_scr, gend_scr, stage_scr) = scratch
    r, lw, k, v, kk, a, g = vals
    lnw_ref, lnb_ref, rk_ref, ones_ref = post_consts
    c = pl.program_id(1)
    cc = chunk
    c2 = 2 * cc
    fused = c2 % 128 == 0

    @pl.when(c == 0)
    def _():
        s_scr[...] = s0_ref[...]

    lane = lax.broadcasted_iota(jnp.int32, (1, PAIR_W), 1)
    first = lane < RWKV_HEAD
    row2 = lax.broadcasted_iota(jnp.int32, (c2, c2), 0)
    col2 = lax.broadcasted_iota(jnp.int32, (c2, c2), 1)
    same = (row2 >= cc) == (col2 >= cc)
    rr = jnp.where(row2 >= cc, row2 - cc, row2)
    cl = jnp.where(col2 >= cc, col2 - cc, col2)
    strict = same & (cl < rr)
    incl = same & (cl <= rr)
    n_factors = max(1, math.ceil(math.log2(cc)))
    problems = [(b, p) for b in range(nseq) for p in range(HEAD_PAIRS)]

    def stack(xs):
        return jnp.concatenate([jnp.where(first, xs, 0.0), jnp.where(first, 0.0, xs)],
                               axis=0).astype(BF16)

    if by_rows:
        cum = lw
        shift = nseq
        while shift < cc * nseq:
            cum = cum + jnp.concatenate([jnp.zeros((shift, RWKV_WIDTH), F32), cum[:-shift]], axis=0)
            shift *= 2
        cum_end = cum[(cc - 1) * nseq:]
        for b in range(nseq):
            gend_scr[b] = jnp.exp(cum_end[b:b + 1])
        cum_end = jnp.broadcast_to(cum_end[None], (cc, nseq, RWKV_WIDTH)).reshape(cc * nseq, RWKV_WIDTH)
    else:
        row = lax.broadcasted_iota(jnp.int32, (cc, cc), 0)
        col = lax.broadcasted_iota(jnp.int32, (cc, cc), 1)
        cum = _dot_exact_lhs((col <= row).astype(BF16), lw)
        cum_end = cum[cc - 1:cc, :]
        g_end = jnp.exp(cum_end)
        for b in range(nseq):
            gend_scr[b] = g_end[:, b * RWKV_WIDTH:(b + 1) * RWKV_WIDTH]
    g_inv = jnp.exp(-cum)
    g_tail = jnp.exp(cum_end - cum)
    kka = kk * a
    operands = [-kk * jnp.exp(cum - lw), r * jnp.exp(cum),
                kka * g_inv, k * g_inv,
                kka * g_tail, k * g_tail, v]
    if by_rows:
        for i, x in enumerate(operands):
            _stage(stage_scr.at[i], x)

    def operand(i, b, p):
        if by_rows:
            return stage_scr[i, p, _seq_rows(b, cc, nseq), :]
        lo = b * RWKV_WIDTH + p * PAIR_W
        return operands[i][:, lo:lo + PAIR_W]

    for b, p in problems:
        ar_scr[b, p, :c2] = stack(operand(0, b, p))
        ar_scr[b, p, c2:] = stack(operand(1, b, p))
        bk_scr[b, p, :c2] = stack(operand(2, b, p))
        bk_scr[b, p, c2:] = stack(operand(3, b, p))
        tl_scr[b, p, :c2] = stack(operand(4, b, p))
        tl_scr[b, p, c2:] = stack(operand(5, b, p))
        v_scr[b, p] = stack(operand(6, b, p))

    for b, p in problems:
        ar = ar_scr[b, p]
        bk = bk_scr[b, p]
        if fused:
            gram = _bdot_nt(ar, bk)
            g_ab, g_ak = gram[:c2, :c2], gram[:c2, c2:]
            g_rb, g_rk = gram[c2:, :c2], gram[c2:, c2:]
        else:
            g_ab, g_ak = _bdot_nt(ar[:c2], bk[:c2]), _bdot_nt(ar[:c2], bk[c2:])
            g_rb, g_rk = _bdot_nt(ar[c2:], bk[:c2]), _bdot_nt(ar[c2:], bk[c2:])
        p_scr[b, p] = jnp.where(strict, g_ab, 0.0).astype(BF16)
        lak_scr[b, p] = jnp.where(strict, g_ak, 0.0).astype(BF16)
        mrb_scr[b, p] = jnp.where(incl, g_rb, 0.0).astype(BF16)
        mrk_scr[b, p] = jnp.where(incl, g_rk, 0.0).astype(BF16)
        ar_state = _bdot_nt(ar, s_scr[b, p])
        r32_scr[b, p] = ar_state[:c2]
        aro_scr[b, p] = ar_state[c2:]

    for b, p in problems:
        rhs = r32_scr[b, p] + jnp.dot(lak_scr[b, p], v_scr[b, p], preferred_element_type=F32)
        r32_scr[b, p] = rhs
        rb_scr[b, p] = rhs.astype(BF16)

    for m in range(n_factors):
        last = m == n_factors - 1
        for b, p in problems:
            pw = p_scr[b, p]
            rb = rb_scr[b, p]
            if last:
                delta = jnp.dot(pw, rb, preferred_element_type=F32)
            elif fused:
                both = jnp.dot(pw, jnp.concatenate([pw, rb], axis=1), preferred_element_type=F32)
                p_scr[b, p] = both[:, :c2].astype(BF16)
                delta = both[:, c2:]
            else:
                p_scr[b, p] = jnp.dot(pw, pw, preferred_element_type=F32).astype(BF16)
                delta = jnp.dot(pw, rb, preferred_element_type=F32)
            rhs = r32_scr[b, p] + delta
            r32_scr[b, p] = rhs
            rb_scr[b, p] = rhs.astype(BF16)

    for b, p in problems:
        sl = slice(p * PAIR_W, (p + 1) * PAIR_W)
        u_s = rb_scr[b, p]
        v_s = v_scr[b, p]
        tl = tl_scr[b, p]
        if fused:
            uv = jnp.concatenate([u_s, v_s], axis=0)
            mm = jnp.concatenate([mrb_scr[b, p], mrk_scr[b, p]], axis=1)
            o_st = aro_scr[b, p] + jnp.dot(mm, uv, preferred_element_type=F32)
            upd = _bdot_tn(uv, tl)
        else:
            o_st = (aro_scr[b, p] + jnp.dot(mrb_scr[b, p], u_s, preferred_element_type=F32)
                    + jnp.dot(mrk_scr[b, p], v_s, preferred_element_type=F32))
            upd = _bdot_tn(u_s, tl[:c2]) + _bdot_tn(v_s, tl[c2:])
        o_pair = o_st[:cc] + o_st[cc:]
        if by_rows:
            o_scr[p, _seq_rows(b, cc, nseq), :] = o_pair
        else:
            lo = b * RWKV_WIDTH + p * PAIR_W
            o_scr[:, lo:lo + PAIR_W] = o_pair
        s_scr[b, p] = s_scr[b, p] * gend_scr[b][:, sl] + upd

    ones = ones_ref[...]
    inv_n = 1.0 / RWKV_HEAD

    def finish(o, r, k, v, g):
        mean = _head_sums(o, ones) * inv_n
        d = o - mean
        var = _head_sums(d * d, ones) * inv_n
        on = d * lax.rsqrt(var + LNX_EPS) * lnw_ref[...] + lnb_ref[...]
        bonus = _head_sums(r * k * rk_ref[...], ones) * v
        return (on + bonus) * g

    if by_rows:
        o = jnp.concatenate([o_scr[j] for j in range(HEAD_PAIRS)], axis=1)
        o_ref[...] = finish(o, r, k, v, g)
    else:
        for b in range(nseq):
            sb = slice(b * RWKV_WIDTH, (b + 1) * RWKV_WIDTH)
            o_ref[:, sb] = finish(o_scr[:, sb], r[:, sb], k[:, sb], v[:, sb], g[:, sb])

    @pl.when(c == pl.num_programs(1) - 1)
    def _():
        s_out_ref[...] = s_scr[...]


def _post_consts(lw, layer):
    return [lw["rwkv_lnx_w"][layer], lw["rwkv_lnx_b"][layer], lw["rwkv_r_k"][layer], lw["pair_ones"]]


def _rwkv_rec_short_kernel(*refs, chunk, nseq):
    seq_refs = refs[:7]
    s0_ref = refs[7]
    post = refs[8:12]
    o_ref, s_out_ref = refs[12:14]
    _rec_phases([x[...] for x in seq_refs], s0_ref, post, o_ref, s_out_ref, refs[14:],
                chunk=chunk, nseq=nseq, by_rows=False)


def _rwkv_rec_short(seqs, s0_bd, lw, layer, steps, batch):
    nseq, chunk = SHORT_GROUP, SHORT_CHUNK
    seq_spec = pl.BlockSpec((chunk, nseq * RWKV_WIDTH), lambda j, c: (c, j))
    st_spec = pl.BlockSpec((nseq, HEAD_PAIRS, PAIR_W, PAIR_W), lambda j, c: (j, 0, 0, 0))
    consts = _post_consts(lw, layer)
    return pl.pallas_call(
        functools.partial(_rwkv_rec_short_kernel, chunk=chunk, nseq=nseq),
        grid=(batch // nseq, steps // chunk),
        in_specs=[seq_spec] * 7 + [st_spec] + [_const_spec(a.shape) for a in consts],
        out_specs=[seq_spec, st_spec],
        out_shape=[jax.ShapeDtypeStruct((steps, batch * RWKV_WIDTH), F32),
                   jax.ShapeDtypeStruct(s0_bd.shape, F32)],
        scratch_shapes=_rec_scratch(chunk, nseq, False),
        compiler_params=_cparams("parallel", "arbitrary"),
        name="rwkv_rec",
    )(*seqs, s0_bd, *consts)


def _rwkv_long_kernel(*refs, chunk, batch, has_vfirst, n_consts):
    p_ref, shift0_ref = refs[:2]
    consts = refs[2:2 + n_consts]
    pos = 2 + n_consts
    vf_ref = refs[pos] if has_vfirst else None
    pos += int(has_vfirst)
    s0_ref = refs[pos]
    post = refs[pos + 1:pos + 5]
    pos += 5
    o_ref, v_out, shift_out, s_out_ref, carry = refs[pos:pos + 5]
    scratch = refs[pos + 5:]

    @pl.when(pl.program_id(1) == 0)
    def _():
        carry[...] = shift0_ref[...]

    p = p_ref[...]
    p_prev, new_carry = _token_shift(p, carry[...], batch)
    carry[...] = new_carry
    shift_out[...] = new_carry
    vals = _prep_math(p, p_prev, vf_ref[...] if has_vfirst else None, consts)
    v_out[...] = vals[3]
    _rec_phases(vals, s0_ref, post, o_ref, s_out_ref, scratch, chunk=chunk, nseq=batch, by_rows=True)


def _rwkv_long(proj, shift0, v_first, s0_bd, lw, layer, steps, batch):
    chunk = LONG_CHUNK
    tile = chunk * batch
    has_vfirst = v_first is not None
    consts = _prep_consts(lw, layer, has_vfirst)
    post = _post_consts(lw, layer)
    row_spec = pl.BlockSpec((tile, RWKV_WIDTH), lambda j, c: (c, 0))
    st_spec = pl.BlockSpec(s0_bd.shape, lambda j, c: (0, 0, 0, 0))
    args = [proj, shift0] + consts + ([v_first] if has_vfirst else []) + [s0_bd] + post
    in_specs = ([pl.BlockSpec((tile, RWKV_COLS), lambda j, c: (c, 0)), _const_spec(shift0.shape)]
                + [_const_spec(a.shape) for a in consts] + ([row_spec] if has_vfirst else [])
                + [_const_spec(s0_bd.shape)] + [_const_spec(a.shape) for a in post])
    rows_shape = jax.ShapeDtypeStruct((steps * batch, RWKV_WIDTH), F32)
    return pl.pallas_call(
        functools.partial(_rwkv_long_kernel, chunk=chunk, batch=batch, has_vfirst=has_vfirst,
                          n_consts=len(consts)),
        grid=(1, steps // chunk),
        in_specs=in_specs,
        out_specs=[row_spec, row_spec, pl.BlockSpec((batch, RWKV_COLS), lambda j, c: (0, 0)), st_spec],
        out_shape=[rows_shape, rows_shape, jax.ShapeDtypeStruct((batch, RWKV_COLS), F32),
                   jax.ShapeDtypeStruct(s0_bd.shape, F32)],
        scratch_shapes=[pltpu.VMEM((batch, RWKV_COLS), F32)] + _rec_scratch(chunk, batch, True),
        compiler_params=_cparams("arbitrary", "arbitrary"),
        name="rwkv_long",
    )(*args)


def _pair_blockdiag(state):
    b = state.shape[0]
    s = state.reshape(b, HEAD_PAIRS, 2, RWKV_HEAD, RWKV_HEAD)
    eye2 = jnp.eye(2, dtype=state.dtype)
    bd = s[:, :, :, :, None, :] * eye2[None, None, :, None, :, None]
    return bd.reshape(b, HEAD_PAIRS, PAIR_W, PAIR_W)


def _pair_unblock(bd):
    b = bd.shape[0]
    s = bd.reshape(b, HEAD_PAIRS, 2, RWKV_HEAD, 2, RWKV_HEAD)
    out = jnp.stack([s[:, :, 0, :, 0, :], s[:, :, 1, :, 1, :]], axis=2)
    return out.reshape(b, RWKV_HEADS, RWKV_HEAD, RWKV_HEAD)


def _s5_kernel(u0_ref, u1_ref, h0r_ref, h0i_ref, are_ref, aim_ref, ldt_ref, bre_ref, bim_ref,
               cre_ref, cim_ref, d_ref, wglu_ref, bglu_ref,
               o_ref, hr_out, hi_out, hr_c, hi_c, hre, him, *, batch):
    @pl.when(pl.program_id(0) == 0)
    def _():
        hr_c[...] = h0r_ref[...]
        hi_c[...] = h0i_ref[...]

    a_re = are_ref[...]
    a_im = aim_ref[...]
    dt = jnp.exp(ldt_ref[...])
    mag = jnp.exp(a_re * dt)
    ab_re = mag * jnp.cos(a_im * dt)
    ab_im = mag * jnp.sin(a_im * dt)
    den = a_re * a_re + a_im * a_im
    nr = ab_re - 1.0
    cf_re = (nr * a_re + ab_im * a_im) / den
    cf_im = (ab_im * a_re - nr * a_im) / den

    us = (u0_ref[...], u1_ref[...])
    tm = us[0].shape[0]
    for hf in range(2):
        ls = slice(hf * S5_HALF_L, (hf + 1) * S5_HALF_L)
        ub = us[hf].astype(BF16)
        pr = jnp.dot(ub, bre_ref[hf], preferred_element_type=F32)
        pi = jnp.dot(ub, bim_ref[hf], preferred_element_type=F32)
        hre[:, ls] = cf_re[:, ls] * pr - cf_im[:, ls] * pi
        him[:, ls] = cf_re[:, ls] * pi + cf_im[:, ls] * pr

    n_steps = tm // batch
    if n_steps <= 8:
        hr = hr_c[...]
        hi = hi_c[...]
        for s in range(n_steps):
            rows = slice(s * batch, (s + 1) * batch)
            nhr = ab_re * hr - ab_im * hi + hre[rows, :]
            nhi = ab_re * hi + ab_im * hr + him[rows, :]
            hre[rows, :] = nhr
            him[rows, :] = nhi
            hr, hi = nhr, nhi
        hr_c[...] = hr
        hi_c[...] = hi
    else:
        lane_w = 512
        for lc in range(S5_LANES // lane_w):
            ls = slice(lc * lane_w, (lc + 1) * lane_w)
            abr = jnp.broadcast_to(ab_re[:, ls], (batch, lane_w))
            abi = jnp.broadcast_to(ab_im[:, ls], (batch, lane_w))

            def body(s, carry, ls=ls, abr=abr, abi=abi):
                hr, hi = carry
                rows = pl.ds(pl.multiple_of(s * batch, batch), batch)
                nhr = abr * hr - abi * hi + hre[rows, ls]
                nhi = abr * hi + abi * hr + him[rows, ls]
                hre[rows, ls] = nhr
                him[rows, ls] = nhi
                return nhr, nhi

            hr, hi = lax.fori_loop(0, n_steps, body, (hr_c[:, ls], hi_c[:, ls]), unroll=8)
            hr_c[:, ls] = hr
            hi_c[:, ls] = hi

    hr_out[...] = hr_c[...]
    hi_out[...] = hi_c[...]

    for hf in range(2):
        ls = slice(hf * S5_HALF_L, (hf + 1) * S5_HALF_L)
        cs = slice(hf * S5_HALF_W, (hf + 1) * S5_HALF_W)
        y = (jnp.dot(hre[:, ls].astype(BF16), cre_ref[hf], preferred_element_type=F32)
             - jnp.dot(him[:, ls].astype(BF16), cim_ref[hf], preferred_element_type=F32)
             + d_ref[:, cs] * us[hf])
        o_ref[:, cs] = jax.nn.gelu(y, approximate=True)
    y = o_ref[...]
    o_ref[...] = y * jax.nn.sigmoid(_bdot(y, wglu_ref[...]) + bglu_ref[...])


def _s5(proj, h0r, h0i, lw, layer, batch):
    rows = proj.shape[0]
    tm = min(ROW_TILE, rows)
    u_blk = RWKV_COLS // S5_HALF_W
    consts = [h0r, h0i, lw["s5_a_re"][layer], lw["s5_a_im"][layer], lw["s5_log_dt"][layer],
              lw["s5_b_re"][layer], lw["s5_b_im"][layer], lw["s5_c_re"][layer], lw["s5_c_im"][layer],
              lw["s5_d"][layer], lw["s5_w_glu"][layer], lw["s5_b_glu"][layer]]
    st_shape = jax.ShapeDtypeStruct((batch, S5_LANES), F32)
    st_spec = pl.BlockSpec((batch, S5_LANES), lambda i: (0, 0))
    return pl.pallas_call(
        functools.partial(_s5_kernel, batch=batch),
        grid=(rows // tm,),
        in_specs=[pl.BlockSpec((tm, S5_HALF_W), lambda i: (i, u_blk)),
                  pl.BlockSpec((tm, S5_HALF_W), lambda i: (i, u_blk + 1))]
                 + [_const_spec(a.shape) for a in consts],
        out_specs=[pl.BlockSpec((tm, S5_WIDTH), lambda i: (i, 0)), st_spec, st_spec],
        out_shape=[jax.ShapeDtypeStruct((rows, S5_WIDTH), F32), st_shape, st_shape],
        scratch_shapes=[pltpu.VMEM((batch, S5_LANES), F32), pltpu.VMEM((batch, S5_LANES), F32),
                        pltpu.VMEM((tm, S5_LANES), F32), pltpu.VMEM((tm, S5_LANES), F32)],
        compiler_params=_cparams("arbitrary"),
        name="s5",
    )(proj, proj, *consts)


def _softmax_rows(s):
    e = jnp.exp(s - jnp.max(s, axis=-1, keepdims=True))
    return e / jnp.sum(e, axis=-1, keepdims=True)


CACHE_ROWS = N_MEM * X_HEADS * (X_HEAD_DIM // LANES)
CACHE_GROUP = X_HEADS * (X_HEAD_DIM // LANES)
Q_ROWS = 8


def _cache_view(cache):
    l, b = cache.shape[:2]
    halves = X_HEAD_DIM // LANES
    c = cache.reshape(l, b, N_MEM, X_HEADS, halves, LANES)
    return jnp.swapaxes(c, 3, 4).reshape(l, b, CACHE_ROWS, LANES)


def _attn_cache_kernel(q_ref, k_ref, v_ref, o_ref):
    halves = X_HEAD_DIM // LANES
    scale = X_HEAD_DIM ** -0.5
    tiles = _lane_tiles(D_MODEL)
    qx = jnp.concatenate([q_ref[:, sl] for sl in tiles], axis=0)
    e = _bdot_nt(qx, k_ref[...])
    col = lax.broadcasted_iota(jnp.int32, (Q_ROWS, CACHE_ROWS), 1) % CACHE_GROUP
    probs = []
    for h in range(X_HEADS):
        base = h * halves * Q_ROWS
        valid = col == h
        s = jnp.where(valid, e[base:base + Q_ROWS], 0.0)
        for j in range(1, halves):
            part = jnp.where(col == j * X_HEADS + h, e[base + j * Q_ROWS:base + (j + 1) * Q_ROWS], 0.0)
            s = s + pltpu.roll(part, shift=CACHE_ROWS - j * X_HEADS, axis=1)
        pr = _softmax_rows(jnp.where(valid, s * scale, -1e30))
        probs.append(pr)
        for j in range(1, halves):
            probs.append(pltpu.roll(pr, shift=j * X_HEADS, axis=1))
    ox = _bdot(jnp.concatenate(probs, axis=0), v_ref[...])
    for n, sl in enumerate(tiles):
        o_ref[:, sl] = ox[n * Q_ROWS:(n + 1) * Q_ROWS]


def _attn_cache(q, cache_k, cache_v, layer, steps, batch):
    q = jnp.pad(q, ((0, Q_ROWS - steps), (0, 0)))
    q_spec = pl.BlockSpec((Q_ROWS, D_MODEL), lambda b: (0, b))
    m_spec = pl.BlockSpec((None, None, CACHE_ROWS, LANES), lambda b: (layer, b, 0, 0))
    out = pl.pallas_call(
        _attn_cache_kernel,
        grid=(batch,),
        in_specs=[q_spec, m_spec, m_spec],
        out_specs=q_spec,
        out_shape=jax.ShapeDtypeStruct((Q_ROWS, batch * D_MODEL), F32),
        compiler_params=_cparams("parallel"),
        name="mem_attn",
    )(q, cache_k, cache_v)
    return out[:steps]


def _cross_long_kernel(x_ref, gain_ref, wq_ref, k_ref, v_ref, wo_ref, o_ref,
                       q_scr, att_scr, s_scr, p_scr, *, batch):
    x = x_ref[...]
    steps = x.shape[0] // batch
    scale = X_HEAD_DIM ** -0.5
    tiles_per_head = X_HEAD_DIM // LANES
    n_tiles = D_MODEL // LANES
    blocks = [(b, h) for b in range(batch) for h in range(X_HEADS)]
    _stage(q_scr, jnp.dot(_rms(x, gain_ref[...]).astype(BF16), wq_ref[...],
                          preferred_element_type=F32))
    for i, (b, h) in enumerate(blocks):
        rows = _seq_rows(b, steps, batch)
        tiles = range(h * tiles_per_head, (h + 1) * tiles_per_head)
        q = jnp.concatenate([q_scr[j, rows, :] for j in tiles], axis=1)
        s_scr[i] = _bdot_nt(q, k_ref[b, :, h * X_HEAD_DIM:(h + 1) * X_HEAD_DIM]) * scale
    p_scr[...] = _softmax_rows(s_scr[...]).astype(BF16)
    for i, (b, h) in enumerate(blocks):
        rows = _seq_rows(b, steps, batch)
        o = jnp.dot(p_scr[i], v_ref[b, :, h * X_HEAD_DIM:(h + 1) * X_HEAD_DIM],
                    preferred_element_type=F32)
        for t in range(tiles_per_head):
            att_scr[h * tiles_per_head + t, rows, :] = o[:, t * LANES:(t + 1) * LANES]
    att = jnp.concatenate([att_scr[j] for j in range(n_tiles)], axis=1)
    o_ref[...] = x + jnp.dot(att.astype(BF16), wo_ref[...], preferred_element_type=F32)


def _cross_long(x, mem_k, mem_v, lw, layer, batch):
    rows = x.shape[0]
    tm = min(ATTN_ROW_TILE, rows)
    steps = tm // batch
    consts = [lw["norm_cross"][layer], lw["w_cq"][layer], mem_k, mem_v, lw["w_co"][layer]]
    return pl.pallas_call(
        functools.partial(_cross_long_kernel, batch=batch),
        grid=(rows // tm,),
        in_specs=[pl.BlockSpec((tm, D_MODEL), lambda i: (i, 0))] + [_const_spec(a.shape) for a in consts],
        out_specs=pl.BlockSpec((tm, D_MODEL), lambda i: (i, 0)),
        out_shape=jax.ShapeDtypeStruct((rows, D_MODEL), F32),
        scratch_shapes=[pltpu.VMEM((D_MODEL // LANES, tm, LANES), F32),
                        pltpu.VMEM((D_MODEL // LANES, tm, LANES), F32),
                        pltpu.VMEM((batch * X_HEADS, steps, N_MEM), F32),
                        pltpu.VMEM((batch * X_HEADS, steps, N_MEM), BF16)],
        compiler_params=_cparams("parallel"),
        name="cross_attn",
    )(x, *consts)


def _ffn_kernel(x_ref, buf0_ref, gain_ref, wg_ref, wu_ref, cw_ref, cb_ref, wd_ref,
                o_ref, buf_out, carry, *, batch):
    @pl.when(pl.program_id(0) == 0)
    def _():
        carry[...] = buf0_ref[...]

    x = x_ref[...]
    tm = x.shape[0]
    h = _rms(x, gain_ref[...]).astype(BF16)
    gt = jnp.dot(h, wg_ref[...], preferred_element_type=F32)
    up = jnp.dot(h, wu_ref[...], preferred_element_type=F32)
    padded = jnp.concatenate([carry[...], gt], axis=0)
    conv = cb_ref[...]
    for i in range(CONV_W):
        conv = conv + cw_ref[i:i + 1, :] * padded[i * batch:i * batch + tm]
    new_carry = padded[tm:]
    carry[...] = new_carry
    buf_out[...] = new_carry
    act = jax.nn.silu(conv) * up
    o_ref[...] = x + jnp.dot(act.astype(BF16), wd_ref[...], preferred_element_type=F32)


def _ffn(x, buf0, lw, layer, batch):
    rows = x.shape[0]
    tm = min(FFN_ROW_TILE, rows)
    tm = max(tm, (CONV_W - 1) * batch)
    consts = [buf0, lw["norm_ffn"][layer], lw["w_gate"][layer], lw["w_up"][layer],
              lw["ffn_conv_w"][layer], lw["ffn_conv_b"][layer], lw["w_down"][layer]]
    nbuf = (CONV_W - 1) * batch
    return pl.pallas_call(
        functools.partial(_ffn_kernel, batch=batch),
        grid=(rows // tm,),
        in_specs=[pl.BlockSpec((tm, D_MODEL), lambda i: (i, 0))] + [_const_spec(a.shape) for a in consts],
        out_specs=[pl.BlockSpec((tm, D_MODEL), lambda i: (i, 0)),
                   pl.BlockSpec((nbuf, D_FF), lambda i: (0, 0))],
        out_shape=[jax.ShapeDtypeStruct((rows, D_MODEL), F32), jax.ShapeDtypeStruct((nbuf, D_FF), F32)],
        scratch_shapes=[pltpu.VMEM((nbuf, D_FF), F32)],
        compiler_params=_cparams("arbitrary"),
        name="conv_ffn",
    )(x, *consts)


def _s5_in_blockdiag(b):
    l = b.shape[0]
    b = b.reshape(l, 2, S5_GROUPS // 2, S5_STATE, S5_GROUP)
    eye = jnp.eye(S5_GROUPS // 2, dtype=b.dtype)
    m = jnp.einsum('lfgph,gk->lfghkp', b, eye)
    return m.reshape(l, 2, S5_HALF_W, S5_HALF_L).astype(BF16)


def _s5_out_blockdiag(c):
    l = c.shape[0]
    c = c.reshape(l, 2, S5_GROUPS // 2, S5_GROUP, S5_STATE)
    eye = jnp.eye(S5_GROUPS // 2, dtype=c.dtype)
    m = jnp.einsum('lfgnp,gk->lfgpkn', c, eye)
    return m.reshape(l, 2, S5_HALF_L, S5_HALF_W).astype(BF16)


def _prep_weights(p):
    l = DEPTH
    row = lambda a: a.reshape(a.shape[0], 1, -1)
    lw = {}
    for name in ("norm_mix", "shift_mu", "rwkv_w0", "rwkv_a0", "rwkv_v0", "rwkv_k_k", "rwkv_k_a",
                 "rwkv_r_k", "rwkv_lnx_w", "rwkv_lnx_b", "s5_d", "s5_b_glu", "norm_cross", "norm_ffn",
                 "ffn_conv_b", "s5_a_re", "s5_a_im"):
        lw[name] = row(p[name])
    lw["s5_log_dt"] = row(jnp.repeat(p["s5_log_dt"], S5_STATE, axis=-1))
    lw["norm_final"] = p["norm_final"].reshape(1, -1)
    lw["ffn_conv_w"] = p["ffn_conv_w"]
    for name in ("w_in", "rwkv_w_g2", "s5_w_glu", "w_cq", "w_ck", "w_cv", "w_co", "w_gate", "w_up",
                 "w_down"):
        lw[name] = p[name].astype(BF16)
    w_out = p["w_out"].astype(BF16)
    lw["w_out_rw"] = w_out[:, :RWKV_WIDTH]
    lw["w_out_s5"] = w_out[:, RWKV_WIDTH:]
    z64 = jnp.zeros((l, LORA_PAD - 64, RWKV_WIDTH), F32)
    lw["rwkv_w_w2p"] = jnp.concatenate([p["rwkv_w_w2"], z64], axis=1).astype(BF16)
    lw["rwkv_w_a2p"] = jnp.concatenate([z64, p["rwkv_w_a2"]], axis=1).astype(BF16)
    v_lora = p["rwkv_w_v1"].shape[-1]
    lw["rwkv_w_v1p"] = jnp.pad(p["rwkv_w_v1"], ((0, 0), (0, 0), (0, LORA_PAD - v_lora))).astype(BF16)
    lw["rwkv_w_v2p"] = jnp.pad(p["rwkv_w_v2"], ((0, 0), (0, LORA_PAD - v_lora), (0, 0))).astype(BF16)
    head = jnp.arange(PAIR_W) // RWKV_HEAD
    lw["pair_ones"] = (head[:, None] == head[None, :]).astype(BF16)
    lw["s5_b_re"] = _s5_in_blockdiag(p["s5_b_re"])
    lw["s5_b_im"] = _s5_in_blockdiag(p["s5_b_im"])
    lw["s5_c_re"] = _s5_out_blockdiag(p["s5_c_re"])
    lw["s5_c_im"] = _s5_out_blockdiag(p["s5_c_im"])
    return lw


def _run_trunk(x, mem_k, mem_v, st_rwkv, st_shift, st_re, st_im, st_conv, lw, steps, batch):
    rows = steps * batch
    long_seq = steps >= LONG_SEQ
    steps_pad = -(-steps // SHORT_CHUNK) * SHORT_CHUNK
    if long_seq:
        mem_k = mem_k.astype(BF16)
        mem_v = mem_v.astype(BF16)
    else:
        mem_k = _cache_view(mem_k)
        mem_v = _cache_view(mem_v)
    v_first = None
    new_rw, new_shift, new_re, new_im, new_conv = [], [], [], [], []
    for l in range(DEPTH):
        proj = _rowmm([x], [lw["w_in"][l]], gain=lw["norm_mix"][l], name="in_proj")
        s0_bd = _pair_blockdiag(st_rwkv[l])
        if long_seq:
            o_rw, v_l, sh, s_rw = _rwkv_long(proj, st_shift[l], v_first, s0_bd, lw, l, steps, batch)
        else:
            *seqs, sh = _rwkv_prep(proj, st_shift[l], v_first, lw, l, batch)
            v_l = seqs[3]
            seqs = [jnp.pad(s.reshape(steps, batch * RWKV_WIDTH), ((0, steps_pad - steps), (0, 0)))
                    for s in seqs]
            o_rw, s_rw = _rwkv_rec_short(seqs, s0_bd, lw, l, steps_pad, batch)
            o_rw = o_rw[:steps].reshape(rows, RWKV_WIDTH)
        if l == 0:
            v_first = v_l
        o_s5, hr, hi = _s5(proj, st_re[l], st_im[l], lw, l, batch)
        x = _rowmm([o_rw, o_s5], [lw["w_out_rw"][l], lw["w_out_s5"][l]], resid=x, name="mix_out")
        if long_seq:
            x = _cross_long(x, mem_k[l], mem_v[l], lw, l, batch)
        else:
            q = _rowmm([x], [lw["w_cq"][l]], gain=lw["norm_cross"][l], name="cross_q")
            att = _attn_cache(q.reshape(steps, batch * D_MODEL), mem_k, mem_v, l, steps, batch)
            x = _rowmm([att.reshape(rows, D_MODEL)], [lw["w_co"][l]], resid=x, name="cross_o")
        x, cb = _ffn(x, st_conv[l], lw, l, batch)
        new_rw.append(_pair_unblock(s_rw))
        new_shift.append(sh)
        new_re.append(hr)
        new_im.append(hi)
        new_conv.append(cb)
    return (x, jnp.stack(new_rw), jnp.stack(new_shift), jnp.stack(new_re), jnp.stack(new_im),
            jnp.stack(new_conv))


def _group(x, mem_k, mem_v, st_rwkv, st_shift, st_re, st_im, st_conv, lw):
    b, t, _ = x.shape
    long_seq = t >= LONG_SEQ
    if long_seq:
        xt = _to_time_major(x)
    else:
        xt = jnp.swapaxes(x, 0, 1).reshape(t * b, D_MODEL)
    conv_t = jnp.swapaxes(st_conv, 1, 2).reshape(DEPTH, (CONV_W - 1) * b, D_FF)
    y, rw, sh, re, im, cv = _run_trunk(
        xt, mem_k, mem_v, st_rwkv, st_shift, st_re.reshape(DEPTH, b, S5_LANES),
        st_im.reshape(DEPTH, b, S5_LANES), conv_t, lw, t, b)
    if long_seq:
        y = _final_norm_long(y, lw["norm_final"], b)
    else:
        y = jnp.swapaxes(_rownorm(y, lw["norm_final"]).reshape(t, b, D_MODEL), 0, 1)
    cv = jnp.swapaxes(cv.reshape(DEPTH, CONV_W - 1, b, D_FF), 1, 2)
    return (y, rw, sh, re.reshape(DEPTH, b, S5_GROUPS, S5_STATE),
            im.reshape(DEPTH, b, S5_GROUPS, S5_STATE), cv)


def kernel(x_prompt, x_sample, mem_prompt, state_rwkv, state_shift, state_s5_re, state_s5_im, state_ffn_conv, cache_mem_k, cache_mem_v, norm_mix, w_in, shift_mu, rwkv_w0, rwkv_w_w2, rwkv_a0, rwkv_w_a2, rwkv_v0, rwkv_w_v1, rwkv_w_v2, rwkv_w_g2, rwkv_k_k, rwkv_k_a, rwkv_r_k, rwkv_lnx_w, rwkv_lnx_b, s5_a_re, s5_a_im, s5_log_dt, s5_b_re, s5_b_im, s5_c_re, s5_c_im, s5_d, s5_w_glu, s5_b_glu, w_out, norm_cross, w_cq, w_ck, w_cv, w_co, norm_ffn, w_gate, w_up, ffn_conv_w, ffn_conv_b, w_down, norm_final):
    lw = _prep_weights(dict(
        norm_mix=norm_mix, w_in=w_in, shift_mu=shift_mu, rwkv_w0=rwkv_w0, rwkv_w_w2=rwkv_w_w2,
        rwkv_a0=rwkv_a0, rwkv_w_a2=rwkv_w_a2, rwkv_v0=rwkv_v0, rwkv_w_v1=rwkv_w_v1,
        rwkv_w_v2=rwkv_w_v2, rwkv_w_g2=rwkv_w_g2, rwkv_k_k=rwkv_k_k, rwkv_k_a=rwkv_k_a,
        rwkv_r_k=rwkv_r_k, rwkv_lnx_w=rwkv_lnx_w, rwkv_lnx_b=rwkv_lnx_b, s5_a_re=s5_a_re,
        s5_a_im=s5_a_im, s5_log_dt=s5_log_dt, s5_b_re=s5_b_re, s5_b_im=s5_b_im, s5_c_re=s5_c_re,
        s5_c_im=s5_c_im, s5_d=s5_d, s5_w_glu=s5_w_glu, s5_b_glu=s5_b_glu, w_out=w_out,
        norm_cross=norm_cross, w_cq=w_cq, w_ck=w_ck, w_cv=w_cv, w_co=w_co, norm_ffn=norm_ffn,
        w_gate=w_gate, w_up=w_up, ffn_conv_w=ffn_conv_w, ffn_conv_b=ffn_conv_b, w_down=w_down,
        norm_final=norm_final))
    bp = x_prompt.shape[0]
    mem_rows = mem_prompt.reshape(bp * N_MEM, D_MODEL)
    p_mem_k = jnp.stack([_rowmm([mem_rows], [lw["w_ck"][l]], name="mem_k") for l in range(DEPTH)])
    p_mem_v = jnp.stack([_rowmm([mem_rows], [lw["w_cv"][l]], name="mem_v") for l in range(DEPTH)])
    p_mem_k = p_mem_k.reshape(DEPTH, bp, N_MEM, D_MODEL)
    p_mem_v = p_mem_v.reshape(DEPTH, bp, N_MEM, D_MODEL)
    z_rw = jnp.zeros((DEPTH, bp, RWKV_HEADS, RWKV_HEAD, RWKV_HEAD), F32)
    z_shift = jnp.zeros((DEPTH, bp, RWKV_COLS), F32)
    z_s5 = jnp.zeros((DEPTH, bp, S5_GROUPS, S5_STATE), F32)
    z_conv = jnp.zeros((DEPTH, bp, CONV_W - 1, D_FF), F32)
    y_prompt, p_rwkv, p_shift, p_re, p_im, p_conv = _group(
        x_prompt, p_mem_k, p_mem_v, z_rw, z_shift, z_s5, z_s5, z_conv, lw)
    y_sample, s_rwkv, s_shift, s_re, s_im, s_conv = _group(
        x_sample, cache_mem_k, cache_mem_v, state_rwkv, state_shift, state_s5_re, state_s5_im,
        state_ffn_conv, lw)
    return (y_prompt, y_sample, p_rwkv, p_shift, p_re, p_im, p_conv,
            p_mem_k.reshape(DEPTH, bp, N_MEM, X_HEADS, X_HEAD_DIM),
            p_mem_v.reshape(DEPTH, bp, N_MEM, X_HEADS, X_HEAD_DIM),
            s_rwkv, s_shift, s_re, s_im, s_conv)
```

```python
import functools
import math

import jax
import jax.numpy as jnp
from jax import lax
from jax.experimental import pallas as pl
from jax.experimental.pallas import tpu as pltpu

F32 = jnp.float32
BF16 = jnp.bfloat16

D_MODEL = 1024
DEPTH = 4
RWKV_WIDTH = 512
RWKV_HEAD = 64
RWKV_HEADS = 8
HEAD_PAIRS = RWKV_HEADS // 2
LANES = 128
PAIR_W = 2 * RWKV_HEAD
LORA_PAD = 128
RWKV_COLS = 3 * RWKV_WIDTH + 64 + 64 + 128
S5_WIDTH = 512
S5_GROUP = 16
S5_GROUPS = 32
S5_STATE = 64
S5_LANES = S5_GROUPS * S5_STATE
S5_HALF_W = S5_WIDTH // 2
S5_HALF_L = S5_LANES // 2
IN_COLS = RWKV_COLS + S5_WIDTH
N_MEM = 256
X_HEADS = 4
X_HEAD_DIM = 256
D_FF = 2816
CONV_W = 3
RMS_EPS = 1e-6
LNX_EPS = 64e-5

ROW_TILE = 512
FFN_ROW_TILE = 512
ATTN_ROW_TILE = 1024
LONG_SEQ = 64
LONG_CHUNK = 64
SHORT_CHUNK = 8
SEQ_GROUP = 8
VMEM_LIMIT = 56 * 1024 * 1024


def _cparams(*sem):
    return pltpu.CompilerParams(dimension_semantics=sem, vmem_limit_bytes=VMEM_LIMIT)


def _const_spec(shape):
    nd = len(shape)
    return pl.BlockSpec(shape, lambda *_: (0,) * nd, pipeline_mode=pl.Buffered(1))


def _bdot(a, b):
    return jnp.dot(a.astype(BF16), b.astype(BF16), preferred_element_type=F32)


def _bdot_nt(a, b):
    return lax.dot_general(a.astype(BF16), b.astype(BF16), (((1,), (1,)), ((), ())),
                           preferred_element_type=F32)


def _bdot_tn(a, b):
    return lax.dot_general(a.astype(BF16), b.astype(BF16), (((0,), (0,)), ((), ())),
                           preferred_element_type=F32)


def _head_sums(x, pair_ones):
    outs = []
    for sl in _lane_tiles(x.shape[1]):
        hi = x[:, sl].astype(BF16)
        lo = (x[:, sl] - hi.astype(F32)).astype(BF16)
        outs.append(jnp.dot(hi, pair_ones, preferred_element_type=F32)
                    + jnp.dot(lo, pair_ones, preferred_element_type=F32))
    return jnp.concatenate(outs, axis=1)


def _rms(x, gain):
    return x * lax.rsqrt(jnp.mean(x * x, axis=-1, keepdims=True) + RMS_EPS) * gain


def _seq_rows(seq, steps, batch):
    return pl.ds(seq, steps, stride=batch)


def _lane_tiles(width):
    return [slice(j * LANES, (j + 1) * LANES) for j in range(width // LANES)]


def _stage(scr, x):
    for j, sl in enumerate(_lane_tiles(x.shape[1])):
        scr[j] = x[:, sl]


def _to_time_major_kernel(x_ref, o_ref, scr, *, batch, steps):
    tiles = _lane_tiles(x_ref.shape[2])
    for b in range(batch):
        for j, sl in enumerate(tiles):
            scr[j, _seq_rows(b, steps, batch), :] = x_ref[b, :, sl]
    for j, sl in enumerate(tiles):
        o_ref[:, sl] = scr[j]


def _to_time_major(x):
    b, t, d = x.shape
    tt = ROW_TILE // b
    return pl.pallas_call(
        functools.partial(_to_time_major_kernel, batch=b, steps=tt),
        grid=(t // tt,),
        in_specs=[pl.BlockSpec((b, tt, d), lambda i: (0, i, 0))],
        out_specs=pl.BlockSpec((tt * b, d), lambda i: (i, 0)),
        out_shape=jax.ShapeDtypeStruct((t * b, d), F32),
        scratch_shapes=[pltpu.VMEM((d // LANES, tt * b, LANES), F32)],
        compiler_params=_cparams("parallel"),
        name="to_time_major",
    )(x)


def _final_norm_long_kernel(x_ref, g_ref, o_ref, scr, *, batch, steps):
    _stage(scr, _rms(x_ref[...], g_ref[...]))
    for b in range(batch):
        for j, sl in enumerate(_lane_tiles(x_ref.shape[1])):
            o_ref[b, :, sl] = scr[j, _seq_rows(b, steps, batch), :]


def _final_norm_long(x, gain, batch):
    rows, d = x.shape
    t = rows // batch
    tt = ROW_TILE // batch
    return pl.pallas_call(
        functools.partial(_final_norm_long_kernel, batch=batch, steps=tt),
        grid=(t // tt,),
        in_specs=[pl.BlockSpec((tt * batch, d), lambda i: (i, 0)), _const_spec(gain.shape)],
        out_specs=pl.BlockSpec((batch, tt, d), lambda i: (0, i, 0)),
        out_shape=jax.ShapeDtypeStruct((batch, t, d), F32),
        scratch_shapes=[pltpu.VMEM((d // LANES, tt * batch, LANES), F32)],
        compiler_params=_cparams("parallel"),
        name="final_norm",
    )(x, gain)


def _rowmm_kernel(*refs, n_x, has_gain, has_resid):
    xs = refs[:n_x]
    ws = refs[n_x:2 * n_x]
    pos = 2 * n_x
    gain = refs[pos] if has_gain else None
    pos += int(has_gain)
    resid = refs[pos] if has_resid else None
    pos += int(has_resid)
    o_ref = refs[pos]
    acc = None
    for x_ref, w_ref in zip(xs, ws):
        x = x_ref[...]
        if has_gain:
            x = _rms(x, gain[...])
        d = jnp.dot(x.astype(BF16), w_ref[...], preferred_element_type=F32)
        acc = d if acc is None else acc + d
    if has_resid:
        acc = acc + resid[...]
    o_ref[...] = acc


def _rowmm(xs, ws, gain=None, resid=None, name="rowmm"):
    rows = xs[0].shape[0]
    n_out = ws[0].shape[1]
    tm = min(ROW_TILE, rows)
    in_specs = [pl.BlockSpec((tm, x.shape[1]), lambda i: (i, 0)) for x in xs]
    in_specs += [_const_spec(w.shape) for w in ws]
    args = list(xs) + list(ws)
    if gain is not None:
        in_specs.append(_const_spec(gain.shape))
        args.append(gain)
    if resid is not None:
        in_specs.append(pl.BlockSpec((tm, n_out), lambda i: (i, 0)))
        args.append(resid)
    return pl.pallas_call(
        functools.partial(_rowmm_kernel, n_x=len(xs), has_gain=gain is not None,
                          has_resid=resid is not None),
        grid=(rows // tm,),
        in_specs=in_specs,
        out_specs=pl.BlockSpec((tm, n_out), lambda i: (i, 0)),
        out_shape=jax.ShapeDtypeStruct((rows, n_out), F32),
        compiler_params=_cparams("parallel"),
        name=name,
    )(*args)


def _rownorm_kernel(x_ref, g_ref, o_ref):
    o_ref[...] = _rms(x_ref[...], g_ref[...])


def _rownorm(x, gain):
    rows, d = x.shape
    tm = min(ROW_TILE, rows)
    return pl.pallas_call(
        _rownorm_kernel,
        grid=(rows // tm,),
        in_specs=[pl.BlockSpec((tm, d), lambda i: (i, 0)), _const_spec(gain.shape)],
        out_specs=pl.BlockSpec((tm, d), lambda i: (i, 0)),
        out_shape=jax.ShapeDtypeStruct((rows, d), F32),
        compiler_params=_cparams("parallel"),
        name="final_norm",
    )(x, gain)


def _softplus(z):
    return jnp.maximum(z, 0.0) + jnp.log1p(jnp.exp(-jnp.abs(z)))


def _prep_consts(lw, layer, has_vfirst):
    names = ["shift_mu", "rwkv_w0", "rwkv_w_w2p", "rwkv_a0", "rwkv_w_a2p", "rwkv_w_g2", "rwkv_k_k",
             "rwkv_k_a"]
    consts = [lw[n][layer] for n in names] + [lw["pair_ones"]]
    if has_vfirst:
        consts += [lw[n][layer - 1] for n in ("rwkv_v0", "rwkv_w_v1p", "rwkv_w_v2p")]
    return consts


def _token_shift(p, carry, batch):
    tm = p.shape[0]
    prev = jnp.concatenate([carry, p[:tm - batch]], axis=0) if tm > batch else carry
    return prev, p[tm - batch:]


def _prep_math(p, p_prev, v_first, consts):
    mu, w0, ww2, a0, wa2, wg2, k_k, k_a, pair_ones = [c[...] for c in consts[:9]]
    q = p + (p_prev - p) * mu
    rw = RWKV_WIDTH
    r = q[:, 0:rw]
    k = q[:, rw:2 * rw]
    v = q[:, 2 * rw:3 * rw]
    x_wa = q[:, 3 * rw:3 * rw + LORA_PAD]
    x_g = q[:, 3 * rw + LORA_PAD:3 * rw + 2 * LORA_PAD]
    w = -_softplus(-(w0 + _bdot(jnp.tanh(x_wa), ww2))) - 0.5
    log_decay = -jnp.exp(w)
    a = jax.nn.sigmoid(a0 + _bdot(x_wa, wa2))
    g = _bdot(jax.nn.sigmoid(x_g), wg2)
    if v_first is not None:
        v0, wv1, wv2 = [c[...] for c in consts[9:12]]
        mix = jax.nn.sigmoid(v0 + _bdot(_bdot(v, wv1), wv2))
        v = v + (v_first - v) * mix
    kk = k * k_k
    kk = kk * lax.rsqrt(jnp.maximum(_head_sums(kk * kk, pair_ones), 1e-24))
    k = k * (1.0 + (a - 1.0) * k_a)
    return r, log_decay, k, v, kk, a, g


def _rec_scratch(chunk, nseq):
    c2 = 2 * chunk
    per = (nseq, HEAD_PAIRS)
    return [
        pltpu.VMEM(per + (PAIR_W, PAIR_W), F32),
        pltpu.VMEM(per + (2 * c2, PAIR_W), BF16),
        pltpu.VMEM(per + (2 * c2, PAIR_W), BF16),
        pltpu.VMEM(per + (c2, PAIR_W), BF16),
        pltpu.VMEM(per + (2 * c2, PAIR_W), BF16),
        pltpu.VMEM(per + (c2, c2), BF16),
        pltpu.VMEM(per + (c2, c2), BF16),
        pltpu.VMEM(per + (c2, c2), BF16),
        pltpu.VMEM(per + (c2, c2), BF16),
        pltpu.VMEM(per + (c2, PAIR_W), BF16),
        pltpu.VMEM(per + (c2, PAIR_W), F32),
        pltpu.VMEM(per + (c2, PAIR_W), F32),
        pltpu.VMEM((HEAD_PAIRS, chunk * nseq, PAIR_W), F32),
        pltpu.VMEM((nseq, 1, RWKV_WIDTH), F32),
        pltpu.VMEM((7, HEAD_PAIRS, chunk * nseq, PAIR_W), F32),
    ]


def _rec_phases(vals, s0_ref, post_consts, s_out_ref, scratch, *, chunk, nseq):
    (s_scr, ar_scr, bk_scr, v_scr, tl_scr, p_scr, lak_scr, mrb_scr, mrk_scr, rb_scr, r32_scr,
     aro_scr, o_scr, gend_scr, stage_scr) = scratch
    r, lw, k, v, kk, a, g = vals
    lnw_ref, lnb_ref, rk_ref, ones_ref = post_consts
    c = pl.program_id(1)
    cc = chunk
    c2 = 2 * cc
    fused = c2 % 128 == 0
    problems = [(b, p) for b in range(nseq) for p in range(HEAD_PAIRS)]

    @pl.when(c == 0)
    def _():
        zero = jnp.zeros((RWKV_HEAD, RWKV_HEAD), F32)
        for b, p in problems:
            top = jnp.concatenate([s0_ref[b, 2 * p], zero], axis=1)
            bottom = jnp.concatenate([zero, s0_ref[b, 2 * p + 1]], axis=1)
            s_scr[b, p] = jnp.concatenate([top, bottom], axis=0)

    lane = lax.broadcasted_iota(jnp.int32, (1, PAIR_W), 1)
    first = lane < RWKV_HEAD
    row2 = lax.broadcasted_iota(jnp.int32, (c2, c2), 0)
    col2 = lax.broadcasted_iota(jnp.int32, (c2, c2), 1)
    same = (row2 >= cc) == (col2 >= cc)
    rr = jnp.where(row2 >= cc, row2 - cc, row2)
    cl = jnp.where(col2 >= cc, col2 - cc, col2)
    strict = same & (cl < rr)
    incl = same & (cl <= rr)
    n_factors = max(1, math.ceil(math.log2(cc)))

    def stack(xs):
        return jnp.concatenate([jnp.where(first, xs, 0.0), jnp.where(first, 0.0, xs)],
                               axis=0).astype(BF16)

    cum = lw
    shift = nseq
    while shift < cc * nseq:
        cum = cum + jnp.concatenate([jnp.zeros((shift, RWKV_WIDTH), F32), cum[:-shift]], axis=0)
        shift *= 2
    cum_end = cum[(cc - 1) * nseq:]
    for b in range(nseq):
        gend_scr[b] = jnp.exp(cum_end[b:b + 1])
    cum_end = jnp.broadcast_to(cum_end[None], (cc, nseq, RWKV_WIDTH)).reshape(cc * nseq, RWKV_WIDTH)
    g_inv = jnp.exp(-cum)
    g_tail = jnp.exp(cum_end - cum)
    kka = kk * a
    operands = [-kk * jnp.exp(cum - lw), r * jnp.exp(cum),
                kka * g_inv, k * g_inv,
                kka * g_tail, k * g_tail, v]
    for i, x in enumerate(operands):
        _stage(stage_scr.at[i], x)

    def operand(i, b, p):
        return stage_scr[i, p, _seq_rows(b, cc, nseq), :]

    for b, p in problems:
        ar_scr[b, p, :c2] = stack(operand(0, b, p))
        ar_scr[b, p, c2:] = stack(operand(1, b, p))
        bk_scr[b, p, :c2] = stack(operand(2, b, p))
        bk_scr[b, p, c2:] = stack(operand(3, b, p))
        tl_scr[b, p, :c2] = stack(operand(4, b, p))
        tl_scr[b, p, c2:] = stack(operand(5, b, p))
        v_scr[b, p] = stack(operand(6, b, p))

    for b, p in problems:
        ar = ar_scr[b, p]
        bk = bk_scr[b, p]
        if fused:
            gram = _bdot_nt(ar, bk)
            g_ab, g_ak = gram[:c2, :c2], gram[:c2, c2:]
            g_rb, g_rk = gram[c2:, :c2], gram[c2:, c2:]
        else:
            g_ab, g_ak = _bdot_nt(ar[:c2], bk[:c2]), _bdot_nt(ar[:c2], bk[c2:])
            g_rb, g_rk = _bdot_nt(ar[c2:], bk[:c2]), _bdot_nt(ar[c2:], bk[c2:])
        p_scr[b, p] = jnp.where(strict, g_ab, 0.0).astype(BF16)
        lak_scr[b, p] = jnp.where(strict, g_ak, 0.0).astype(BF16)
        mrb_scr[b, p] = jnp.where(incl, g_rb, 0.0).astype(BF16)
        mrk_scr[b, p] = jnp.where(incl, g_rk, 0.0).astype(BF16)
        ar_state = _bdot_nt(ar, s_scr[b, p])
        r32_scr[b, p] = ar_state[:c2]
        aro_scr[b, p] = ar_state[c2:]

    for b, p in problems:
        rhs = r32_scr[b, p] + jnp.dot(lak_scr[b, p], v_scr[b, p], preferred_element_type=F32)
        r32_scr[b, p] = rhs
        rb_scr[b, p] = rhs.astype(BF16)

    for m in range(n_factors):
        last = m == n_factors - 1
        for b, p in problems:
            pw = p_scr[b, p]
            rb = rb_scr[b, p]
            if last:
                delta = jnp.dot(pw, rb, preferred_element_type=F32)
            elif fused:
                both = jnp.dot(pw, jnp.concatenate([pw, rb], axis=1), preferred_element_type=F32)
                p_scr[b, p] = both[:, :c2].astype(BF16)
                delta = both[:, c2:]
            else:
                p_scr[b, p] = jnp.dot(pw, pw, preferred_element_type=F32).astype(BF16)
                delta = jnp.dot(pw, rb, preferred_element_type=F32)
            rhs = r32_scr[b, p] + delta
            r32_scr[b, p] = rhs
            rb_scr[b, p] = rhs.astype(BF16)

    for b, p in problems:
        sl = slice(p * PAIR_W, (p + 1) * PAIR_W)
        u_s = rb_scr[b, p]
        v_s = v_scr[b, p]
        tl = tl_scr[b, p]
        if fused:
            uv = jnp.concatenate([u_s, v_s], axis=0)
            mm = jnp.concatenate([mrb_scr[b, p], mrk_scr[b, p]], axis=1)
            o_st = aro_scr[b, p] + jnp.dot(mm, uv, preferred_element_type=F32)
            upd = _bdot_tn(uv, tl)
        else:
            o_st = (aro_scr[b, p] + jnp.dot(mrb_scr[b, p], u_s, preferred_element_type=F32)
                    + jnp.dot(mrk_scr[b, p], v_s, preferred_element_type=F32))
            upd = _bdot_tn(u_s, tl[:c2]) + _bdot_tn(v_s, tl[c2:])
        o_scr[p, _seq_rows(b, cc, nseq), :] = o_st[:cc] + o_st[cc:]
        s_scr[b, p] = s_scr[b, p] * gend_scr[b][:, sl] + upd

    @pl.when(c == pl.num_programs(1) - 1)
    def _():
        for b, p in problems:
            s_pair = s_scr[b, p]
            s_out_ref[b, 2 * p] = s_pair[:RWKV_HEAD, :RWKV_HEAD]
            s_out_ref[b, 2 * p + 1] = s_pair[RWKV_HEAD:, RWKV_HEAD:]

    ones = ones_ref[...]
    inv_n = 1.0 / RWKV_HEAD
    o = jnp.concatenate([o_scr[j] for j in range(HEAD_PAIRS)], axis=1)
    mean = _head_sums(o, ones) * inv_n
    d = o - mean
    var = _head_sums(d * d, ones) * inv_n
    on = d * lax.rsqrt(var + LNX_EPS) * lnw_ref[...] + lnb_ref[...]
    bonus = _head_sums(r * k * rk_ref[...], ones) * v
    return (on + bonus) * g


def _post_consts(lw, layer):
    return [lw["rwkv_lnx_w"][layer], lw["rwkv_lnx_b"][layer], lw["rwkv_r_k"][layer], lw["pair_ones"]]


def _rwkv_group_kernel(*refs, chunk, steps, nseq, has_vfirst, n_consts):
    p_ref, shift0_ref = refs[:2]
    consts = refs[2:2 + n_consts]
    pos = 2 + n_consts
    vf_ref = refs[pos] if has_vfirst else None
    pos += int(has_vfirst)
    s0_ref = refs[pos]
    post = refs[pos + 1:pos + 5]
    pos += 5
    o_ref, v_out, shift_out, s_out_ref, carry = refs[pos:pos + 5]
    scratch = refs[pos + 5:]
    rows = steps * nseq

    @pl.when(pl.program_id(1) == 0)
    def _():
        carry[...] = shift0_ref[...]

    p = p_ref[...].reshape(rows, RWKV_COLS)
    p_prev, new_carry = _token_shift(p, carry[...], nseq)
    carry[...] = new_carry
    shift_out[...] = new_carry
    v_first = vf_ref[...].reshape(rows, RWKV_WIDTH) if has_vfirst else None
    vals = _prep_math(p, p_prev, v_first, consts)
    v_out[...] = vals[3].reshape(steps, nseq, RWKV_WIDTH)
    if steps < chunk:
        pad = jnp.zeros(((chunk - steps) * nseq, RWKV_WIDTH), F32)
        vals = [jnp.concatenate([x, pad], axis=0) for x in vals]
    o = _rec_phases(vals, s0_ref, post, s_out_ref, scratch, chunk=chunk, nseq=nseq)
    o_ref[...] = o[:rows].reshape(steps, nseq, RWKV_WIDTH)


def _rwkv_group(proj, shift0, v_first, state, lw, layer, steps, batch):
    nseq = SEQ_GROUP
    chunk = LONG_CHUNK if steps >= LONG_CHUNK else SHORT_CHUNK
    tile_steps = min(chunk, steps)
    has_vfirst = v_first is not None
    consts = _prep_consts(lw, layer, has_vfirst)
    post = _post_consts(lw, layer)
    rows3 = lambda a: a.reshape(steps, batch, a.shape[-1])
    row_spec = pl.BlockSpec((tile_steps, nseq, RWKV_WIDTH), lambda j, c: (c, j, 0))
    shift_spec = pl.BlockSpec((nseq, RWKV_COLS), lambda j, c: (j, 0))
    st_spec = pl.BlockSpec((nseq, RWKV_HEADS, RWKV_HEAD, RWKV_HEAD), lambda j, c: (j, 0, 0, 0))
    args = ([rows3(proj), shift0] + consts + ([rows3(v_first)] if has_vfirst else [])
            + [state] + post)
    in_specs = ([pl.BlockSpec((tile_steps, nseq, RWKV_COLS), lambda j, c: (c, j, 0)), shift_spec]
                + [_const_spec(a.shape) for a in consts] + ([row_spec] if has_vfirst else [])
                + [st_spec] + [_const_spec(a.shape) for a in post])
    rows_shape = jax.ShapeDtypeStruct((steps, batch, RWKV_WIDTH), F32)
    o, v, shift, s_new = pl.pallas_call(
        functools.partial(_rwkv_group_kernel, chunk=chunk, steps=tile_steps, nseq=nseq,
                          has_vfirst=has_vfirst, n_consts=len(consts)),
        grid=(batch // nseq, steps // tile_steps),
        in_specs=in_specs,
        out_specs=[row_spec, row_spec, shift_spec, st_spec],
        out_shape=[rows_shape, rows_shape, jax.ShapeDtypeStruct((batch, RWKV_COLS), F32),
                   jax.ShapeDtypeStruct(state.shape, F32)],
        scratch_shapes=[pltpu.VMEM((nseq, RWKV_COLS), F32)] + _rec_scratch(chunk, nseq),
        compiler_params=_cparams("parallel", "arbitrary"),
        name="rwkv_group",
    )(*args)
    return o.reshape(steps * batch, RWKV_WIDTH), v.reshape(steps * batch, RWKV_WIDTH), shift, s_new


def _s5_kernel(u0_ref, u1_ref, h0r_ref, h0i_ref, are_ref, aim_ref, ldt_ref, bre_ref, bim_ref,
               cre_ref, cim_ref, d_ref, wglu_ref, bglu_ref,
               o_ref, hr_out, hi_out, hr_c, hi_c, hre, him, *, batch):
    @pl.when(pl.program_id(0) == 0)
    def _():
        hr_c[...] = h0r_ref[...]
        hi_c[...] = h0i_ref[...]

    a_re = are_ref[...]
    a_im = aim_ref[...]
    dt = jnp.exp(ldt_ref[...])
    mag = jnp.exp(a_re * dt)
    ab_re = mag * jnp.cos(a_im * dt)
    ab_im = mag * jnp.sin(a_im * dt)
    den = a_re * a_re + a_im * a_im
    nr = ab_re - 1.0
    cf_re = (nr * a_re + ab_im * a_im) / den
    cf_im = (ab_im * a_re - nr * a_im) / den

    us = (u0_ref[...], u1_ref[...])
    tm = us[0].shape[0]
    for hf in range(2):
        ls = slice(hf * S5_HALF_L, (hf + 1) * S5_HALF_L)
        ub = us[hf].astype(BF16)
        pr = jnp.dot(ub, bre_ref[hf], preferred_element_type=F32)
        pi = jnp.dot(ub, bim_ref[hf], preferred_element_type=F32)
        hre[:, ls] = cf_re[:, ls] * pr - cf_im[:, ls] * pi
        him[:, ls] = cf_re[:, ls] * pi + cf_im[:, ls] * pr

    n_steps = tm // batch
    if n_steps <= 8:
        hr = hr_c[...]
        hi = hi_c[...]
        for s in range(n_steps):
            rows = slice(s * batch, (s + 1) * batch)
            nhr = ab_re * hr - ab_im * hi + hre[rows, :]
            nhi = ab_re * hi + ab_im * hr + him[rows, :]
            hre[rows, :] = nhr
            him[rows, :] = nhi
            hr, hi = nhr, nhi
        hr_c[...] = hr
        hi_c[...] = hi
    else:
        lane_w = 512
        for lc in range(S5_LANES // lane_w):
            ls = slice(lc * lane_w, (lc + 1) * lane_w)
            abr = jnp.broadcast_to(ab_re[:, ls], (batch, lane_w))
            abi = jnp.broadcast_to(ab_im[:, ls], (batch, lane_w))

            def body(s, carry, ls=ls, abr=abr, abi=abi):
                hr, hi = carry
                rows = pl.ds(pl.multiple_of(s * batch, batch), batch)
                nhr = abr * hr - abi * hi + hre[rows, ls]
                nhi = abr * hi + abi * hr + him[rows, ls]
                hre[rows, ls] = nhr
                him[rows, ls] = nhi
                return nhr, nhi

            hr, hi = lax.fori_loop(0, n_steps, body, (hr_c[:, ls], hi_c[:, ls]), unroll=8)
            hr_c[:, ls] = hr
            hi_c[:, ls] = hi

    hr_out[...] = hr_c[...]
    hi_out[...] = hi_c[...]

    for hf in range(2):
        ls = slice(hf * S5_HALF_L, (hf + 1) * S5_HALF_L)
        cs = slice(hf * S5_HALF_W, (hf + 1) * S5_HALF_W)
        y = (jnp.dot(hre[:, ls].astype(BF16), cre_ref[hf], preferred_element_type=F32)
             - jnp.dot(him[:, ls].astype(BF16), cim_ref[hf], preferred_element_type=F32)
             + d_ref[:, cs] * us[hf])
        o_ref[:, cs] = jax.nn.gelu(y, approximate=True)
    y = o_ref[...]
    o_ref[...] = y * jax.nn.sigmoid(_bdot(y, wglu_ref[...]) + bglu_ref[...])


def _s5(proj, h0r, h0i, lw, layer, batch):
    rows = proj.shape[0]
    tm = min(ROW_TILE, rows)
    u_blk = RWKV_COLS // S5_HALF_W
    consts = [h0r, h0i, lw["s5_a_re"][layer], lw["s5_a_im"][layer], lw["s5_log_dt"][layer],
              lw["s5_b_re"][layer], lw["s5_b_im"][layer], lw["s5_c_re"][layer], lw["s5_c_im"][layer],
              lw["s5_d"][layer], lw["s5_w_glu"][layer], lw["s5_b_glu"][layer]]
    st_shape = jax.ShapeDtypeStruct((batch, S5_LANES), F32)
    st_spec = pl.BlockSpec((batch, S5_LANES), lambda i: (0, 0))
    return pl.pallas_call(
        functools.partial(_s5_kernel, batch=batch),
        grid=(rows // tm,),
        in_specs=[pl.BlockSpec((tm, S5_HALF_W), lambda i: (i, u_blk)),
                  pl.BlockSpec((tm, S5_HALF_W), lambda i: (i, u_blk + 1))]
                 + [_const_spec(a.shape) for a in consts],
        out_specs=[pl.BlockSpec((tm, S5_WIDTH), lambda i: (i, 0)), st_spec, st_spec],
        out_shape=[jax.ShapeDtypeStruct((rows, S5_WIDTH), F32), st_shape, st_shape],
        scratch_shapes=[pltpu.VMEM((batch, S5_LANES), F32), pltpu.VMEM((batch, S5_LANES), F32),
                        pltpu.VMEM((tm, S5_LANES), F32), pltpu.VMEM((tm, S5_LANES), F32)],
        compiler_params=_cparams("arbitrary"),
        name="s5",
    )(proj, proj, *consts)


def _softmax_rows(s):
    e = jnp.exp(s - jnp.max(s, axis=-1, keepdims=True))
    return e / jnp.sum(e, axis=-1, keepdims=True)


CACHE_ROWS = N_MEM * X_HEADS * (X_HEAD_DIM // LANES)
CACHE_GROUP = X_HEADS * (X_HEAD_DIM // LANES)
Q_ROWS = 8


def _cache_view(cache):
    l, b = cache.shape[:2]
    halves = X_HEAD_DIM // LANES
    c = cache.reshape(l, b, N_MEM, X_HEADS, halves, LANES)
    return jnp.swapaxes(c, 3, 4).reshape(l, b, CACHE_ROWS, LANES)


def _attn_cache_kernel(q_ref, k_ref, v_ref, o_ref, q_scr, o_scr, *, steps, nseq):
    halves = X_HEAD_DIM // LANES
    scale = X_HEAD_DIM ** -0.5
    n_tiles = D_MODEL // LANES
    rows = steps * nseq
    q = q_ref[...].reshape(rows, D_MODEL)
    pad = jnp.zeros(((Q_ROWS - steps) * nseq, D_MODEL), F32)
    _stage(q_scr, jnp.concatenate([q, pad], axis=0))
    col = lax.broadcasted_iota(jnp.int32, (Q_ROWS, CACHE_ROWS), 1) % CACHE_GROUP
    for b in range(nseq):
        seq = _seq_rows(b, Q_ROWS, nseq)
        qx = jnp.concatenate([q_scr[n, seq, :] for n in range(n_tiles)], axis=0)
        e = _bdot_nt(qx, k_ref[b])
        probs = []
        for h in range(X_HEADS):
            base = h * halves * Q_ROWS
            valid = col == h
            s = jnp.where(valid, e[base:base + Q_ROWS], 0.0)
            for j in range(1, halves):
                part = jnp.where(col == j * X_HEADS + h,
                                 e[base + j * Q_ROWS:base + (j + 1) * Q_ROWS], 0.0)
                s = s + pltpu.roll(part, shift=CACHE_ROWS - j * X_HEADS, axis=1)
            pr = _softmax_rows(jnp.where(valid, s * scale, -1e30))
            probs.append(pr)
            for j in range(1, halves):
                probs.append(pltpu.roll(pr, shift=j * X_HEADS, axis=1))
        ox = _bdot(jnp.concatenate(probs, axis=0), v_ref[b])
        for n in range(n_tiles):
            o_scr[n, seq, :] = ox[n * Q_ROWS:(n + 1) * Q_ROWS]
    o = jnp.concatenate([o_scr[n, :rows, :] for n in range(n_tiles)], axis=1)
    o_ref[...] = o.reshape(steps, nseq, D_MODEL)


def _attn_cache(q, cache_k, cache_v, layer, steps, batch):
    nseq = SEQ_GROUP
    q_spec = pl.BlockSpec((steps, nseq, D_MODEL), lambda j: (0, j, 0))
    m_spec = pl.BlockSpec((None, nseq, CACHE_ROWS, LANES), lambda j: (layer, j, 0, 0))
    stage = pltpu.VMEM((D_MODEL // LANES, Q_ROWS * nseq, LANES), F32)
    out = pl.pallas_call(
        functools.partial(_attn_cache_kernel, steps=steps, nseq=nseq),
        grid=(batch // nseq,),
        in_specs=[q_spec, m_spec, m_spec],
        out_specs=q_spec,
        out_shape=jax.ShapeDtypeStruct((steps, batch, D_MODEL), F32),
        scratch_shapes=[stage, stage],
        compiler_params=_cparams("parallel"),
        name="mem_attn",
    )(q.reshape(steps, batch, D_MODEL), cache_k, cache_v)
    return out.reshape(steps * batch, D_MODEL)


def _cross_long_kernel(x_ref, gain_ref, wq_ref, k_ref, v_ref, wo_ref, o_ref,
                       q_scr, att_scr, s_scr, p_scr, *, batch):
    x = x_ref[...]
    steps = x.shape[0] // batch
    scale = X_HEAD_DIM ** -0.5
    tiles_per_head = X_HEAD_DIM // LANES
    n_tiles = D_MODEL // LANES
    blocks = [(b, h) for b in range(batch) for h in range(X_HEADS)]
    _stage(q_scr, jnp.dot(_rms(x, gain_ref[...]).astype(BF16), wq_ref[...],
                          preferred_element_type=F32))
    for i, (b, h) in enumerate(blocks):
        rows = _seq_rows(b, steps, batch)
        tiles = range(h * tiles_per_head, (h + 1) * tiles_per_head)
        q = jnp.concatenate([q_scr[j, rows, :] for j in tiles], axis=1)
        s_scr[i] = _bdot_nt(q, k_ref[b, :, h * X_HEAD_DIM:(h + 1) * X_HEAD_DIM]) * scale
    p_scr[...] = _softmax_rows(s_scr[...]).astype(BF16)
    for i, (b, h) in enumerate(blocks):
        rows = _seq_rows(b, steps, batch)
        o = jnp.dot(p_scr[i], v_ref[b, :, h * X_HEAD_DIM:(h + 1) * X_HEAD_DIM],
                    preferred_element_type=F32)
        for t in range(tiles_per_head):
            att_scr[h * tiles_per_head + t, rows, :] = o[:, t * LANES:(t + 1) * LANES]
    att = jnp.concatenate([att_scr[j] for j in range(n_tiles)], axis=1)
    o_ref[...] = x + jnp.dot(att.astype(BF16), wo_ref[...], preferred_element_type=F32)


def _cross_long(x, mem_k, mem_v, lw, layer, batch):
    rows = x.shape[0]
    tm = min(ATTN_ROW_TILE, rows)
    steps = tm // batch
    consts = [lw["norm_cross"][layer], lw["w_cq"][layer], mem_k, mem_v, lw["w_co"][layer]]
    return pl.pallas_call(
        functools.partial(_cross_long_kernel, batch=batch),
        grid=(rows // tm,),
        in_specs=[pl.BlockSpec((tm, D_MODEL), lambda i: (i, 0))] + [_const_spec(a.shape) for a in consts],
        out_specs=pl.BlockSpec((tm, D_MODEL), lambda i: (i, 0)),
        out_shape=jax.ShapeDtypeStruct((rows, D_MODEL), F32),
        scratch_shapes=[pltpu.VMEM((D_MODEL // LANES, tm, LANES), F32),
                        pltpu.VMEM((D_MODEL // LANES, tm, LANES), F32),
                        pltpu.VMEM((batch * X_HEADS, steps, N_MEM), F32),
                        pltpu.VMEM((batch * X_HEADS, steps, N_MEM), BF16)],
        compiler_params=_cparams("parallel"),
        name="cross_attn",
    )(x, *consts)


def _ffn_kernel(x_ref, buf0_ref, gain_ref, wg_ref, wu_ref, cw_ref, cb_ref, wd_ref,
                o_ref, buf_out, carry, *, batch):
    @pl.when(pl.program_id(0) == 0)
    def _():
        carry[...] = buf0_ref[...]

    x = x_ref[...]
    tm = x.shape[0]
    h = _rms(x, gain_ref[...]).astype(BF16)
    gt = jnp.dot(h, wg_ref[...], preferred_element_type=F32)
    up = jnp.dot(h, wu_ref[...], preferred_element_type=F32)
    padded = jnp.concatenate([carry[...], gt], axis=0)
    conv = cb_ref[...]
    for i in range(CONV_W):
        conv = conv + cw_ref[i:i + 1, :] * padded[i * batch:i * batch + tm]
    new_carry = padded[tm:]
    carry[...] = new_carry
    buf_out[...] = new_carry
    act = jax.nn.silu(conv) * up
    o_ref[...] = x + jnp.dot(act.astype(BF16), wd_ref[...], preferred_element_type=F32)


def _ffn(x, buf0, lw, layer, batch):
    rows = x.shape[0]
    tm = min(FFN_ROW_TILE, rows)
    tm = max(tm, (CONV_W - 1) * batch)
    consts = [buf0, lw["norm_ffn"][layer], lw["w_gate"][layer], lw["w_up"][layer],
              lw["ffn_conv_w"][layer], lw["ffn_conv_b"][layer], lw["w_down"][layer]]
    nbuf = (CONV_W - 1) * batch
    return pl.pallas_call(
        functools.partial(_ffn_kernel, batch=batch),
        grid=(rows // tm,),
        in_specs=[pl.BlockSpec((tm, D_MODEL), lambda i: (i, 0))] + [_const_spec(a.shape) for a in consts],
        out_specs=[pl.BlockSpec((tm, D_MODEL), lambda i: (i, 0)),
                   pl.BlockSpec((nbuf, D_FF), lambda i: (0, 0))],
        out_shape=[jax.ShapeDtypeStruct((rows, D_MODEL), F32), jax.ShapeDtypeStruct((nbuf, D_FF), F32)],
        scratch_shapes=[pltpu.VMEM((nbuf, D_FF), F32)],
        compiler_params=_cparams("arbitrary"),
        name="conv_ffn",
    )(x, *consts)


def _s5_in_blockdiag(b):
    l = b.shape[0]
    b = b.reshape(l, 2, S5_GROUPS // 2, S5_STATE, S5_GROUP)
    eye = jnp.eye(S5_GROUPS // 2, dtype=b.dtype)
    m = jnp.einsum('lfgph,gk->lfghkp', b, eye)
    return m.reshape(l, 2, S5_HALF_W, S5_HALF_L).astype(BF16)


def _s5_out_blockdiag(c):
    l = c.shape[0]
    c = c.reshape(l, 2, S5_GROUPS // 2, S5_GROUP, S5_STATE)
    eye = jnp.eye(S5_GROUPS // 2, dtype=c.dtype)
    m = jnp.einsum('lfgnp,gk->lfgpkn', c, eye)
    return m.reshape(l, 2, S5_HALF_L, S5_HALF_W).astype(BF16)


def _prep_weights(p):
    l = DEPTH
    row = lambda a: a.reshape(a.shape[0], 1, -1)
    lw = {}
    for name in ("norm_mix", "shift_mu", "rwkv_w0", "rwkv_a0", "rwkv_v0", "rwkv_k_k", "rwkv_k_a",
                 "rwkv_r_k", "rwkv_lnx_w", "rwkv_lnx_b", "s5_d", "s5_b_glu", "norm_cross", "norm_ffn",
                 "ffn_conv_b", "s5_a_re", "s5_a_im"):
        lw[name] = row(p[name])
    lw["s5_log_dt"] = row(jnp.repeat(p["s5_log_dt"], S5_STATE, axis=-1))
    lw["norm_final"] = p["norm_final"].reshape(1, -1)
    lw["ffn_conv_w"] = p["ffn_conv_w"]
    for name in ("w_in", "rwkv_w_g2", "s5_w_glu", "w_cq", "w_ck", "w_cv", "w_co", "w_gate", "w_up",
                 "w_down"):
        lw[name] = p[name].astype(BF16)
    w_out = p["w_out"].astype(BF16)
    lw["w_out_rw"] = w_out[:, :RWKV_WIDTH]
    lw["w_out_s5"] = w_out[:, RWKV_WIDTH:]
    z64 = jnp.zeros((l, LORA_PAD - 64, RWKV_WIDTH), F32)
    lw["rwkv_w_w2p"] = jnp.concatenate([p["rwkv_w_w2"], z64], axis=1).astype(BF16)
    lw["rwkv_w_a2p"] = jnp.concatenate([z64, p["rwkv_w_a2"]], axis=1).astype(BF16)
    v_lora = p["rwkv_w_v1"].shape[-1]
    lw["rwkv_w_v1p"] = jnp.pad(p["rwkv_w_v1"], ((0, 0), (0, 0), (0, LORA_PAD - v_lora))).astype(BF16)
    lw["rwkv_w_v2p"] = jnp.pad(p["rwkv_w_v2"], ((0, 0), (0, LORA_PAD - v_lora), (0, 0))).astype(BF16)
    head = jnp.arange(PAIR_W) // RWKV_HEAD
    lw["pair_ones"] = (head[:, None] == head[None, :]).astype(BF16)
    lw["s5_b_re"] = _s5_in_blockdiag(p["s5_b_re"])
    lw["s5_b_im"] = _s5_in_blockdiag(p["s5_b_im"])
    lw["s5_c_re"] = _s5_out_blockdiag(p["s5_c_re"])
    lw["s5_c_im"] = _s5_out_blockdiag(p["s5_c_im"])
    return lw


def _run_trunk(x, mem_k, mem_v, st_rwkv, st_shift, st_re, st_im, st_conv, lw, steps, batch):
    long_seq = steps >= LONG_SEQ
    if long_seq:
        mem_k = mem_k.astype(BF16)
        mem_v = mem_v.astype(BF16)
    else:
        mem_k = _cache_view(mem_k)
        mem_v = _cache_view(mem_v)
    v_first = None
    new_rw, new_shift, new_re, new_im, new_conv = [], [], [], [], []
    for l in range(DEPTH):
        proj = _rowmm([x], [lw["w_in"][l]], gain=lw["norm_mix"][l], name="in_proj")
        o_rw, v_l, sh, s_rw = _rwkv_group(proj, st_shift[l], v_first, st_rwkv[l], lw, l, steps, batch)
        if l == 0:
            v_first = v_l
        o_s5, hr, hi = _s5(proj, st_re[l], st_im[l], lw, l, batch)
        x = _rowmm([o_rw, o_s5], [lw["w_out_rw"][l], lw["w_out_s5"][l]], resid=x, name="mix_out")
        if long_seq:
            x = _cross_long(x, mem_k[l], mem_v[l], lw, l, batch)
        else:
            q = _rowmm([x], [lw["w_cq"][l]], gain=lw["norm_cross"][l], name="cross_q")
            att = _attn_cache(q, mem_k, mem_v, l, steps, batch)
            x = _rowmm([att], [lw["w_co"][l]], resid=x, name="cross_o")
        x, cb = _ffn(x, st_conv[l], lw, l, batch)
        new_rw.append(s_rw)
        new_shift.append(sh)
        new_re.append(hr)
        new_im.append(hi)
        new_conv.append(cb)
    return (x, jnp.stack(new_rw), jnp.stack(new_shift), jnp.stack(new_re), jnp.stack(new_im),
            jnp.stack(new_conv))


def _group(x, mem_k, mem_v, st_rwkv, st_shift, st_re, st_im, st_conv, lw):
    b, t, _ = x.shape
    long_seq = t >= LONG_SEQ
    if long_seq:
        xt = _to_time_major(x)
    else:
        xt = jnp.swapaxes(x, 0, 1).reshape(t * b, D_MODEL)
    conv_t = jnp.swapaxes(st_conv, 1, 2).reshape(DEPTH, (CONV_W - 1) * b, D_FF)
    y, rw, sh, re, im, cv = _run_trunk(
        xt, mem_k, mem_v, st_rwkv, st_shift, st_re.reshape(DEPTH, b, S5_LANES),
        st_im.reshape(DEPTH, b, S5_LANES), conv_t, lw, t, b)
    if long_seq:
        y = _final_norm_long(y, lw["norm_final"], b)
    else:
        y = jnp.swapaxes(_rownorm(y, lw["norm_final"]).reshape(t, b, D_MODEL), 0, 1)
    cv = jnp.swapaxes(cv.reshape(DEPTH, CONV_W - 1, b, D_FF), 1, 2)
    return (y, rw, sh, re.reshape(DEPTH, b, S5_GROUPS, S5_STATE),
            im.reshape(DEPTH, b, S5_GROUPS, S5_STATE), cv)


def kernel(x_prompt, x_sample, mem_prompt, state_rwkv, state_shift, state_s5_re, state_s5_im, state_ffn_conv, cache_mem_k, cache_mem_v, norm_mix, w_in, shift_mu, rwkv_w0, rwkv_w_w2, rwkv_a0, rwkv_w_a2, rwkv_v0, rwkv_w_v1, rwkv_w_v2, rwkv_w_g2, rwkv_k_k, rwkv_k_a, rwkv_r_k, rwkv_lnx_w, rwkv_lnx_b, s5_a_re, s5_a_im, s5_log_dt, s5_b_re, s5_b_im, s5_c_re, s5_c_im, s5_d, s5_w_glu, s5_b_glu, w_out, norm_cross, w_cq, w_ck, w_cv, w_co, norm_ffn, w_gate, w_up, ffn_conv_w, ffn_conv_b, w_down, norm_final):
    lw = _prep_weights(dict(
        norm_mix=norm_mix, w_in=w_in, shift_mu=shift_mu, rwkv_w0=rwkv_w0, rwkv_w_w2=rwkv_w_w2,
        rwkv_a0=rwkv_a0, rwkv_w_a2=rwkv_w_a2, rwkv_v0=rwkv_v0, rwkv_w_v1=rwkv_w_v1,
        rwkv_w_v2=rwkv_w_v2, rwkv_w_g2=rwkv_w_g2, rwkv_k_k=rwkv_k_k, rwkv_k_a=rwkv_k_a,
        rwkv_r_k=rwkv_r_k, rwkv_lnx_w=rwkv_lnx_w, rwkv_lnx_b=rwkv_lnx_b, s5_a_re=s5_a_re,
        s5_a_im=s5_a_im, s5_log_dt=s5_log_dt, s5_b_re=s5_b_re, s5_b_im=s5_b_im, s5_c_re=s5_c_re,
        s5_c_im=s5_c_im, s5_d=s5_d, s5_w_glu=s5_w_glu, s5_b_glu=s5_b_glu, w_out=w_out,
        norm_cross=norm_cross, w_cq=w_cq, w_ck=w_ck, w_cv=w_cv, w_co=w_co, norm_ffn=norm_ffn,
        w_gate=w_gate, w_up=w_up, ffn_conv_w=ffn_conv_w, ffn_conv_b=ffn_conv_b, w_down=w_down,
        norm_final=norm_final))
    bp = x_prompt.shape[0]
    mem_rows = mem_prompt.reshape(bp * N_MEM, D_MODEL)
    p_mem_k = jnp.stack([_rowmm([mem_rows], [lw["w_ck"][l]], name="mem_k") for l in range(DEPTH)])
    p_mem_v = jnp.stack([_rowmm([mem_rows], [lw["w_cv"][l]], name="mem_v") for l in range(DEPTH)])
    p_mem_k = p_mem_k.reshape(DEPTH, bp, N_MEM, D_MODEL)
    p_mem_v = p_mem_v.reshape(DEPTH, bp, N_MEM, D_MODEL)
    z_rw = jnp.zeros((DEPTH, bp, RWKV_HEADS, RWKV_HEAD, RWKV_HEAD), F32)
    z_shift = jnp.zeros((DEPTH, bp, RWKV_COLS), F32)
    z_s5 = jnp.zeros((DEPTH, bp, S5_GROUPS, S5_STATE), F32)
    z_conv = jnp.zeros((DEPTH, bp, CONV_W - 1, D_FF), F32)
    y_prompt, p_rwkv, p_shift, p_re, p_im, p_conv = _group(
        x_prompt, p_mem_k, p_mem_v, z_rw, z_shift, z_s5, z_s5, z_conv, lw)
    y_sample, s_rwkv, s_shift, s_re, s_im, s_conv = _group(
        x_sample, cache_mem_k, cache_mem_v, state_rwkv, state_shift, state_s5_re, state_s5_im,
        state_ffn_conv, lw)
    return (y_prompt, y_sample, p_rwkv, p_shift, p_re, p_im, p_conv,
            p_mem_k.reshape(DEPTH, bp, N_MEM, X_HEADS, X_HEAD_DIM),
            p_mem_v.reshape(DEPTH, bp, N_MEM, X_HEADS, X_HEAD_DIM),
            s_rwkv, s_shift, s_re, s_im, s_conv)
```

```python
import functools
import math

import jax
import jax.numpy as jnp
from jax import lax
from jax.experimental import pallas as pl
from jax.experimental.pallas import tpu as pltpu

F32 = jnp.float32
BF16 = jnp.bfloat16

D_MODEL = 1024
DEPTH = 4
RWKV_WIDTH = 512
RWKV_HEAD = 64
RWKV_HEADS = 8
HEAD_PAIRS = RWKV_HEADS // 2
LANES = 128
PAIR_W = 2 * RWKV_HEAD
LORA_PAD = 128
RWKV_COLS = 3 * RWKV_WIDTH + 64 + 64 + 128
S5_WIDTH = 512
S5_GROUP = 16
S5_GROUPS = 32
S5_STATE = 64
S5_LANES = S5_GROUPS * S5_STATE
S5_HALF_W = S5_WIDTH // 2
S5_HALF_L = S5_LANES // 2
IN_COLS = RWKV_COLS + S5_WIDTH
N_MEM = 256
X_HEADS = 4
X_HEAD_DIM = 256
D_FF = 2816
CONV_W = 3
RMS_EPS = 1e-6
LNX_EPS = 64e-5

ROW_TILE = 512
FFN_ROW_TILE = 512
ATTN_ROW_TILE = 1024
LONG_SEQ = 64
LONG_CHUNK = 64
SHORT_CHUNK = 8
SEQ_GROUP = 8
VMEM_LIMIT = 56 * 1024 * 1024


def _cparams(*sem):
    return pltpu.CompilerParams(dimension_semantics=sem, vmem_limit_bytes=VMEM_LIMIT)


def _const_spec(shape):
    nd = len(shape)
    return pl.BlockSpec(shape, lambda *_: (0,) * nd, pipeline_mode=pl.Buffered(1))


class _Layer:
    def __init__(self, stacked, layer):
        self.array = stacked
        self.layer = layer
        self.shape = stacked.shape[1:]


def _resident_spec(op):
    if isinstance(op, _Layer):
        nd = len(op.shape)
        layer = op.layer
        return pl.BlockSpec((None,) + tuple(op.shape), lambda *_: (layer,) + (0,) * nd,
                            pipeline_mode=pl.Buffered(1))
    return _const_spec(op.shape)


def _array(op):
    return op.array if isinstance(op, _Layer) else op


def _bdot(a, b):
    return jnp.dot(a.astype(BF16), b.astype(BF16), preferred_element_type=F32)


def _bdot_nt(a, b):
    return lax.dot_general(a.astype(BF16), b.astype(BF16), (((1,), (1,)), ((), ())),
                           preferred_element_type=F32)


def _bdot_tn(a, b):
    return lax.dot_general(a.astype(BF16), b.astype(BF16), (((0,), (0,)), ((), ())),
                           preferred_element_type=F32)


def _head_sums(x, pair_ones):
    outs = []
    for sl in _lane_tiles(x.shape[1]):
        hi = x[:, sl].astype(BF16)
        lo = (x[:, sl] - hi.astype(F32)).astype(BF16)
        outs.append(jnp.dot(hi, pair_ones, preferred_element_type=F32)
                    + jnp.dot(lo, pair_ones, preferred_element_type=F32))
    return jnp.concatenate(outs, axis=1)


def _rms(x, gain):
    return x * lax.rsqrt(jnp.mean(x * x, axis=-1, keepdims=True) + RMS_EPS) * gain


def _seq_rows(seq, steps, batch):
    return pl.ds(seq, steps, stride=batch)


def _lane_tiles(width):
    return [slice(j * LANES, (j + 1) * LANES) for j in range(width // LANES)]


def _stage(scr, x):
    for j, sl in enumerate(_lane_tiles(x.shape[1])):
        scr[j] = x[:, sl]


def _to_time_major_kernel(x_ref, o_ref, scr, *, batch, steps):
    tiles = _lane_tiles(x_ref.shape[2])
    for b in range(batch):
        for j, sl in enumerate(tiles):
            scr[j, _seq_rows(b, steps, batch), :] = x_ref[b, :, sl]
    for j, sl in enumerate(tiles):
        o_ref[:, sl] = scr[j]


def _to_time_major(x):
    b, t, d = x.shape
    tt = ROW_TILE // b
    return pl.pallas_call(
        functools.partial(_to_time_major_kernel, batch=b, steps=tt),
        grid=(t // tt,),
        in_specs=[pl.BlockSpec((b, tt, d), lambda i: (0, i, 0))],
        out_specs=pl.BlockSpec((tt * b, d), lambda i: (i, 0)),
        out_shape=jax.ShapeDtypeStruct((t * b, d), F32),
        scratch_shapes=[pltpu.VMEM((d // LANES, tt * b, LANES), F32)],
        compiler_params=_cparams("parallel"),
        name="to_time_major",
    )(x)


def _final_norm_long_kernel(x_ref, g_ref, o_ref, scr, *, batch, steps):
    _stage(scr, _rms(x_ref[...], g_ref[...]))
    for b in range(batch):
        for j, sl in enumerate(_lane_tiles(x_ref.shape[1])):
            o_ref[b, :, sl] = scr[j, _seq_rows(b, steps, batch), :]


def _final_norm_long(x, gain, batch):
    rows, d = x.shape
    t = rows // batch
    tt = ROW_TILE // batch
    return pl.pallas_call(
        functools.partial(_final_norm_long_kernel, batch=batch, steps=tt),
        grid=(t // tt,),
        in_specs=[pl.BlockSpec((tt * batch, d), lambda i: (i, 0)), _const_spec(gain.shape)],
        out_specs=pl.BlockSpec((batch, tt, d), lambda i: (0, i, 0)),
        out_shape=jax.ShapeDtypeStruct((batch, t, d), F32),
        scratch_shapes=[pltpu.VMEM((d // LANES, tt * batch, LANES), F32)],
        compiler_params=_cparams("parallel"),
        name="final_norm",
    )(x, gain)


def _rowmm_kernel(*refs, n_x, has_gain, has_resid):
    xs = refs[:n_x]
    ws = refs[n_x:2 * n_x]
    pos = 2 * n_x
    gain = refs[pos] if has_gain else None
    pos += int(has_gain)
    resid = refs[pos] if has_resid else None
    pos += int(has_resid)
    o_ref = refs[pos]
    acc = None
    for x_ref, w_ref in zip(xs, ws):
        x = x_ref[...]
        if has_gain:
            x = _rms(x, gain[...])
        d = jnp.dot(x.astype(BF16), w_ref[...], preferred_element_type=F32)
        acc = d if acc is None else acc + d
    if has_resid:
        acc = acc + resid[...]
    o_ref[...] = acc


def _rowmm(xs, ws, gain=None, resid=None, name="rowmm"):
    rows = xs[0].shape[0]
    n_out = ws[0].shape[1]
    tm = min(ROW_TILE, rows)
    in_specs = [pl.BlockSpec((tm, x.shape[1]), lambda i: (i, 0)) for x in xs]
    in_specs += [_resident_spec(w) for w in ws]
    args = list(xs) + [_array(w) for w in ws]
    if gain is not None:
        in_specs.append(_resident_spec(gain))
        args.append(_array(gain))
    if resid is not None:
        in_specs.append(pl.BlockSpec((tm, n_out), lambda i: (i, 0)))
        args.append(resid)
    return pl.pallas_call(
        functools.partial(_rowmm_kernel, n_x=len(xs), has_gain=gain is not None,
                          has_resid=resid is not None),
        grid=(rows // tm,),
        in_specs=in_specs,
        out_specs=pl.BlockSpec((tm, n_out), lambda i: (i, 0)),
        out_shape=jax.ShapeDtypeStruct((rows, n_out), F32),
        compiler_params=_cparams("parallel"),
        name=name,
    )(*args)


def _rownorm_kernel(x_ref, g_ref, o_ref):
    o_ref[...] = _rms(x_ref[...], g_ref[...])


def _rownorm(x, gain):
    rows, d = x.shape
    tm = min(ROW_TILE, rows)
    return pl.pallas_call(
        _rownorm_kernel,
        grid=(rows // tm,),
        in_specs=[pl.BlockSpec((tm, d), lambda i: (i, 0)), _const_spec(gain.shape)],
        out_specs=pl.BlockSpec((tm, d), lambda i: (i, 0)),
        out_shape=jax.ShapeDtypeStruct((rows, d), F32),
        compiler_params=_cparams("parallel"),
        name="final_norm",
    )(x, gain)


def _softplus(z):
    return jnp.maximum(z, 0.0) + jnp.log1p(jnp.exp(-jnp.abs(z)))


def _prep_consts(lw, layer, has_vfirst):
    names = ["shift_mu", "rwkv_w0", "rwkv_w_w2p", "rwkv_a0", "rwkv_w_a2p", "rwkv_w_g2", "rwkv_k_k",
             "rwkv_k_a"]
    consts = [_Layer(lw[n], layer) for n in names] + [lw["pair_ones"]]
    if has_vfirst:
        consts += [_Layer(lw[n], layer - 1) for n in ("rwkv_v0", "rwkv_w_v1p", "rwkv_w_v2p")]
    return consts


def _token_shift(p, carry, batch):
    tm = p.shape[0]
    prev = jnp.concatenate([carry, p[:tm - batch]], axis=0) if tm > batch else carry
    return prev, p[tm - batch:]


def _prep_math(p, p_prev, v_first, consts):
    mu, w0, ww2, a0, wa2, wg2, k_k, k_a, pair_ones = [c[...] for c in consts[:9]]
    q = p + (p_prev - p) * mu
    rw = RWKV_WIDTH
    r = q[:, 0:rw]
    k = q[:, rw:2 * rw]
    v = q[:, 2 * rw:3 * rw]
    x_wa = q[:, 3 * rw:3 * rw + LORA_PAD]
    x_g = q[:, 3 * rw + LORA_PAD:3 * rw + 2 * LORA_PAD]
    w = -_softplus(-(w0 + _bdot(jnp.tanh(x_wa), ww2))) - 0.5
    log_decay = -jnp.exp(w)
    a = jax.nn.sigmoid(a0 + _bdot(x_wa, wa2))
    g = _bdot(jax.nn.sigmoid(x_g), wg2)
    if v_first is not None:
        v0, wv1, wv2 = [c[...] for c in consts[9:12]]
        mix = jax.nn.sigmoid(v0 + _bdot(_bdot(v, wv1), wv2))
        v = v + (v_first - v) * mix
    kk = k * k_k
    kk = kk * lax.rsqrt(jnp.maximum(_head_sums(kk * kk, pair_ones), 1e-24))
    k = k * (1.0 + (a - 1.0) * k_a)
    return r, log_decay, k, v, kk, a, g


def _rec_scratch(chunk, nseq):
    c2 = 2 * chunk
    per = (nseq, HEAD_PAIRS)
    return [
        pltpu.VMEM(per + (PAIR_W, PAIR_W), F32),
        pltpu.VMEM(per + (2 * c2, PAIR_W), BF16),
        pltpu.VMEM(per + (2 * c2, PAIR_W), BF16),
        pltpu.VMEM(per + (c2, PAIR_W), BF16),
        pltpu.VMEM(per + (2 * c2, PAIR_W), BF16),
        pltpu.VMEM(per + (c2, c2), BF16),
        pltpu.VMEM(per + (c2, c2), BF16),
        pltpu.VMEM(per + (c2, c2), BF16),
        pltpu.VMEM(per + (c2, c2), BF16),
        pltpu.VMEM(per + (c2, PAIR_W), BF16),
        pltpu.VMEM(per + (c2, PAIR_W), F32),
        pltpu.VMEM(per + (c2, PAIR_W), F32),
        pltpu.VMEM((HEAD_PAIRS, chunk * nseq, PAIR_W), F32),
        pltpu.VMEM((nseq, 1, RWKV_WIDTH), F32),
        pltpu.VMEM((7, HEAD_PAIRS, chunk * nseq, PAIR_W), F32),
    ]


def _rec_phases(vals, s0_ref, post_consts, s_out_ref, scratch, *, chunk, nseq):
    (s_scr, ar_scr, bk_scr, v_scr, tl_scr, p_scr, lak_scr, mrb_scr, mrk_scr, rb_scr, r32_scr,
     aro_scr, o_scr, gend_scr, stage_scr) = scratch
    r, lw, k, v, kk, a, g = vals
    lnw_ref, lnb_ref, rk_ref, ones_ref = post_consts
    c = pl.program_id(1)
    cc = chunk
    c2 = 2 * cc
    fused = c2 % 128 == 0
    problems = [(b, p) for b in range(nseq) for p in range(HEAD_PAIRS)]

    @pl.when(c == 0)
    def _():
        zero = jnp.zeros((RWKV_HEAD, RWKV_HEAD), F32)
        for b, p in problems:
            top = jnp.concatenate([s0_ref[b, 2 * p], zero], axis=1)
            bottom = jnp.concatenate([zero, s0_ref[b, 2 * p + 1]], axis=1)
            s_scr[b, p] = jnp.concatenate([top, bottom], axis=0)

    lane = lax.broadcasted_iota(jnp.int32, (1, PAIR_W), 1)
    first = lane < RWKV_HEAD
    row2 = lax.broadcasted_iota(jnp.int32, (c2, c2), 0)
    col2 = lax.broadcasted_iota(jnp.int32, (c2, c2), 1)
    same = (row2 >= cc) == (col2 >= cc)
    rr = jnp.where(row2 >= cc, row2 - cc, row2)
    cl = jnp.where(col2 >= cc, col2 - cc, col2)
    strict = same & (cl < rr)
    incl = same & (cl <= rr)
    n_factors = max(1, math.ceil(math.log2(cc)))

    def stack(xs):
        return jnp.concatenate([jnp.where(first, xs, 0.0), jnp.where(first, 0.0, xs)],
                               axis=0).astype(BF16)

    cum = lw
    shift = nseq
    while shift < cc * nseq:
        cum = cum + jnp.concatenate([jnp.zeros((shift, RWKV_WIDTH), F32), cum[:-shift]], axis=0)
        shift *= 2
    cum_end = cum[(cc - 1) * nseq:]
    for b in range(nseq):
        gend_scr[b] = jnp.exp(cum_end[b:b + 1])
    cum_end = jnp.broadcast_to(cum_end[None], (cc, nseq, RWKV_WIDTH)).reshape(cc * nseq, RWKV_WIDTH)
    g_inv = jnp.exp(-cum)
    g_tail = jnp.exp(cum_end - cum)
    kka = kk * a
    operands = [-kk * jnp.exp(cum - lw), r * jnp.exp(cum),
                kka * g_inv, k * g_inv,
                kka * g_tail, k * g_tail, v]
    for i, x in enumerate(operands):
        _stage(stage_scr.at[i], x)

    def operand(i, b, p):
        return stage_scr[i, p, _seq_rows(b, cc, nseq), :]

    for b, p in problems:
        ar_scr[b, p, :c2] = stack(operand(0, b, p))
        ar_scr[b, p, c2:] = stack(operand(1, b, p))
        bk_scr[b, p, :c2] = stack(operand(2, b, p))
        bk_scr[b, p, c2:] = stack(operand(3, b, p))
        tl_scr[b, p, :c2] = stack(operand(4, b, p))
        tl_scr[b, p, c2:] = stack(operand(5, b, p))
        v_scr[b, p] = stack(operand(6, b, p))

    for b, p in problems:
        ar = ar_scr[b, p]
        bk = bk_scr[b, p]
        if fused:
            gram = _bdot_nt(ar, bk)
            g_ab, g_ak = gram[:c2, :c2], gram[:c2, c2:]
            g_rb, g_rk = gram[c2:, :c2], gram[c2:, c2:]
        else:
            g_ab, g_ak = _bdot_nt(ar[:c2], bk[:c2]), _bdot_nt(ar[:c2], bk[c2:])
            g_rb, g_rk = _bdot_nt(ar[c2:], bk[:c2]), _bdot_nt(ar[c2:], bk[c2:])
        p_scr[b, p] = jnp.where(strict, g_ab, 0.0).astype(BF16)
        lak_scr[b, p] = jnp.where(strict, g_ak, 0.0).astype(BF16)
        mrb_scr[b, p] = jnp.where(incl, g_rb, 0.0).astype(BF16)
        mrk_scr[b, p] = jnp.where(incl, g_rk, 0.0).astype(BF16)
        ar_state = _bdot_nt(ar, s_scr[b, p])
        r32_scr[b, p] = ar_state[:c2]
        aro_scr[b, p] = ar_state[c2:]

    for b, p in problems:
        rhs = r32_scr[b, p] + jnp.dot(lak_scr[b, p], v_scr[b, p], preferred_element_type=F32)
        r32_scr[b, p] = rhs
        rb_scr[b, p] = rhs.astype(BF16)

    for m in range(n_factors):
        last = m == n_factors - 1
        for b, p in problems:
            pw = p_scr[b, p]
            rb = rb_scr[b, p]
            if last:
                delta = jnp.dot(pw, rb, preferred_element_type=F32)
            elif fused:
                both = jnp.dot(pw, jnp.concatenate([pw, rb], axis=1), preferred_element_type=F32)
                p_scr[b, p] = both[:, :c2].astype(BF16)
                delta = both[:, c2:]
            else:
                p_scr[b, p] = jnp.dot(pw, pw, preferred_element_type=F32).astype(BF16)
                delta = jnp.dot(pw, rb, preferred_element_type=F32)
            rhs = r32_scr[b, p] + delta
            r32_scr[b, p] = rhs
            rb_scr[b, p] = rhs.astype(BF16)

    for b, p in problems:
        sl = slice(p * PAIR_W, (p + 1) * PAIR_W)
        u_s = rb_scr[b, p]
        v_s = v_scr[b, p]
        tl = tl_scr[b, p]
        if fused:
            uv = jnp.concatenate([u_s, v_s], axis=0)
            mm = jnp.concatenate([mrb_scr[b, p], mrk_scr[b, p]], axis=1)
            o_st = aro_scr[b, p] + jnp.dot(mm, uv, preferred_element_type=F32)
            upd = _bdot_tn(uv, tl)
        else:
            o_st = (aro_scr[b, p] + jnp.dot(mrb_scr[b, p], u_s, preferred_element_type=F32)
                    + jnp.dot(mrk_scr[b, p], v_s, preferred_element_type=F32))
            upd = _bdot_tn(u_s, tl[:c2]) + _bdot_tn(v_s, tl[c2:])
        o_scr[p, _seq_rows(b, cc, nseq), :] = o_st[:cc] + o_st[cc:]
        s_scr[b, p] = s_scr[b, p] * gend_scr[b][:, sl] + upd

    @pl.when(c == pl.num_programs(1) - 1)
    def _():
        for b, p in problems:
            s_pair = s_scr[b, p]
            s_out_ref[b, 2 * p] = s_pair[:RWKV_HEAD, :RWKV_HEAD]
            s_out_ref[b, 2 * p + 1] = s_pair[RWKV_HEAD:, RWKV_HEAD:]

    ones = ones_ref[...]
    inv_n = 1.0 / RWKV_HEAD
    o = jnp.concatenate([o_scr[j] for j in range(HEAD_PAIRS)], axis=1)
    mean = _head_sums(o, ones) * inv_n
    d = o - mean
    var = _head_sums(d * d, ones) * inv_n
    on = d * lax.rsqrt(var + LNX_EPS) * lnw_ref[...] + lnb_ref[...]
    bonus = _head_sums(r * k * rk_ref[...], ones) * v
    return (on + bonus) * g


def _post_consts(lw, layer):
    return [_Layer(lw[n], layer) for n in ("rwkv_lnx_w", "rwkv_lnx_b", "rwkv_r_k")] + [lw["pair_ones"]]


def _rwkv_group_kernel(*refs, chunk, steps, nseq, has_vfirst, n_consts):
    p_ref, shift0_ref = refs[:2]
    consts = refs[2:2 + n_consts]
    pos = 2 + n_consts
    vf_ref = refs[pos] if has_vfirst else None
    pos += int(has_vfirst)
    s0_ref = refs[pos]
    post = refs[pos + 1:pos + 5]
    pos += 5
    o_ref, v_out, shift_out, s_out_ref, carry = refs[pos:pos + 5]
    scratch = refs[pos + 5:]
    rows = steps * nseq

    @pl.when(pl.program_id(1) == 0)
    def _():
        carry[...] = shift0_ref[...]

    p = p_ref[...].reshape(rows, RWKV_COLS)
    p_prev, new_carry = _token_shift(p, carry[...], nseq)
    carry[...] = new_carry
    shift_out[...] = new_carry
    v_first = vf_ref[...].reshape(rows, RWKV_WIDTH) if has_vfirst else None
    vals = _prep_math(p, p_prev, v_first, consts)
    v_out[...] = vals[3].reshape(steps, nseq, RWKV_WIDTH)
    if steps < chunk:
        pad = jnp.zeros(((chunk - steps) * nseq, RWKV_WIDTH), F32)
        vals = [jnp.concatenate([x, pad], axis=0) for x in vals]
    o = _rec_phases(vals, s0_ref, post, s_out_ref, scratch, chunk=chunk, nseq=nseq)
    o_ref[...] = o[:rows].reshape(steps, nseq, RWKV_WIDTH)


def _rwkv_group(proj, shift0, v_first, state, lw, layer, steps, batch):
    nseq = SEQ_GROUP
    chunk = LONG_CHUNK if steps >= LONG_CHUNK else SHORT_CHUNK
    tile_steps = min(chunk, steps)
    has_vfirst = v_first is not None
    consts = _prep_consts(lw, layer, has_vfirst)
    post = _post_consts(lw, layer)
    rows3 = lambda a: a.reshape(steps, batch, a.shape[-1])
    row_spec = pl.BlockSpec((tile_steps, nseq, RWKV_WIDTH), lambda j, c: (c, j, 0))
    shift_spec = pl.BlockSpec((nseq, RWKV_COLS), lambda j, c: (j, 0))
    st_spec = pl.BlockSpec((nseq, RWKV_HEADS, RWKV_HEAD, RWKV_HEAD), lambda j, c: (j, 0, 0, 0))
    args = ([rows3(proj), shift0] + [_array(a) for a in consts]
            + ([rows3(v_first)] if has_vfirst else []) + [state] + [_array(a) for a in post])
    in_specs = ([pl.BlockSpec((tile_steps, nseq, RWKV_COLS), lambda j, c: (c, j, 0)),
                 pl.BlockSpec((None, nseq, RWKV_COLS), lambda j, c: (layer, j, 0))]
                + [_resident_spec(a) for a in consts] + ([row_spec] if has_vfirst else [])
                + [pl.BlockSpec((None, nseq, RWKV_HEADS, RWKV_HEAD, RWKV_HEAD),
                                lambda j, c: (layer, j, 0, 0, 0))]
                + [_resident_spec(a) for a in post])
    rows_shape = jax.ShapeDtypeStruct((steps, batch, RWKV_WIDTH), F32)
    o, v, shift, s_new = pl.pallas_call(
        functools.partial(_rwkv_group_kernel, chunk=chunk, steps=tile_steps, nseq=nseq,
                          has_vfirst=has_vfirst, n_consts=len(consts)),
        grid=(batch // nseq, steps // tile_steps),
        in_specs=in_specs,
        out_specs=[row_spec, row_spec, shift_spec, st_spec],
        out_shape=[rows_shape, rows_shape, jax.ShapeDtypeStruct((batch, RWKV_COLS), F32),
                   jax.ShapeDtypeStruct(state.shape[1:], F32)],
        scratch_shapes=[pltpu.VMEM((nseq, RWKV_COLS), F32)] + _rec_scratch(chunk, nseq),
        compiler_params=_cparams("parallel", "arbitrary"),
        name="rwkv_group",
    )(*args)
    return o.reshape(steps * batch, RWKV_WIDTH), v.reshape(steps * batch, RWKV_WIDTH), shift, s_new


def _s5_kernel(u0_ref, u1_ref, x_ref, orw_ref, h0r_ref, h0i_ref, are_ref, aim_ref, ldt_ref, bre_ref,
               bim_ref, cre_ref, cim_ref, d_ref, wglu_ref, bglu_ref, wrw_ref, ws5_ref,
               o_ref, hr_out, hi_out, hr_c, hi_c, hre, him, bbre, bbim, *, batch):
    a_re = are_ref[...]
    a_im = aim_ref[...]
    dt = jnp.exp(ldt_ref[...])
    mag = jnp.exp(a_re * dt)
    ab_re = mag * jnp.cos(a_im * dt)
    ab_im = mag * jnp.sin(a_im * dt)

    @pl.when(pl.program_id(0) == 0)
    def _():
        hr_c[...] = h0r_ref[...]
        hi_c[...] = h0i_ref[...]
        den = a_re * a_re + a_im * a_im
        nr = ab_re - 1.0
        cf_re = (nr * a_re + ab_im * a_im) / den
        cf_im = (ab_im * a_re - nr * a_im) / den
        for hf in range(2):
            ls = slice(hf * S5_HALF_L, (hf + 1) * S5_HALF_L)
            b_re = bre_ref[hf]
            b_im = bim_ref[hf]
            bbre[hf] = (cf_re[:, ls] * b_re - cf_im[:, ls] * b_im).astype(BF16)
            bbim[hf] = (cf_re[:, ls] * b_im + cf_im[:, ls] * b_re).astype(BF16)

    us = (u0_ref[...], u1_ref[...])
    tm = us[0].shape[0]
    for hf in range(2):
        ls = slice(hf * S5_HALF_L, (hf + 1) * S5_HALF_L)
        ub = us[hf].astype(BF16)
        hre[:, ls] = jnp.dot(ub, bbre[hf], preferred_element_type=F32)
        him[:, ls] = jnp.dot(ub, bbim[hf], preferred_element_type=F32)

    n_steps = tm // batch
    if n_steps <= 8:
        hr = hr_c[...]
        hi = hi_c[...]
        for s in range(n_steps):
            rows = slice(s * batch, (s + 1) * batch)
            nhr = ab_re * hr - ab_im * hi + hre[rows, :]
            nhi = ab_re * hi + ab_im * hr + him[rows, :]
            hre[rows, :] = nhr
            him[rows, :] = nhi
            hr, hi = nhr, nhi
        hr_c[...] = hr
        hi_c[...] = hi
    else:
        lane_w = 512
        for lc in range(S5_LANES // lane_w):
            ls = slice(lc * lane_w, (lc + 1) * lane_w)
            abr = jnp.broadcast_to(ab_re[:, ls], (batch, lane_w))
            abi = jnp.broadcast_to(ab_im[:, ls], (batch, lane_w))

            def body(s, carry, ls=ls, abr=abr, abi=abi):
                hr, hi = carry
                rows = pl.ds(pl.multiple_of(s * batch, batch), batch)
                nhr = abr * hr - abi * hi + hre[rows, ls]
                nhi = abr * hi + abi * hr + him[rows, ls]
                hre[rows, ls] = nhr
                him[rows, ls] = nhi
                return nhr, nhi

            hr, hi = lax.fori_loop(0, n_steps, body, (hr_c[:, ls], hi_c[:, ls]), unroll=8)
            hr_c[:, ls] = hr
            hi_c[:, ls] = hi

    hr_out[...] = hr_c[...]
    hi_out[...] = hi_c[...]

    ys = []
    for hf in range(2):
        ls = slice(hf * S5_HALF_L, (hf + 1) * S5_HALF_L)
        cs = slice(hf * S5_HALF_W, (hf + 1) * S5_HALF_W)
        y = (jnp.dot(hre[:, ls].astype(BF16), cre_ref[hf], preferred_element_type=F32)
             - jnp.dot(him[:, ls].astype(BF16), cim_ref[hf], preferred_element_type=F32)
             + d_ref[:, cs] * us[hf])
        ys.append(jax.nn.gelu(y, approximate=True))
    y = jnp.concatenate(ys, axis=1)
    o_s5 = y * jax.nn.sigmoid(_bdot(y, wglu_ref[...]) + bglu_ref[...])
    o_ref[...] = (x_ref[...] + _bdot(orw_ref[...], wrw_ref[...]) + _bdot(o_s5, ws5_ref[...]))


def _s5_mix(proj, x, o_rw, h0r, h0i, lw, layer, batch):
    rows = proj.shape[0]
    tm = min(ROW_TILE, rows)
    u_blk = RWKV_COLS // S5_HALF_W
    consts = [_Layer(h0r, layer), _Layer(h0i, layer)] + [
        _Layer(lw[n], layer) for n in ("s5_a_re", "s5_a_im", "s5_log_dt", "s5_b_re", "s5_b_im", "s5_c_re",
                                       "s5_c_im", "s5_d", "s5_w_glu", "s5_b_glu", "w_out_rw", "w_out_s5")]
    st_shape = jax.ShapeDtypeStruct((batch, S5_LANES), F32)
    st_spec = pl.BlockSpec((batch, S5_LANES), lambda i: (0, 0))
    in_mats = pltpu.VMEM((2, S5_HALF_W, S5_HALF_L), BF16)
    return pl.pallas_call(
        functools.partial(_s5_kernel, batch=batch),
        grid=(rows // tm,),
        in_specs=[pl.BlockSpec((tm, S5_HALF_W), lambda i: (i, u_blk)),
                  pl.BlockSpec((tm, S5_HALF_W), lambda i: (i, u_blk + 1)),
                  pl.BlockSpec((tm, D_MODEL), lambda i: (i, 0)),
                  pl.BlockSpec((tm, RWKV_WIDTH), lambda i: (i, 0))]
                 + [_resident_spec(a) for a in consts],
        out_specs=[pl.BlockSpec((tm, D_MODEL), lambda i: (i, 0)), st_spec, st_spec],
        out_shape=[jax.ShapeDtypeStruct((rows, D_MODEL), F32), st_shape, st_shape],
        scratch_shapes=[pltpu.VMEM((batch, S5_LANES), F32), pltpu.VMEM((batch, S5_LANES), F32),
                        pltpu.VMEM((tm, S5_LANES), F32), pltpu.VMEM((tm, S5_LANES), F32),
                        in_mats, in_mats],
        compiler_params=_cparams("arbitrary"),
        name="s5_mix",
    )(proj, proj, x, o_rw, *[_array(a) for a in consts])


def _softmax_rows(s):
    e = jnp.exp(s - jnp.max(s, axis=-1, keepdims=True))
    return e / jnp.sum(e, axis=-1, keepdims=True)


CACHE_ROWS = N_MEM * X_HEADS * (X_HEAD_DIM // LANES)
CACHE_GROUP = X_HEADS * (X_HEAD_DIM // LANES)
Q_ROWS = 8


def _cache_view(cache):
    l, b = cache.shape[:2]
    halves = X_HEAD_DIM // LANES
    c = cache.reshape(l, b, N_MEM, X_HEADS, halves, LANES)
    return jnp.swapaxes(c, 3, 4).reshape(l, b, CACHE_ROWS, LANES)


def _attn_cache_kernel(q_ref, k_ref, v_ref, o_ref, q_scr, o_scr, *, steps, nseq):
    halves = X_HEAD_DIM // LANES
    scale = X_HEAD_DIM ** -0.5
    n_tiles = D_MODEL // LANES
    rows = steps * nseq
    q = q_ref[...].reshape(rows, D_MODEL)
    pad = jnp.zeros(((Q_ROWS - steps) * nseq, D_MODEL), F32)
    _stage(q_scr, jnp.concatenate([q, pad], axis=0))
    col = lax.broadcasted_iota(jnp.int32, (Q_ROWS, CACHE_ROWS), 1) % CACHE_GROUP
    for b in range(nseq):
        seq = _seq_rows(b, Q_ROWS, nseq)
        qx = jnp.concatenate([q_scr[n, seq, :] for n in range(n_tiles)], axis=0)
        e = _bdot_nt(qx, k_ref[b])
        probs = []
        for h in range(X_HEADS):
            base = h * halves * Q_ROWS
            valid = col == h
            s = jnp.where(valid, e[base:base + Q_ROWS], 0.0)
            for j in range(1, halves):
                part = jnp.where(col == j * X_HEADS + h,
                                 e[base + j * Q_ROWS:base + (j + 1) * Q_ROWS], 0.0)
                s = s + pltpu.roll(part, shift=CACHE_ROWS - j * X_HEADS, axis=1)
            pr = _softmax_rows(jnp.where(valid, s * scale, -1e30))
            probs.append(pr)
            for j in range(1, halves):
                probs.append(pltpu.roll(pr, shift=j * X_HEADS, axis=1))
        ox = _bdot(jnp.concatenate(probs, axis=0), v_ref[b])
        for n in range(n_tiles):
            o_scr[n, seq, :] = ox[n * Q_ROWS:(n + 1) * Q_ROWS]
    o = jnp.concatenate([o_scr[n, :rows, :] for n in range(n_tiles)], axis=1)
    o_ref[...] = o.reshape(steps, nseq, D_MODEL)


def _attn_cache(q, cache_k, cache_v, layer, steps, batch):
    nseq = SEQ_GROUP
    q_spec = pl.BlockSpec((steps, nseq, D_MODEL), lambda j: (0, j, 0))
    m_spec = pl.BlockSpec((None, nseq, CACHE_ROWS, LANES), lambda j: (layer, j, 0, 0))
    stage = pltpu.VMEM((D_MODEL // LANES, Q_ROWS * nseq, LANES), F32)
    out = pl.pallas_call(
        functools.partial(_attn_cache_kernel, steps=steps, nseq=nseq),
        grid=(batch // nseq,),
        in_specs=[q_spec, m_spec, m_spec],
        out_specs=q_spec,
        out_shape=jax.ShapeDtypeStruct((steps, batch, D_MODEL), F32),
        scratch_shapes=[stage, stage],
        compiler_params=_cparams("parallel"),
        name="mem_attn",
    )(q.reshape(steps, batch, D_MODEL), cache_k, cache_v)
    return out.reshape(steps * batch, D_MODEL)


def _cross_long_kernel(x_ref, gain_ref, wq_ref, k_ref, v_ref, wo_ref, o_ref,
                       q_scr, att_scr, s_scr, p_scr, *, batch):
    x = x_ref[...]
    steps = x.shape[0] // batch
    scale = X_HEAD_DIM ** -0.5
    tiles_per_head = X_HEAD_DIM // LANES
    n_tiles = D_MODEL // LANES
    blocks = [(b, h) for b in range(batch) for h in range(X_HEADS)]
    _stage(q_scr, jnp.dot(_rms(x, gain_ref[...]).astype(BF16), wq_ref[...],
                          preferred_element_type=F32))
    for i, (b, h) in enumerate(blocks):
        rows = _seq_rows(b, steps, batch)
        tiles = range(h * tiles_per_head, (h + 1) * tiles_per_head)
        q = jnp.concatenate([q_scr[j, rows, :] for j in tiles], axis=1)
        s_scr[i] = _bdot_nt(q, k_ref[b, :, h * X_HEAD_DIM:(h + 1) * X_HEAD_DIM]) * scale
    p_scr[...] = _softmax_rows(s_scr[...]).astype(BF16)
    for i, (b, h) in enumerate(blocks):
        rows = _seq_rows(b, steps, batch)
        o = jnp.dot(p_scr[i], v_ref[b, :, h * X_HEAD_DIM:(h + 1) * X_HEAD_DIM],
                    preferred_element_type=F32)
        for t in range(tiles_per_head):
            att_scr[h * tiles_per_head + t, rows, :] = o[:, t * LANES:(t + 1) * LANES]
    att = jnp.concatenate([att_scr[j] for j in range(n_tiles)], axis=1)
    o_ref[...] = x + jnp.dot(att.astype(BF16), wo_ref[...], preferred_element_type=F32)


def _cross_long(x, mem_k, mem_v, lw, layer, batch):
    rows = x.shape[0]
    tm = min(ATTN_ROW_TILE, rows)
    steps = tm // batch
    consts = [_Layer(lw["norm_cross"], layer), _Layer(lw["w_cq"], layer), _Layer(mem_k, layer),
              _Layer(mem_v, layer), _Layer(lw["w_co"], layer)]
    return pl.pallas_call(
        functools.partial(_cross_long_kernel, batch=batch),
        grid=(rows // tm,),
        in_specs=[pl.BlockSpec((tm, D_MODEL), lambda i: (i, 0))] + [_resident_spec(a) for a in consts],
        out_specs=pl.BlockSpec((tm, D_MODEL), lambda i: (i, 0)),
        out_shape=jax.ShapeDtypeStruct((rows, D_MODEL), F32),
        scratch_shapes=[pltpu.VMEM((D_MODEL // LANES, tm, LANES), F32),
                        pltpu.VMEM((D_MODEL // LANES, tm, LANES), F32),
                        pltpu.VMEM((batch * X_HEADS, steps, N_MEM), F32),
                        pltpu.VMEM((batch * X_HEADS, steps, N_MEM), BF16)],
        compiler_params=_cparams("parallel"),
        name="cross_attn",
    )(x, *[_array(a) for a in consts])


def _ffn_kernel(x_ref, buf0_ref, gain_ref, wg_ref, wu_ref, cw_ref, cb_ref, wd_ref,
                o_ref, buf_out, carry, *, batch):
    @pl.when(pl.program_id(0) == 0)
    def _():
        carry[...] = buf0_ref[...]

    x = x_ref[...]
    tm = x.shape[0]
    h = _rms(x, gain_ref[...]).astype(BF16)
    gt = jnp.dot(h, wg_ref[...], preferred_element_type=F32)
    up = jnp.dot(h, wu_ref[...], preferred_element_type=F32)
    padded = jnp.concatenate([carry[...], gt], axis=0)
    conv = cb_ref[...]
    for i in range(CONV_W):
        conv = conv + cw_ref[i:i + 1, :] * padded[i * batch:i * batch + tm]
    new_carry = padded[tm:]
    carry[...] = new_carry
    buf_out[...] = new_carry
    act = jax.nn.silu(conv) * up
    o_ref[...] = x + jnp.dot(act.astype(BF16), wd_ref[...], preferred_element_type=F32)


def _ffn(x, buf0, lw, layer, batch):
    rows = x.shape[0]
    tm = min(FFN_ROW_TILE, rows)
    tm = max(tm, (CONV_W - 1) * batch)
    consts = [_Layer(buf0, layer)] + [_Layer(lw[n], layer) for n in (
        "norm_ffn", "w_gate", "w_up", "ffn_conv_w", "ffn_conv_b", "w_down")]
    nbuf = (CONV_W - 1) * batch
    return pl.pallas_call(
        functools.partial(_ffn_kernel, batch=batch),
        grid=(rows // tm,),
        in_specs=[pl.BlockSpec((tm, D_MODEL), lambda i: (i, 0))] + [_resident_spec(a) for a in consts],
        out_specs=[pl.BlockSpec((tm, D_MODEL), lambda i: (i, 0)),
                   pl.BlockSpec((nbuf, D_FF), lambda i: (0, 0))],
        out_shape=[jax.ShapeDtypeStruct((rows, D_MODEL), F32), jax.ShapeDtypeStruct((nbuf, D_FF), F32)],
        scratch_shapes=[pltpu.VMEM((nbuf, D_FF), F32)],
        compiler_params=_cparams("arbitrary"),
        name="conv_ffn",
    )(x, *[_array(a) for a in consts])


def _s5_in_blockdiag(b):
    l = b.shape[0]
    b = b.reshape(l, 2, S5_GROUPS // 2, S5_STATE, S5_GROUP)
    eye = jnp.eye(S5_GROUPS // 2, dtype=b.dtype)
    m = jnp.einsum('lfgph,gk->lfghkp', b, eye)
    return m.reshape(l, 2, S5_HALF_W, S5_HALF_L)


def _s5_out_blockdiag(c):
    l = c.shape[0]
    c = c.reshape(l, 2, S5_GROUPS // 2, S5_GROUP, S5_STATE)
    eye = jnp.eye(S5_GROUPS // 2, dtype=c.dtype)
    m = jnp.einsum('lfgnp,gk->lfgpkn', c, eye)
    return m.reshape(l, 2, S5_HALF_L, S5_HALF_W).astype(BF16)


def _prep_weights(p):
    l = DEPTH
    row = lambda a: a.reshape(a.shape[0], 1, -1)
    lw = {}
    for name in ("norm_mix", "shift_mu", "rwkv_w0", "rwkv_a0", "rwkv_v0", "rwkv_k_k", "rwkv_k_a",
                 "rwkv_r_k", "rwkv_lnx_w", "rwkv_lnx_b", "s5_d", "s5_b_glu", "norm_cross", "norm_ffn",
                 "ffn_conv_b", "s5_a_re", "s5_a_im"):
        lw[name] = row(p[name])
    lw["s5_log_dt"] = row(jnp.repeat(p["s5_log_dt"], S5_STATE, axis=-1))
    lw["norm_final"] = p["norm_final"].reshape(1, -1)
    lw["ffn_conv_w"] = p["ffn_conv_w"]
    for name in ("w_in", "rwkv_w_g2", "s5_w_glu", "w_cq", "w_ck", "w_cv", "w_co", "w_gate", "w_up",
                 "w_down"):
        lw[name] = p[name].astype(BF16)
    w_out = p["w_out"].astype(BF16)
    lw["w_out_rw"] = w_out[:, :RWKV_WIDTH]
    lw["w_out_s5"] = w_out[:, RWKV_WIDTH:]
    z64 = jnp.zeros((l, LORA_PAD - 64, RWKV_WIDTH), F32)
    lw["rwkv_w_w2p"] = jnp.concatenate([p["rwkv_w_w2"], z64], axis=1).astype(BF16)
    lw["rwkv_w_a2p"] = jnp.concatenate([z64, p["rwkv_w_a2"]], axis=1).astype(BF16)
    v_lora = p["rwkv_w_v1"].shape[-1]
    lw["rwkv_w_v1p"] = jnp.pad(p["rwkv_w_v1"], ((0, 0), (0, 0), (0, LORA_PAD - v_lora))).astype(BF16)
    lw["rwkv_w_v2p"] = jnp.pad(p["rwkv_w_v2"], ((0, 0), (0, LORA_PAD - v_lora), (0, 0))).astype(BF16)
    head = jnp.arange(PAIR_W) // RWKV_HEAD
    lw["pair_ones"] = (head[:, None] == head[None, :]).astype(BF16)
    lw["s5_b_re"] = _s5_in_blockdiag(p["s5_b_re"])
    lw["s5_b_im"] = _s5_in_blockdiag(p["s5_b_im"])
    lw["s5_c_re"] = _s5_out_blockdiag(p["s5_c_re"])
    lw["s5_c_im"] = _s5_out_blockdiag(p["s5_c_im"])
    return lw


def _run_trunk(x, mem_k, mem_v, st_rwkv, st_shift, st_re, st_im, st_conv, lw, steps, batch):
    long_seq = steps >= LONG_SEQ
    if long_seq:
        mem_k = mem_k.astype(BF16)
        mem_v = mem_v.astype(BF16)
    else:
        mem_k = _cache_view(mem_k)
        mem_v = _cache_view(mem_v)
    v_first = None
    new_rw, new_shift, new_re, new_im, new_conv = [], [], [], [], []
    for l in range(DEPTH):
        proj = _rowmm([x], [_Layer(lw["w_in"], l)], gain=_Layer(lw["norm_mix"], l), name="in_proj")
        o_rw, v_l, sh, s_rw = _rwkv_group(proj, st_shift, v_first, st_rwkv, lw, l, steps, batch)
        if l == 0:
            v_first = v_l
        x, hr, hi = _s5_mix(proj, x, o_rw, st_re, st_im, lw, l, batch)
        if long_seq:
            x = _cross_long(x, mem_k, mem_v, lw, l, batch)
        else:
            q = _rowmm([x], [_Layer(lw["w_cq"], l)], gain=_Layer(lw["norm_cross"], l), name="cross_q")
            att = _attn_cache(q, mem_k, mem_v, l, steps, batch)
            x = _rowmm([att], [_Layer(lw["w_co"], l)], resid=x, name="cross_o")
        x, cb = _ffn(x, st_conv, lw, l, batch)
        new_rw.append(s_rw)
        new_shift.append(sh)
        new_re.append(hr)
        new_im.append(hi)
        new_conv.append(cb)
    return (x, jnp.stack(new_rw), jnp.stack(new_shift), jnp.stack(new_re), jnp.stack(new_im),
            jnp.stack(new_conv))


def _group(x, mem_k, mem_v, st_rwkv, st_shift, st_re, st_im, st_conv, lw):
    b, t, _ = x.shape
    long_seq = t >= LONG_SEQ
    if long_seq:
        xt = _to_time_major(x)
    else:
        xt = jnp.swapaxes(x, 0, 1).reshape(t * b, D_MODEL)
    conv_t = jnp.swapaxes(st_conv, 1, 2).reshape(DEPTH, (CONV_W - 1) * b, D_FF)
    y, rw, sh, re, im, cv = _run_trunk(
        xt, mem_k, mem_v, st_rwkv, st_shift, st_re.reshape(DEPTH, b, S5_LANES),
        st_im.reshape(DEPTH, b, S5_LANES), conv_t, lw, t, b)
    if long_seq:
        y = _final_norm_long(y, lw["norm_final"], b)
    else:
        y = jnp.swapaxes(_rownorm(y, lw["norm_final"]).reshape(t, b, D_MODEL), 0, 1)
    cv = jnp.swapaxes(cv.reshape(DEPTH, CONV_W - 1, b, D_FF), 1, 2)
    return (y, rw, sh, re.reshape(DEPTH, b, S5_GROUPS, S5_STATE),
            im.reshape(DEPTH, b, S5_GROUPS, S5_STATE), cv)


def kernel(x_prompt, x_sample, mem_prompt, state_rwkv, state_shift, state_s5_re, state_s5_im, state_ffn_conv, cache_mem_k, cache_mem_v, norm_mix, w_in, shift_mu, rwkv_w0, rwkv_w_w2, rwkv_a0, rwkv_w_a2, rwkv_v0, rwkv_w_v1, rwkv_w_v2, rwkv_w_g2, rwkv_k_k, rwkv_k_a, rwkv_r_k, rwkv_lnx_w, rwkv_lnx_b, s5_a_re, s5_a_im, s5_log_dt, s5_b_re, s5_b_im, s5_c_re, s5_c_im, s5_d, s5_w_glu, s5_b_glu, w_out, norm_cross, w_cq, w_ck, w_cv, w_co, norm_ffn, w_gate, w_up, ffn_conv_w, ffn_conv_b, w_down, norm_final):
    lw = _prep_weights(dict(
        norm_mix=norm_mix, w_in=w_in, shift_mu=shift_mu, rwkv_w0=rwkv_w0, rwkv_w_w2=rwkv_w_w2,
        rwkv_a0=rwkv_a0, rwkv_w_a2=rwkv_w_a2, rwkv_v0=rwkv_v0, rwkv_w_v1=rwkv_w_v1,
        rwkv_w_v2=rwkv_w_v2, rwkv_w_g2=rwkv_w_g2, rwkv_k_k=rwkv_k_k, rwkv_k_a=rwkv_k_a,
        rwkv_r_k=rwkv_r_k, rwkv_lnx_w=rwkv_lnx_w, rwkv_lnx_b=rwkv_lnx_b, s5_a_re=s5_a_re,
        s5_a_im=s5_a_im, s5_log_dt=s5_log_dt, s5_b_re=s5_b_re, s5_b_im=s5_b_im, s5_c_re=s5_c_re,
        s5_c_im=s5_c_im, s5_d=s5_d, s5_w_glu=s5_w_glu, s5_b_glu=s5_b_glu, w_out=w_out,
        norm_cross=norm_cross, w_cq=w_cq, w_ck=w_ck, w_cv=w_cv, w_co=w_co, norm_ffn=norm_ffn,
        w_gate=w_gate, w_up=w_up, ffn_conv_w=ffn_conv_w, ffn_conv_b=ffn_conv_b, w_down=w_down,
        norm_final=norm_final))
    bp = x_prompt.shape[0]
    mem_rows = mem_prompt.reshape(bp * N_MEM, D_MODEL)
    p_mem_k = jnp.stack([_rowmm([mem_rows], [_Layer(lw["w_ck"], l)], name="mem_k") for l in range(DEPTH)])
    p_mem_v = jnp.stack([_rowmm([mem_rows], [_Layer(lw["w_cv"], l)], name="mem_v") for l in range(DEPTH)])
    p_mem_k = p_mem_k.reshape(DEPTH, bp, N_MEM, D_MODEL)
    p_mem_v = p_mem_v.reshape(DEPTH, bp, N_MEM, D_MODEL)
    z_rw = jnp.zeros((DEPTH, bp, RWKV_HEADS, RWKV_HEAD, RWKV_HEAD), F32)
    z_shift = jnp.zeros((DEPTH, bp, RWKV_COLS), F32)
    z_s5 = jnp.zeros((DEPTH, bp, S5_GROUPS, S5_STATE), F32)
    z_conv = jnp.zeros((DEPTH, bp, CONV_W - 1, D_FF), F32)
    y_prompt, p_rwkv, p_shift, p_re, p_im, p_conv = _group(
        x_prompt, p_mem_k, p_mem_v, z_rw, z_shift, z_s5, z_s5, z_conv, lw)
    y_sample, s_rwkv, s_shift, s_re, s_im, s_conv = _group(
        x_sample, cache_mem_k, cache_mem_v, state_rwkv, state_shift, state_s5_re, state_s5_im,
        state_ffn_conv, lw)
    return (y_prompt, y_sample, p_rwkv, p_shift, p_re, p_im, p_conv,
            p_mem_k.reshape(DEPTH, bp, N_MEM, X_HEADS, X_HEAD_DIM),
            p_mem_v.reshape(DEPTH, bp, N_MEM, X_HEADS, X_HEAD_DIM),
            s_rwkv, s_shift, s_re, s_im, s_conv)
```

```python
import functools
import math

import jax
import jax.numpy as jnp
from jax import lax
from jax.experimental import pallas as pl
from jax.experimental.pallas import tpu as pltpu

F32 = jnp.float32
BF16 = jnp.bfloat16

D_MODEL = 1024
DEPTH = 4
RWKV_WIDTH = 512
RWKV_HEAD = 64
RWKV_HEADS = 8
HEAD_PAIRS = RWKV_HEADS // 2
LANES = 128
PAIR_W = 2 * RWKV_HEAD
LORA_PAD = 128
RWKV_COLS = 3 * RWKV_WIDTH + 64 + 64 + 128
S5_WIDTH = 512
S5_GROUP = 16
S5_GROUPS = 32
S5_STATE = 64
S5_LANES = S5_GROUPS * S5_STATE
S5_HALF_W = S5_WIDTH // 2
S5_HALF_L = S5_LANES // 2
IN_COLS = RWKV_COLS + S5_WIDTH
N_MEM = 256
X_HEADS = 4
X_HEAD_DIM = 256
D_FF = 2816
CONV_W = 3
RMS_EPS = 1e-6
LNX_EPS = 64e-5

ROW_TILE = 512
FFN_ROW_TILE = 512
IN_PROJ_ROW_TILE = 1024
ATTN_ROW_TILE = 1024
LONG_SEQ = 64
LONG_CHUNK = 64
SHORT_CHUNK = 8
SEQ_GROUP = 8
VMEM_LIMIT = 56 * 1024 * 1024


def _cparams(*sem):
    return pltpu.CompilerParams(dimension_semantics=sem, vmem_limit_bytes=VMEM_LIMIT)


def _const_spec(shape):
    nd = len(shape)
    return pl.BlockSpec(shape, lambda *_: (0,) * nd, pipeline_mode=pl.Buffered(1))


class _Layer:
    def __init__(self, stacked, layer):
        self.array = stacked
        self.layer = layer
        self.shape = stacked.shape[1:]


def _resident_spec(op):
    if isinstance(op, _Layer):
        nd = len(op.shape)
        layer = op.layer
        return pl.BlockSpec((None,) + tuple(op.shape), lambda *_: (layer,) + (0,) * nd,
                            pipeline_mode=pl.Buffered(1))
    return _const_spec(op.shape)


def _array(op):
    return op.array if isinstance(op, _Layer) else op


def _bdot(a, b):
    return jnp.dot(a.astype(BF16), b.astype(BF16), preferred_element_type=F32)


def _bdot_nt(a, b):
    return lax.dot_general(a.astype(BF16), b.astype(BF16), (((1,), (1,)), ((), ())),
                           preferred_element_type=F32)


def _bdot_tn(a, b):
    return lax.dot_general(a.astype(BF16), b.astype(BF16), (((0,), (0,)), ((), ())),
                           preferred_element_type=F32)


def _head_sums(x, pair_ones):
    outs = []
    for sl in _lane_tiles(x.shape[1]):
        hi = x[:, sl].astype(BF16)
        lo = (x[:, sl] - hi.astype(F32)).astype(BF16)
        outs.append(jnp.dot(hi, pair_ones, preferred_element_type=F32)
                    + jnp.dot(lo, pair_ones, preferred_element_type=F32))
    return jnp.concatenate(outs, axis=1)


def _rms(x, gain):
    return x * lax.rsqrt(jnp.mean(x * x, axis=-1, keepdims=True) + RMS_EPS) * gain


def _seq_rows(seq, steps, batch):
    return pl.ds(seq, steps, stride=batch)


def _lane_tiles(width):
    return [slice(j * LANES, (j + 1) * LANES) for j in range(width // LANES)]


def _stage(scr, x):
    for j, sl in enumerate(_lane_tiles(x.shape[1])):
        scr[j] = x[:, sl]


def _to_time_major_kernel(x_ref, o_ref, scr, *, batch, steps):
    tiles = _lane_tiles(x_ref.shape[2])
    for b in range(batch):
        for j, sl in enumerate(tiles):
            scr[j, _seq_rows(b, steps, batch), :] = x_ref[b, :, sl]
    for j, sl in enumerate(tiles):
        o_ref[:, sl] = scr[j]


def _to_time_major(x):
    b, t, d = x.shape
    tt = ROW_TILE // b
    return pl.pallas_call(
        functools.partial(_to_time_major_kernel, batch=b, steps=tt),
        grid=(t // tt,),
        in_specs=[pl.BlockSpec((b, tt, d), lambda i: (0, i, 0))],
        out_specs=pl.BlockSpec((tt * b, d), lambda i: (i, 0)),
        out_shape=jax.ShapeDtypeStruct((t * b, d), F32),
        scratch_shapes=[pltpu.VMEM((d // LANES, tt * b, LANES), F32)],
        compiler_params=_cparams("parallel"),
        name="to_time_major",
    )(x)


def _final_norm_long_kernel(x_ref, g_ref, o_ref, scr, *, batch, steps):
    _stage(scr, _rms(x_ref[...], g_ref[...]))
    for b in range(batch):
        for j, sl in enumerate(_lane_tiles(x_ref.shape[1])):
            o_ref[b, :, sl] = scr[j, _seq_rows(b, steps, batch), :]


def _final_norm_long(x, gain, batch):
    rows, d = x.shape
    t = rows // batch
    tt = ROW_TILE // batch
    return pl.pallas_call(
        functools.partial(_final_norm_long_kernel, batch=batch, steps=tt),
        grid=(t // tt,),
        in_specs=[pl.BlockSpec((tt * batch, d), lambda i: (i, 0)), _const_spec(gain.shape)],
        out_specs=pl.BlockSpec((batch, tt, d), lambda i: (0, i, 0)),
        out_shape=jax.ShapeDtypeStruct((batch, t, d), F32),
        scratch_shapes=[pltpu.VMEM((d // LANES, tt * batch, LANES), F32)],
        compiler_params=_cparams("parallel"),
        name="final_norm",
    )(x, gain)


def _rowmm_kernel(*refs, n_x, has_gain, has_resid):
    xs = refs[:n_x]
    ws = refs[n_x:2 * n_x]
    pos = 2 * n_x
    gain = refs[pos] if has_gain else None
    pos += int(has_gain)
    resid = refs[pos] if has_resid else None
    pos += int(has_resid)
    o_ref = refs[pos]
    acc = None
    for x_ref, w_ref in zip(xs, ws):
        x = x_ref[...]
        if has_gain:
            x = _rms(x, gain[...])
        d = jnp.dot(x.astype(BF16), w_ref[...], preferred_element_type=F32)
        acc = d if acc is None else acc + d
    if has_resid:
        acc = acc + resid[...]
    o_ref[...] = acc


def _rowmm(xs, ws, gain=None, resid=None, name="rowmm", row_tile=ROW_TILE):
    rows = xs[0].shape[0]
    n_out = ws[0].shape[1]
    tm = min(row_tile, rows)
    in_specs = [pl.BlockSpec((tm, x.shape[1]), lambda i: (i, 0)) for x in xs]
    in_specs += [_resident_spec(w) for w in ws]
    args = list(xs) + [_array(w) for w in ws]
    if gain is not None:
        in_specs.append(_resident_spec(gain))
        args.append(_array(gain))
    if resid is not None:
        in_specs.append(pl.BlockSpec((tm, n_out), lambda i: (i, 0)))
        args.append(resid)
    return pl.pallas_call(
        functools.partial(_rowmm_kernel, n_x=len(xs), has_gain=gain is not None,
                          has_resid=resid is not None),
        grid=(rows // tm,),
        in_specs=in_specs,
        out_specs=pl.BlockSpec((tm, n_out), lambda i: (i, 0)),
        out_shape=jax.ShapeDtypeStruct((rows, n_out), F32),
        compiler_params=_cparams("parallel"),
        name=name,
    )(*args)


def _layered_mm(x, w):
    rows, k = x.shape
    layers, _, n_out = w.shape
    tm = min(ROW_TILE, rows)
    return pl.pallas_call(
        functools.partial(_rowmm_kernel, n_x=1, has_gain=False, has_resid=False),
        grid=(layers, rows // tm),
        in_specs=[pl.BlockSpec((tm, k), lambda l, i: (i, 0)),
                  pl.BlockSpec((None, k, n_out), lambda l, i: (l, 0, 0))],
        out_specs=pl.BlockSpec((None, tm, n_out), lambda l, i: (l, i, 0)),
        out_shape=jax.ShapeDtypeStruct((layers, rows, n_out), F32),
        compiler_params=_cparams("parallel", "parallel"),
        name="mem_proj",
    )(x, w)


def _rownorm_kernel(x_ref, g_ref, o_ref):
    o_ref[...] = _rms(x_ref[...], g_ref[...])


def _rownorm(x, gain):
    rows, d = x.shape
    tm = min(ROW_TILE, rows)
    return pl.pallas_call(
        _rownorm_kernel,
        grid=(rows // tm,),
        in_specs=[pl.BlockSpec((tm, d), lambda i: (i, 0)), _const_spec(gain.shape)],
        out_specs=pl.BlockSpec((tm, d), lambda i: (i, 0)),
        out_shape=jax.ShapeDtypeStruct((rows, d), F32),
        compiler_params=_cparams("parallel"),
        name="final_norm",
    )(x, gain)


def _softplus(z):
    return jnp.maximum(z, 0.0) + jnp.log1p(jnp.exp(-jnp.abs(z)))


def _prep_consts(lw, layer, has_vfirst):
    names = ["shift_mu", "rwkv_w0", "rwkv_w_w2p", "rwkv_a0", "rwkv_w_a2p", "rwkv_w_g2", "rwkv_k_k",
             "rwkv_k_a"]
    consts = [_Layer(lw[n], layer) for n in names] + [lw["pair_ones"]]
    if has_vfirst:
        consts += [_Layer(lw[n], layer - 1) for n in ("rwkv_v0", "rwkv_w_v1p", "rwkv_w_v2p")]
    return consts


def _token_shift(p, carry, batch):
    tm = p.shape[0]
    prev = jnp.concatenate([carry, p[:tm - batch]], axis=0) if tm > batch else carry
    return prev, p[tm - batch:]


def _prep_math(p, p_prev, v_first, consts):
    mu, w0, ww2, a0, wa2, wg2, k_k, k_a, pair_ones = [c[...] for c in consts[:9]]
    q = p + (p_prev - p) * mu
    rw = RWKV_WIDTH
    r = q[:, 0:rw]
    k = q[:, rw:2 * rw]
    v = q[:, 2 * rw:3 * rw]
    x_wa = q[:, 3 * rw:3 * rw + LORA_PAD]
    x_g = q[:, 3 * rw + LORA_PAD:3 * rw + 2 * LORA_PAD]
    w = -_softplus(-(w0 + _bdot(jnp.tanh(x_wa), ww2))) - 0.5
    log_decay = -jnp.exp(w)
    a = jax.nn.sigmoid(a0 + _bdot(x_wa, wa2))
    g = _bdot(jax.nn.sigmoid(x_g), wg2)
    if v_first is not None:
        v0, wv1, wv2 = [c[...] for c in consts[9:12]]
        mix = jax.nn.sigmoid(v0 + _bdot(_bdot(v, wv1), wv2))
        v = v + (v_first - v) * mix
    kk = k * k_k
    kk = kk * lax.rsqrt(jnp.maximum(_head_sums(kk * kk, pair_ones), 1e-24))
    k = k * (1.0 + (a - 1.0) * k_a)
    return r, log_decay, k, v, kk, a, g


def _rec_scratch(chunk, nseq):
    c2 = 2 * chunk
    per = (nseq, HEAD_PAIRS)
    return [
        pltpu.VMEM(per + (PAIR_W, PAIR_W), F32),
        pltpu.VMEM(per + (2 * c2, PAIR_W), BF16),
        pltpu.VMEM(per + (2 * c2, PAIR_W), BF16),
        pltpu.VMEM(per + (c2, PAIR_W), BF16),
        pltpu.VMEM(per + (2 * c2, PAIR_W), BF16),
        pltpu.VMEM(per + (c2, c2), BF16),
        pltpu.VMEM(per + (c2, c2), BF16),
        pltpu.VMEM(per + (c2, c2), BF16),
        pltpu.VMEM(per + (c2, c2), BF16),
        pltpu.VMEM(per + (c2, PAIR_W), BF16),
        pltpu.VMEM(per + (c2, PAIR_W), F32),
        pltpu.VMEM(per + (c2, PAIR_W), F32),
        pltpu.VMEM((HEAD_PAIRS, chunk * nseq, PAIR_W), F32),
        pltpu.VMEM((nseq, 1, RWKV_WIDTH), F32),
        pltpu.VMEM((7, HEAD_PAIRS, chunk * nseq, PAIR_W), F32),
    ]


def _rec_phases(vals, s0_ref, post_consts, s_out_ref, scratch, *, chunk, nseq):
    (s_scr, ar_scr, bk_scr, v_scr, tl_scr, p_scr, lak_scr, mrb_scr, mrk_scr, rb_scr, r32_scr,
     aro_scr, o_scr, gend_scr, stage_scr) = scratch
    r, lw, k, v, kk, a, g = vals
    lnw_ref, lnb_ref, rk_ref, ones_ref = post_consts
    c = pl.program_id(1)
    cc = chunk
    c2 = 2 * cc
    fused = c2 % 128 == 0
    problems = [(b, p) for b in range(nseq) for p in range(HEAD_PAIRS)]

    @pl.when(c == 0)
    def _():
        zero = jnp.zeros((RWKV_HEAD, RWKV_HEAD), F32)
        for b, p in problems:
            top = jnp.concatenate([s0_ref[b, 2 * p], zero], axis=1)
            bottom = jnp.concatenate([zero, s0_ref[b, 2 * p + 1]], axis=1)
            s_scr[b, p] = jnp.concatenate([top, bottom], axis=0)

    lane = lax.broadcasted_iota(jnp.int32, (1, PAIR_W), 1)
    first = lane < RWKV_HEAD
    row2 = lax.broadcasted_iota(jnp.int32, (c2, c2), 0)
    col2 = lax.broadcasted_iota(jnp.int32, (c2, c2), 1)
    same = (row2 >= cc) == (col2 >= cc)
    rr = jnp.where(row2 >= cc, row2 - cc, row2)
    cl = jnp.where(col2 >= cc, col2 - cc, col2)
    strict = same & (cl < rr)
    incl = same & (cl <= rr)
    n_factors = max(1, math.ceil(math.log2(cc)))

    def stack(xs):
        return jnp.concatenate([jnp.where(first, xs, 0.0), jnp.where(first, 0.0, xs)],
                               axis=0).astype(BF16)

    cum = lw
    shift = nseq
    while shift < cc * nseq:
        cum = cum + jnp.concatenate([jnp.zeros((shift, RWKV_WIDTH), F32), cum[:-shift]], axis=0)
        shift *= 2
    cum_end = cum[(cc - 1) * nseq:]
    for b in range(nseq):
        gend_scr[b] = jnp.exp(cum_end[b:b + 1])
    cum_end = jnp.broadcast_to(cum_end[None], (cc, nseq, RWKV_WIDTH)).reshape(cc * nseq, RWKV_WIDTH)
    g_inv = jnp.exp(-cum)
    g_tail = jnp.exp(cum_end - cum)
    kka = kk * a
    operands = [-kk * jnp.exp(cum - lw), r * jnp.exp(cum),
                kka * g_inv, k * g_inv,
                kka * g_tail, k * g_tail, v]
    for i, x in enumerate(operands):
        _stage(stage_scr.at[i], x)

    def operand(i, b, p):
        return stage_scr[i, p, _seq_rows(b, cc, nseq), :]

    for b, p in problems:
        ar_scr[b, p, :c2] = stack(operand(0, b, p))
        ar_scr[b, p, c2:] = stack(operand(1, b, p))
        bk_scr[b, p, :c2] = stack(operand(2, b, p))
        bk_scr[b, p, c2:] = stack(operand(3, b, p))
        tl_scr[b, p, :c2] = stack(operand(4, b, p))
        tl_scr[b, p, c2:] = stack(operand(5, b, p))
        v_scr[b, p] = stack(operand(6, b, p))

    for b, p in problems:
        ar = ar_scr[b, p]
        bk = bk_scr[b, p]
        if fused:
            gram = _bdot_nt(ar, bk)
            g_ab, g_ak = gram[:c2, :c2], gram[:c2, c2:]
            g_rb, g_rk = gram[c2:, :c2], gram[c2:, c2:]
        else:
            g_ab, g_ak = _bdot_nt(ar[:c2], bk[:c2]), _bdot_nt(ar[:c2], bk[c2:])
            g_rb, g_rk = _bdot_nt(ar[c2:], bk[:c2]), _bdot_nt(ar[c2:], bk[c2:])
        p_scr[b, p] = jnp.where(strict, g_ab, 0.0).astype(BF16)
        lak_scr[b, p] = jnp.where(strict, g_ak, 0.0).astype(BF16)
        mrb_scr[b, p] = jnp.where(incl, g_rb, 0.0).astype(BF16)
        mrk_scr[b, p] = jnp.where(incl, g_rk, 0.0).astype(BF16)
        ar_state = _bdot_nt(ar, s_scr[b, p])
        r32_scr[b, p] = ar_state[:c2]
        aro_scr[b, p] = ar_state[c2:]

    for b, p in problems:
        rhs = r32_scr[b, p] + jnp.dot(lak_scr[b, p], v_scr[b, p], preferred_element_type=F32)
        r32_scr[b, p] = rhs
        rb_scr[b, p] = rhs.astype(BF16)

    for m in range(n_factors):
        last = m == n_factors - 1
        for b, p in problems:
            pw = p_scr[b, p]
            rb = rb_scr[b, p]
            if last:
                delta = jnp.dot(pw, rb, preferred_element_type=F32)
            elif fused:
                both = jnp.dot(pw, jnp.concatenate([pw, rb], axis=1), preferred_element_type=F32)
                p_scr[b, p] = both[:, :c2].astype(BF16)
                delta = both[:, c2:]
            else:
                p_scr[b, p] = jnp.dot(pw, pw, preferred_element_type=F32).astype(BF16)
                delta = jnp.dot(pw, rb, preferred_element_type=F32)
            rhs = r32_scr[b, p] + delta
            r32_scr[b, p] = rhs
            rb_scr[b, p] = rhs.astype(BF16)

    for b, p in problems:
        sl = slice(p * PAIR_W, (p + 1) * PAIR_W)
        u_s = rb_scr[b, p]
        v_s = v_scr[b, p]
        tl = tl_scr[b, p]
        if fused:
            uv = jnp.concatenate([u_s, v_s], axis=0)
            mm = jnp.concatenate([mrb_scr[b, p], mrk_scr[b, p]], axis=1)
            o_st = aro_scr[b, p] + jnp.dot(mm, uv, preferred_element_type=F32)
            upd = _bdot_tn(uv, tl)
        else:
            o_st = (aro_scr[b, p] + jnp.dot(mrb_scr[b, p], u_s, preferred_element_type=F32)
                    + jnp.dot(mrk_scr[b, p], v_s, preferred_element_type=F32))
            upd = _bdot_tn(u_s, tl[:c2]) + _bdot_tn(v_s, tl[c2:])
        o_scr[p, _seq_rows(b, cc, nseq), :] = o_st[:cc] + o_st[cc:]
        s_scr[b, p] = s_scr[b, p] * gend_scr[b][:, sl] + upd

    @pl.when(c == pl.num_programs(1) - 1)
    def _():
        for b, p in problems:
            s_pair = s_scr[b, p]
            s_out_ref[b, 2 * p] = s_pair[:RWKV_HEAD, :RWKV_HEAD]
            s_out_ref[b, 2 * p + 1] = s_pair[RWKV_HEAD:, RWKV_HEAD:]

    ones = ones_ref[...]
    inv_n = 1.0 / RWKV_HEAD
    o = jnp.concatenate([o_scr[j] for j in range(HEAD_PAIRS)], axis=1)
    mean = _head_sums(o, ones) * inv_n
    d = o - mean
    var = _head_sums(d * d, ones) * inv_n
    on = d * lax.rsqrt(var + LNX_EPS) * lnw_ref[...] + lnb_ref[...]
    bonus = _head_sums(r * k * rk_ref[...], ones) * v
    return (on + bonus) * g


def _post_consts(lw, layer):
    return [_Layer(lw[n], layer) for n in ("rwkv_lnx_w", "rwkv_lnx_b", "rwkv_r_k")] + [lw["pair_ones"]]


def _rwkv_group_kernel(*refs, chunk, steps, nseq, has_vfirst, n_consts):
    p_ref, shift0_ref = refs[:2]
    consts = refs[2:2 + n_consts]
    pos = 2 + n_consts
    vf_ref = refs[pos] if has_vfirst else None
    pos += int(has_vfirst)
    s0_ref = refs[pos]
    post = refs[pos + 1:pos + 5]
    pos += 5
    o_ref, v_out, shift_out, s_out_ref, carry = refs[pos:pos + 5]
    scratch = refs[pos + 5:]
    rows = steps * nseq

    @pl.when(pl.program_id(1) == 0)
    def _():
        carry[...] = shift0_ref[...]

    p = p_ref[...].reshape(rows, RWKV_COLS)
    p_prev, new_carry = _token_shift(p, carry[...], nseq)
    carry[...] = new_carry
    shift_out[...] = new_carry
    v_first = vf_ref[...].reshape(rows, RWKV_WIDTH) if has_vfirst else None
    vals = _prep_math(p, p_prev, v_first, consts)
    v_out[...] = vals[3].reshape(steps, nseq, RWKV_WIDTH)
    if steps < chunk:
        pad = jnp.zeros(((chunk - steps) * nseq, RWKV_WIDTH), F32)
        vals = [jnp.concatenate([x, pad], axis=0) for x in vals]
    o = _rec_phases(vals, s0_ref, post, s_out_ref, scratch, chunk=chunk, nseq=nseq)
    o_ref[...] = o[:rows].reshape(steps, nseq, RWKV_WIDTH)


def _rwkv_group(proj, shift0, v_first, state, lw, layer, steps, batch):
    nseq = SEQ_GROUP
    chunk = LONG_CHUNK if steps >= LONG_CHUNK else SHORT_CHUNK
    tile_steps = min(chunk, steps)
    has_vfirst = v_first is not None
    consts = _prep_consts(lw, layer, has_vfirst)
    post = _post_consts(lw, layer)
    rows3 = lambda a: a.reshape(steps, batch, a.shape[-1])
    row_spec = pl.BlockSpec((tile_steps, nseq, RWKV_WIDTH), lambda j, c: (c, j, 0))
    shift_spec = pl.BlockSpec((nseq, RWKV_COLS), lambda j, c: (j, 0))
    st_spec = pl.BlockSpec((None, nseq, RWKV_HEADS, RWKV_HEAD, RWKV_HEAD), lambda j, c: (layer, j, 0, 0, 0))
    args = ([rows3(proj), shift0] + [_array(a) for a in consts]
            + ([rows3(v_first)] if has_vfirst else []) + [state] + [_array(a) for a in post])
    state_arg = 2 + len(consts) + int(has_vfirst)
    in_specs = ([pl.BlockSpec((tile_steps, nseq, RWKV_COLS), lambda j, c: (c, j, 0)),
                 pl.BlockSpec((None, nseq, RWKV_COLS), lambda j, c: (layer, j, 0))]
                + [_resident_spec(a) for a in consts] + ([row_spec] if has_vfirst else [])
                + [st_spec] + [_resident_spec(a) for a in post])
    rows_shape = jax.ShapeDtypeStruct((steps, batch, RWKV_WIDTH), F32)
    o, v, shift, s_new = pl.pallas_call(
        functools.partial(_rwkv_group_kernel, chunk=chunk, steps=tile_steps, nseq=nseq,
                          has_vfirst=has_vfirst, n_consts=len(consts)),
        grid=(batch // nseq, steps // tile_steps),
        in_specs=in_specs,
        out_specs=[row_spec, row_spec, shift_spec, st_spec],
        out_shape=[rows_shape, rows_shape, jax.ShapeDtypeStruct((batch, RWKV_COLS), F32),
                   jax.ShapeDtypeStruct(state.shape, F32)],
        input_output_aliases={state_arg: 3},
        scratch_shapes=[pltpu.VMEM((nseq, RWKV_COLS), F32)] + _rec_scratch(chunk, nseq),
        compiler_params=_cparams("parallel", "arbitrary"),
        name="rwkv_group",
    )(*args)
    return o.reshape(steps * batch, RWKV_WIDTH), v.reshape(steps * batch, RWKV_WIDTH), shift, s_new


def _s5_kernel(u0_ref, u1_ref, x_ref, orw_ref, h0r_ref, h0i_ref, are_ref, aim_ref, ldt_ref, bre_ref,
               bim_ref, cre_ref, cim_ref, d_ref, wglu_ref, bglu_ref, wrw_ref, ws5_ref,
               o_ref, hr_out, hi_out, hr_c, hi_c, hre, him, bbre, bbim, *, batch):
    a_re = are_ref[...]
    a_im = aim_ref[...]
    dt = jnp.exp(ldt_ref[...])
    mag = jnp.exp(a_re * dt)
    ab_re = mag * jnp.cos(a_im * dt)
    ab_im = mag * jnp.sin(a_im * dt)

    @pl.when(pl.program_id(0) == 0)
    def _():
        hr_c[...] = h0r_ref[...]
        hi_c[...] = h0i_ref[...]
        den = a_re * a_re + a_im * a_im
        nr = ab_re - 1.0
        cf_re = (nr * a_re + ab_im * a_im) / den
        cf_im = (ab_im * a_re - nr * a_im) / den
        for hf in range(2):
            ls = slice(hf * S5_HALF_L, (hf + 1) * S5_HALF_L)
            b_re = bre_ref[hf]
            b_im = bim_ref[hf]
            bbre[hf] = (cf_re[:, ls] * b_re - cf_im[:, ls] * b_im).astype(BF16)
            bbim[hf] = (cf_re[:, ls] * b_im + cf_im[:, ls] * b_re).astype(BF16)

    us = (u0_ref[...], u1_ref[...])
    tm = us[0].shape[0]
    for hf in range(2):
        ls = slice(hf * S5_HALF_L, (hf + 1) * S5_HALF_L)
        ub = us[hf].astype(BF16)
        hre[:, ls] = jnp.dot(ub, bbre[hf], preferred_element_type=F32)
        him[:, ls] = jnp.dot(ub, bbim[hf], preferred_element_type=F32)

    n_steps = tm // batch
    if n_steps <= 8:
        hr = hr_c[...]
        hi = hi_c[...]
        for s in range(n_steps):
            rows = slice(s * batch, (s + 1) * batch)
            nhr = ab_re * hr - ab_im * hi + hre[rows, :]
            nhi = ab_re * hi + ab_im * hr + him[rows, :]
            hre[rows, :] = nhr
            him[rows, :] = nhi
            hr, hi = nhr, nhi
        hr_c[...] = hr
        hi_c[...] = hi
    else:
        lane_w = 512
        for lc in range(S5_LANES // lane_w):
            ls = slice(lc * lane_w, (lc + 1) * lane_w)
            abr = jnp.broadcast_to(ab_re[:, ls], (batch, lane_w))
            abi = jnp.broadcast_to(ab_im[:, ls], (batch, lane_w))

            def body(s, carry, ls=ls, abr=abr, abi=abi):
                hr, hi = carry
                rows = pl.ds(pl.multiple_of(s * batch, batch), batch)
                nhr = abr * hr - abi * hi + hre[rows, ls]
                nhi = abr * hi + abi * hr + him[rows, ls]
                hre[rows, ls] = nhr
                him[rows, ls] = nhi
                return nhr, nhi

            hr, hi = lax.fori_loop(0, n_steps, body, (hr_c[:, ls], hi_c[:, ls]), unroll=8)
            hr_c[:, ls] = hr
            hi_c[:, ls] = hi

    hr_out[...] = hr_c[...]
    hi_out[...] = hi_c[...]

    ys = []
    for hf in range(2):
        ls = slice(hf * S5_HALF_L, (hf + 1) * S5_HALF_L)
        cs = slice(hf * S5_HALF_W, (hf + 1) * S5_HALF_W)
        y = (jnp.dot(hre[:, ls].astype(BF16), cre_ref[hf], preferred_element_type=F32)
             - jnp.dot(him[:, ls].astype(BF16), cim_ref[hf], preferred_element_type=F32)
             + d_ref[:, cs] * us[hf])
        ys.append(jax.nn.gelu(y, approximate=True))
    y = jnp.concatenate(ys, axis=1)
    o_s5 = y * jax.nn.sigmoid(_bdot(y, wglu_ref[...]) + bglu_ref[...])
    o_ref[...] = (x_ref[...] + _bdot(orw_ref[...], wrw_ref[...]) + _bdot(o_s5, ws5_ref[...]))


def _s5_mix(proj, x, o_rw, h0r, h0i, lw, layer, batch):
    rows = proj.shape[0]
    tm = min(ROW_TILE, rows)
    u_blk = RWKV_COLS // S5_HALF_W
    consts = [_Layer(h0r, layer), _Layer(h0i, layer)] + [
        _Layer(lw[n], layer) for n in ("s5_a_re", "s5_a_im", "s5_log_dt", "s5_b_re", "s5_b_im", "s5_c_re",
                                       "s5_c_im", "s5_d", "s5_w_glu", "s5_b_glu", "w_out_rw", "w_out_s5")]
    st_shape = jax.ShapeDtypeStruct((batch, S5_LANES), F32)
    st_spec = pl.BlockSpec((batch, S5_LANES), lambda i: (0, 0))
    in_mats = pltpu.VMEM((2, S5_HALF_W, S5_HALF_L), BF16)
    return pl.pallas_call(
        functools.partial(_s5_kernel, batch=batch),
        grid=(rows // tm,),
        in_specs=[pl.BlockSpec((tm, S5_HALF_W), lambda i: (i, u_blk)),
                  pl.BlockSpec((tm, S5_HALF_W), lambda i: (i, u_blk + 1)),
                  pl.BlockSpec((tm, D_MODEL), lambda i: (i, 0)),
                  pl.BlockSpec((tm, RWKV_WIDTH), lambda i: (i, 0))]
                 + [_resident_spec(a) for a in consts],
        out_specs=[pl.BlockSpec((tm, D_MODEL), lambda i: (i, 0)), st_spec, st_spec],
        out_shape=[jax.ShapeDtypeStruct((rows, D_MODEL), F32), st_shape, st_shape],
        scratch_shapes=[pltpu.VMEM((batch, S5_LANES), F32), pltpu.VMEM((batch, S5_LANES), F32),
                        pltpu.VMEM((tm, S5_LANES), F32), pltpu.VMEM((tm, S5_LANES), F32),
                        in_mats, in_mats],
        compiler_params=_cparams("arbitrary"),
        name="s5_mix",
    )(proj, proj, x, o_rw, *[_array(a) for a in consts])


def _softmax_rows(s):
    e = jnp.exp(s - jnp.max(s, axis=-1, keepdims=True))
    return e / jnp.sum(e, axis=-1, keepdims=True)


CACHE_ROWS = N_MEM * X_HEADS * (X_HEAD_DIM // LANES)
CACHE_GROUP = X_HEADS * (X_HEAD_DIM // LANES)
Q_ROWS = 8


def _cache_view(cache):
    l, b = cache.shape[:2]
    halves = X_HEAD_DIM // LANES
    c = cache.reshape(l, b, N_MEM, X_HEADS, halves, LANES)
    return jnp.swapaxes(c, 3, 4).reshape(l, b, CACHE_ROWS, LANES)


def _attn_cache_kernel(q_ref, k_ref, v_ref, o_ref, q_scr, o_scr, *, steps, nseq):
    halves = X_HEAD_DIM // LANES
    scale = X_HEAD_DIM ** -0.5
    n_tiles = D_MODEL // LANES
    rows = steps * nseq
    q = q_ref[...].reshape(rows, D_MODEL)
    pad = jnp.zeros(((Q_ROWS - steps) * nseq, D_MODEL), F32)
    _stage(q_scr, jnp.concatenate([q, pad], axis=0))
    col = lax.broadcasted_iota(jnp.int32, (Q_ROWS, CACHE_ROWS), 1) % CACHE_GROUP
    for b in range(nseq):
        seq = _seq_rows(b, Q_ROWS, nseq)
        qx = jnp.concatenate([q_scr[n, seq, :] for n in range(n_tiles)], axis=0)
        e = _bdot_nt(qx, k_ref[b])
        probs = []
        for h in range(X_HEADS):
            base = h * halves * Q_ROWS
            valid = col == h
            s = jnp.where(valid, e[base:base + Q_ROWS], 0.0)
            for j in range(1, halves):
                part = jnp.where(col == j * X_HEADS + h,
                                 e[base + j * Q_ROWS:base + (j + 1) * Q_ROWS], 0.0)
                s = s + pltpu.roll(part, shift=CACHE_ROWS - j * X_HEADS, axis=1)
            pr = _softmax_rows(jnp.where(valid, s * scale, -1e30))
            probs.append(pr)
            for j in range(1, halves):
                probs.append(pltpu.roll(pr, shift=j * X_HEADS, axis=1))
        ox = _bdot(jnp.concatenate(probs, axis=0), v_ref[b])
        for n in range(n_tiles):
            o_scr[n, seq, :] = ox[n * Q_ROWS:(n + 1) * Q_ROWS]
    o = jnp.concatenate([o_scr[n, :rows, :] for n in range(n_tiles)], axis=1)
    o_ref[...] = o.reshape(steps, nseq, D_MODEL)


def _attn_cache(q, cache_k, cache_v, layer, steps, batch):
    nseq = SEQ_GROUP
    q_spec = pl.BlockSpec((steps, nseq, D_MODEL), lambda j: (0, j, 0))
    m_spec = pl.BlockSpec((None, nseq, CACHE_ROWS, LANES), lambda j: (layer, j, 0, 0))
    stage = pltpu.VMEM((D_MODEL // LANES, Q_ROWS * nseq, LANES), F32)
    out = pl.pallas_call(
        functools.partial(_attn_cache_kernel, steps=steps, nseq=nseq),
        grid=(batch // nseq,),
        in_specs=[q_spec, m_spec, m_spec],
        out_specs=q_spec,
        out_shape=jax.ShapeDtypeStruct((steps, batch, D_MODEL), F32),
        scratch_shapes=[stage, stage],
        compiler_params=_cparams("parallel"),
        name="mem_attn",
    )(q.reshape(steps, batch, D_MODEL), cache_k, cache_v)
    return out.reshape(steps * batch, D_MODEL)


def _cross_long_kernel(x_ref, gain_ref, wq_ref, k_ref, v_ref, wo_ref, o_ref,
                       q_scr, att_scr, s_scr, p_scr, *, batch):
    x = x_ref[...]
    steps = x.shape[0] // batch
    scale = X_HEAD_DIM ** -0.5
    tiles_per_head = X_HEAD_DIM // LANES
    n_tiles = D_MODEL // LANES
    blocks = [(b, h) for b in range(batch) for h in range(X_HEADS)]
    _stage(q_scr, jnp.dot(_rms(x, gain_ref[...]).astype(BF16), wq_ref[...],
                          preferred_element_type=F32))
    for i, (b, h) in enumerate(blocks):
        rows = _seq_rows(b, steps, batch)
        tiles = range(h * tiles_per_head, (h + 1) * tiles_per_head)
        q = jnp.concatenate([q_scr[j, rows, :] for j in tiles], axis=1)
        s_scr[i] = _bdot_nt(q, k_ref[b, :, h * X_HEAD_DIM:(h + 1) * X_HEAD_DIM]) * scale
    p_scr[...] = _softmax_rows(s_scr[...]).astype(BF16)
    for i, (b, h) in enumerate(blocks):
        rows = _seq_rows(b, steps, batch)
        o = jnp.dot(p_scr[i], v_ref[b, :, h * X_HEAD_DIM:(h + 1) * X_HEAD_DIM],
                    preferred_element_type=F32)
        for t in range(tiles_per_head):
            att_scr[h * tiles_per_head + t, rows, :] = o[:, t * LANES:(t + 1) * LANES]
    att = jnp.concatenate([att_scr[j] for j in range(n_tiles)], axis=1)
    o_ref[...] = x + jnp.dot(att.astype(BF16), wo_ref[...], preferred_element_type=F32)


def _cross_long(x, mem_k, mem_v, lw, layer, batch):
    rows = x.shape[0]
    tm = min(ATTN_ROW_TILE, rows)
    steps = tm // batch
    consts = [_Layer(lw["norm_cross"], layer), _Layer(lw["w_cq"], layer), _Layer(mem_k, layer),
              _Layer(mem_v, layer), _Layer(lw["w_co"], layer)]
    return pl.pallas_call(
        functools.partial(_cross_long_kernel, batch=batch),
        grid=(rows // tm,),
        in_specs=[pl.BlockSpec((tm, D_MODEL), lambda i: (i, 0))] + [_resident_spec(a) for a in consts],
        out_specs=pl.BlockSpec((tm, D_MODEL), lambda i: (i, 0)),
        out_shape=jax.ShapeDtypeStruct((rows, D_MODEL), F32),
        scratch_shapes=[pltpu.VMEM((D_MODEL // LANES, tm, LANES), F32),
                        pltpu.VMEM((D_MODEL // LANES, tm, LANES), F32),
                        pltpu.VMEM((batch * X_HEADS, steps, N_MEM), F32),
                        pltpu.VMEM((batch * X_HEADS, steps, N_MEM), BF16)],
        compiler_params=_cparams("parallel"),
        name="cross_attn",
    )(x, *[_array(a) for a in consts])


def _ffn_kernel(x_ref, buf0_ref, gain_ref, wg_ref, wu_ref, cw_ref, cb_ref, wd_ref,
                o_ref, buf_out, carry, *, batch):
    @pl.when(pl.program_id(0) == 0)
    def _():
        carry[...] = buf0_ref[...]

    x = x_ref[...]
    tm = x.shape[0]
    h = _rms(x, gain_ref[...]).astype(BF16)
    gt = jnp.dot(h, wg_ref[...], preferred_element_type=F32)
    up = jnp.dot(h, wu_ref[...], preferred_element_type=F32)
    padded = jnp.concatenate([carry[...], gt], axis=0)
    conv = cb_ref[...]
    for i in range(CONV_W):
        conv = conv + cw_ref[i:i + 1, :] * padded[i * batch:i * batch + tm]
    new_carry = padded[tm:]
    carry[...] = new_carry
    buf_out[...] = new_carry
    act = jax.nn.silu(conv) * up
    o_ref[...] = x + jnp.dot(act.astype(BF16), wd_ref[...], preferred_element_type=F32)


def _ffn(x, buf0, lw, layer, batch):
    rows = x.shape[0]
    tm = min(FFN_ROW_TILE, rows)
    tm = max(tm, (CONV_W - 1) * batch)
    consts = [_Layer(buf0, layer)] + [_Layer(lw[n], layer) for n in (
        "norm_ffn", "w_gate", "w_up", "ffn_conv_w", "ffn_conv_b", "w_down")]
    nbuf = (CONV_W - 1) * batch
    return pl.pallas_call(
        functools.partial(_ffn_kernel, batch=batch),
        grid=(rows // tm,),
        in_specs=[pl.BlockSpec((tm, D_MODEL), lambda i: (i, 0))] + [_resident_spec(a) for a in consts],
        out_specs=[pl.BlockSpec((tm, D_MODEL), lambda i: (i, 0)),
                   pl.BlockSpec((nbuf, D_FF), lambda i: (0, 0))],
        out_shape=[jax.ShapeDtypeStruct((rows, D_MODEL), F32), jax.ShapeDtypeStruct((nbuf, D_FF), F32)],
        scratch_shapes=[pltpu.VMEM((nbuf, D_FF), F32)],
        compiler_params=_cparams("arbitrary"),
        name="conv_ffn",
    )(x, *[_array(a) for a in consts])


def _s5_in_blockdiag(b):
    l = b.shape[0]
    b = b.reshape(l, 2, S5_GROUPS // 2, S5_STATE, S5_GROUP)
    eye = jnp.eye(S5_GROUPS // 2, dtype=b.dtype)
    m = jnp.einsum('lfgph,gk->lfghkp', b, eye)
    return m.reshape(l, 2, S5_HALF_W, S5_HALF_L)


def _s5_out_blockdiag(c):
    l = c.shape[0]
    c = c.reshape(l, 2, S5_GROUPS // 2, S5_GROUP, S5_STATE)
    eye = jnp.eye(S5_GROUPS // 2, dtype=c.dtype)
    m = jnp.einsum('lfgnp,gk->lfgpkn', c, eye)
    return m.reshape(l, 2, S5_HALF_L, S5_HALF_W).astype(BF16)


def _prep_weights(p):
    l = DEPTH
    row = lambda a: a.reshape(a.shape[0], 1, -1)
    lw = {}
    for name in ("norm_mix", "shift_mu", "rwkv_w0", "rwkv_a0", "rwkv_v0", "rwkv_k_k", "rwkv_k_a",
                 "rwkv_r_k", "rwkv_lnx_w", "rwkv_lnx_b", "s5_d", "s5_b_glu", "norm_cross", "norm_ffn",
                 "ffn_conv_b", "s5_a_re", "s5_a_im"):
        lw[name] = row(p[name])
    lw["s5_log_dt"] = row(jnp.repeat(p["s5_log_dt"], S5_STATE, axis=-1))
    lw["norm_final"] = p["norm_final"].reshape(1, -1)
    lw["ffn_conv_w"] = p["ffn_conv_w"]
    for name in ("w_in", "rwkv_w_g2", "s5_w_glu", "w_cq", "w_ck", "w_cv", "w_co", "w_gate", "w_up",
                 "w_down"):
        lw[name] = p[name].astype(BF16)
    w_out = p["w_out"].astype(BF16)
    lw["w_out_rw"] = w_out[:, :RWKV_WIDTH]
    lw["w_out_s5"] = w_out[:, RWKV_WIDTH:]
    z64 = jnp.zeros((l, LORA_PAD - 64, RWKV_WIDTH), F32)
    lw["rwkv_w_w2p"] = jnp.concatenate([p["rwkv_w_w2"], z64], axis=1).astype(BF16)
    lw["rwkv_w_a2p"] = jnp.concatenate([z64, p["rwkv_w_a2"]], axis=1).astype(BF16)
    v_lora = p["rwkv_w_v1"].shape[-1]
    lw["rwkv_w_v1p"] = jnp.pad(p["rwkv_w_v1"], ((0, 0), (0, 0), (0, LORA_PAD - v_lora))).astype(BF16)
    lw["rwkv_w_v2p"] = jnp.pad(p["rwkv_w_v2"], ((0, 0), (0, LORA_PAD - v_lora), (0, 0))).astype(BF16)
    head = jnp.arange(PAIR_W) // RWKV_HEAD
    lw["pair_ones"] = (head[:, None] == head[None, :]).astype(BF16)
    lw["s5_b_re"] = _s5_in_blockdiag(p["s5_b_re"])
    lw["s5_b_im"] = _s5_in_blockdiag(p["s5_b_im"])
    lw["s5_c_re"] = _s5_out_blockdiag(p["s5_c_re"])
    lw["s5_c_im"] = _s5_out_blockdiag(p["s5_c_im"])
    return lw


def _run_trunk(x, mem_k, mem_v, st_rwkv, st_shift, st_re, st_im, st_conv, lw, steps, batch):
    long_seq = steps >= LONG_SEQ
    if long_seq:
        mem_k = mem_k.astype(BF16)
        mem_v = mem_v.astype(BF16)
    else:
        mem_k = _cache_view(mem_k)
        mem_v = _cache_view(mem_v)
    v_first = None
    new_shift, new_re, new_im, new_conv = [], [], [], []
    for l in range(DEPTH):
        proj = _rowmm([x], [_Layer(lw["w_in"], l)], gain=_Layer(lw["norm_mix"], l), name="in_proj",
                      row_tile=IN_PROJ_ROW_TILE)
        o_rw, v_l, sh, st_rwkv = _rwkv_group(proj, st_shift, v_first, st_rwkv, lw, l, steps, batch)
        if l == 0:
            v_first = v_l
        x, hr, hi = _s5_mix(proj, x, o_rw, st_re, st_im, lw, l, batch)
        if long_seq:
            x = _cross_long(x, mem_k, mem_v, lw, l, batch)
        else:
            q = _rowmm([x], [_Layer(lw["w_cq"], l)], gain=_Layer(lw["norm_cross"], l), name="cross_q")
            att = _attn_cache(q, mem_k, mem_v, l, steps, batch)
            x = _rowmm([att], [_Layer(lw["w_co"], l)], resid=x, name="cross_o")
        x, cb = _ffn(x, st_conv, lw, l, batch)
        new_shift.append(sh)
        new_re.append(hr)
        new_im.append(hi)
        new_conv.append(cb)
    return (x, st_rwkv, jnp.stack(new_shift), jnp.stack(new_re), jnp.stack(new_im),
            jnp.stack(new_conv))


def _group(x, mem_k, mem_v, st_rwkv, st_shift, st_re, st_im, st_conv, lw):
    b, t, _ = x.shape
    long_seq = t >= LONG_SEQ
    if long_seq:
        xt = _to_time_major(x)
    else:
        xt = jnp.swapaxes(x, 0, 1).reshape(t * b, D_MODEL)
    conv_t = jnp.swapaxes(st_conv, 1, 2).reshape(DEPTH, (CONV_W - 1) * b, D_FF)
    y, rw, sh, re, im, cv = _run_trunk(
        xt, mem_k, mem_v, st_rwkv, st_shift, st_re.reshape(DEPTH, b, S5_LANES),
        st_im.reshape(DEPTH, b, S5_LANES), conv_t, lw, t, b)
    if long_seq:
        y = _final_norm_long(y, lw["norm_final"], b)
    else:
        y = jnp.swapaxes(_rownorm(y, lw["norm_final"]).reshape(t, b, D_MODEL), 0, 1)
    cv = jnp.swapaxes(cv.reshape(DEPTH, CONV_W - 1, b, D_FF), 1, 2)
    return (y, rw, sh, re.reshape(DEPTH, b, S5_GROUPS, S5_STATE),
            im.reshape(DEPTH, b, S5_GROUPS, S5_STATE), cv)


def kernel(x_prompt, x_sample, mem_prompt, state_rwkv, state_shift, state_s5_re, state_s5_im, state_ffn_conv, cache_mem_k, cache_mem_v, norm_mix, w_in, shift_mu, rwkv_w0, rwkv_w_w2, rwkv_a0, rwkv_w_a2, rwkv_v0, rwkv_w_v1, rwkv_w_v2, rwkv_w_g2, rwkv_k_k, rwkv_k_a, rwkv_r_k, rwkv_lnx_w, rwkv_lnx_b, s5_a_re, s5_a_im, s5_log_dt, s5_b_re, s5_b_im, s5_c_re, s5_c_im, s5_d, s5_w_glu, s5_b_glu, w_out, norm_cross, w_cq, w_ck, w_cv, w_co, norm_ffn, w_gate, w_up, ffn_conv_w, ffn_conv_b, w_down, norm_final):
    lw = _prep_weights(dict(
        norm_mix=norm_mix, w_in=w_in, shift_mu=shift_mu, rwkv_w0=rwkv_w0, rwkv_w_w2=rwkv_w_w2,
        rwkv_a0=rwkv_a0, rwkv_w_a2=rwkv_w_a2, rwkv_v0=rwkv_v0, rwkv_w_v1=rwkv_w_v1,
        rwkv_w_v2=rwkv_w_v2, rwkv_w_g2=rwkv_w_g2, rwkv_k_k=rwkv_k_k, rwkv_k_a=rwkv_k_a,
        rwkv_r_k=rwkv_r_k, rwkv_lnx_w=rwkv_lnx_w, rwkv_lnx_b=rwkv_lnx_b, s5_a_re=s5_a_re,
        s5_a_im=s5_a_im, s5_log_dt=s5_log_dt, s5_b_re=s5_b_re, s5_b_im=s5_b_im, s5_c_re=s5_c_re,
        s5_c_im=s5_c_im, s5_d=s5_d, s5_w_glu=s5_w_glu, s5_b_glu=s5_b_glu, w_out=w_out,
        norm_cross=norm_cross, w_cq=w_cq, w_ck=w_ck, w_cv=w_cv, w_co=w_co, norm_ffn=norm_ffn,
        w_gate=w_gate, w_up=w_up, ffn_conv_w=ffn_conv_w, ffn_conv_b=ffn_conv_b, w_down=w_down,
        norm_final=norm_final))
    bp = x_prompt.shape[0]
    mem_rows = mem_prompt.reshape(bp * N_MEM, D_MODEL)
    p_mem_k = _layered_mm(mem_rows, lw["w_ck"]).reshape(DEPTH, bp, N_MEM, D_MODEL)
    p_mem_v = _layered_mm(mem_rows, lw["w_cv"]).reshape(DEPTH, bp, N_MEM, D_MODEL)
    z_rw = jnp.zeros((DEPTH, bp, RWKV_HEADS, RWKV_HEAD, RWKV_HEAD), F32)
    z_shift = jnp.zeros((DEPTH, bp, RWKV_COLS), F32)
    z_s5 = jnp.zeros((DEPTH, bp, S5_GROUPS, S5_STATE), F32)
    z_conv = jnp.zeros((DEPTH, bp, CONV_W - 1, D_FF), F32)
    y_prompt, p_rwkv, p_shift, p_re, p_im, p_conv = _group(
        x_prompt, p_mem_k, p_mem_v, z_rw, z_shift, z_s5, z_s5, z_conv, lw)
    y_sample, s_rwkv, s_shift, s_re, s_im, s_conv = _group(
        x_sample, cache_mem_k, cache_mem_v, state_rwkv, state_shift, state_s5_re, state_s5_im,
        state_ffn_conv, lw)
    return (y_prompt, y_sample, p_rwkv, p_shift, p_re, p_im, p_conv,
            p_mem_k.reshape(DEPTH, bp, N_MEM, X_HEADS, X_HEAD_DIM),
            p_mem_v.reshape(DEPTH, bp, N_MEM, X_HEADS, X_HEAD_DIM),
            s_rwkv, s_shift, s_re, s_im, s_conv)
```

```python
import functools
import math

import jax
import jax.numpy as jnp
from jax import lax
from jax.experimental import pallas as pl
from jax.experimental.pallas import tpu as pltpu

F32 = jnp.float32
BF16 = jnp.bfloat16

D_MODEL = 1024
DEPTH = 4
RWKV_WIDTH = 512
RWKV_HEAD = 64
RWKV_HEADS = 8
HEAD_PAIRS = RWKV_HEADS // 2
LANES = 128
PAIR_W = 2 * RWKV_HEAD
LORA_PAD = 128
RWKV_COLS = 3 * RWKV_WIDTH + 64 + 64 + 128
S5_WIDTH = 512
S5_GROUP = 16
S5_GROUPS = 32
S5_STATE = 64
S5_LANES = S5_GROUPS * S5_STATE
S5_HALF_W = S5_WIDTH // 2
S5_HALF_L = S5_LANES // 2
IN_COLS = RWKV_COLS + S5_WIDTH
N_MEM = 256
X_HEADS = 4
X_HEAD_DIM = 256
D_FF = 2816
CONV_W = 3
RMS_EPS = 1e-6
LNX_EPS = 64e-5

ROW_TILE = 512
FFN_ROW_TILE = 512
IN_PROJ_ROW_TILE = 1024
ATTN_ROW_TILE = 1024
LONG_SEQ = 64
LONG_CHUNK = 64
SHORT_CHUNK = 8
SEQ_GROUP = 8
VMEM_LIMIT = 56 * 1024 * 1024


def _cparams(*sem):
    return pltpu.CompilerParams(dimension_semantics=sem, vmem_limit_bytes=VMEM_LIMIT)


def _const_spec(shape):
    nd = len(shape)
    return pl.BlockSpec(shape, lambda *_: (0,) * nd, pipeline_mode=pl.Buffered(1))


class _Layer:
    def __init__(self, stacked, layer):
        self.array = stacked
        self.layer = layer
        self.shape = stacked.shape[1:]


def _resident_spec(op):
    if isinstance(op, _Layer):
        nd = len(op.shape)
        layer = op.layer
        return pl.BlockSpec((None,) + tuple(op.shape), lambda *_: (layer,) + (0,) * nd,
                            pipeline_mode=pl.Buffered(1))
    return _const_spec(op.shape)


def _array(op):
    return op.array if isinstance(op, _Layer) else op


def _bdot(a, b):
    return jnp.dot(a.astype(BF16), b.astype(BF16), preferred_element_type=F32)


def _bdot_nt(a, b):
    return lax.dot_general(a.astype(BF16), b.astype(BF16), (((1,), (1,)), ((), ())),
                           preferred_element_type=F32)


def _bdot_tn(a, b):
    return lax.dot_general(a.astype(BF16), b.astype(BF16), (((0,), (0,)), ((), ())),
                           preferred_element_type=F32)


def _head_sums(x, pair_ones):
    outs = []
    for sl in _lane_tiles(x.shape[1]):
        hi = x[:, sl].astype(BF16)
        lo = (x[:, sl] - hi.astype(F32)).astype(BF16)
        outs.append(jnp.dot(hi, pair_ones, preferred_element_type=F32)
                    + jnp.dot(lo, pair_ones, preferred_element_type=F32))
    return jnp.concatenate(outs, axis=1)


def _rms(x, gain):
    return x * lax.rsqrt(jnp.mean(x * x, axis=-1, keepdims=True) + RMS_EPS) * gain


def _seq_rows(seq, steps, batch):
    return pl.ds(seq, steps, stride=batch)


def _lane_tiles(width):
    return [slice(j * LANES, (j + 1) * LANES) for j in range(width // LANES)]


def _stage(scr, x):
    for j, sl in enumerate(_lane_tiles(x.shape[1])):
        scr[j] = x[:, sl]


def _to_time_major_kernel(x_ref, o_ref, scr, *, batch, steps):
    tiles = _lane_tiles(x_ref.shape[2])
    for b in range(batch):
        for j, sl in enumerate(tiles):
            scr[j, _seq_rows(b, steps, batch), :] = x_ref[b, :, sl]
    for j, sl in enumerate(tiles):
        o_ref[:, sl] = scr[j]


def _to_time_major(x):
    b, t, d = x.shape
    tt = ROW_TILE // b
    return pl.pallas_call(
        functools.partial(_to_time_major_kernel, batch=b, steps=tt),
        grid=(t // tt,),
        in_specs=[pl.BlockSpec((b, tt, d), lambda i: (0, i, 0))],
        out_specs=pl.BlockSpec((tt * b, d), lambda i: (i, 0)),
        out_shape=jax.ShapeDtypeStruct((t * b, d), F32),
        scratch_shapes=[pltpu.VMEM((d // LANES, tt * b, LANES), F32)],
        compiler_params=_cparams("parallel"),
        name="to_time_major",
    )(x)


def _final_norm_long_kernel(x_ref, g_ref, o_ref, scr, *, batch, steps):
    _stage(scr, _rms(x_ref[...], g_ref[...]))
    for b in range(batch):
        for j, sl in enumerate(_lane_tiles(x_ref.shape[1])):
            o_ref[b, :, sl] = scr[j, _seq_rows(b, steps, batch), :]


def _final_norm_long(x, gain, batch):
    rows, d = x.shape
    t = rows // batch
    tt = ROW_TILE // batch
    return pl.pallas_call(
        functools.partial(_final_norm_long_kernel, batch=batch, steps=tt),
        grid=(t // tt,),
        in_specs=[pl.BlockSpec((tt * batch, d), lambda i: (i, 0)), _const_spec(gain.shape)],
        out_specs=pl.BlockSpec((batch, tt, d), lambda i: (0, i, 0)),
        out_shape=jax.ShapeDtypeStruct((batch, t, d), F32),
        scratch_shapes=[pltpu.VMEM((d // LANES, tt * batch, LANES), F32)],
        compiler_params=_cparams("parallel"),
        name="final_norm",
    )(x, gain)


def _rowmm_kernel(*refs, n_x, has_gain, has_resid):
    xs = refs[:n_x]
    ws = refs[n_x:2 * n_x]
    pos = 2 * n_x
    gain = refs[pos] if has_gain else None
    pos += int(has_gain)
    resid = refs[pos] if has_resid else None
    pos += int(has_resid)
    o_ref = refs[pos]
    acc = None
    for x_ref, w_ref in zip(xs, ws):
        x = x_ref[...]
        if has_gain:
            x = _rms(x, gain[...])
        d = jnp.dot(x.astype(BF16), w_ref[...], preferred_element_type=F32)
        acc = d if acc is None else acc + d
    if has_resid:
        acc = acc + resid[...]
    o_ref[...] = acc


def _rowmm(xs, ws, gain=None, resid=None, name="rowmm", row_tile=ROW_TILE):
    rows = xs[0].shape[0]
    n_out = ws[0].shape[1]
    tm = min(row_tile, rows)
    in_specs = [pl.BlockSpec((tm, x.shape[1]), lambda i: (i, 0)) for x in xs]
    in_specs += [_resident_spec(w) for w in ws]
    args = list(xs) + [_array(w) for w in ws]
    if gain is not None:
        in_specs.append(_resident_spec(gain))
        args.append(_array(gain))
    if resid is not None:
        in_specs.append(pl.BlockSpec((tm, n_out), lambda i: (i, 0)))
        args.append(resid)
    return pl.pallas_call(
        functools.partial(_rowmm_kernel, n_x=len(xs), has_gain=gain is not None,
                          has_resid=resid is not None),
        grid=(rows // tm,),
        in_specs=in_specs,
        out_specs=pl.BlockSpec((tm, n_out), lambda i: (i, 0)),
        out_shape=jax.ShapeDtypeStruct((rows, n_out), F32),
        compiler_params=_cparams("parallel"),
        name=name,
    )(*args)


def _mem_proj_kernel(x_ref, w_ref, z_ref, rows_ref):
    y = jnp.dot(x_ref[...].astype(BF16), w_ref[...], preferred_element_type=F32)
    rows_ref[...] = y.astype(BF16)
    halves = X_HEAD_DIM // LANES
    group = X_HEADS * halves
    for n, sl in enumerate(_lane_tiles(D_MODEL)):
        head, half = divmod(n, halves)
        z_ref[pl.ds(half * X_HEADS + head, y.shape[0], stride=group), :] = y[:, sl]


def _mem_proj(x, w):
    rows, k = x.shape
    layers, _, n_out = w.shape
    tm = min(ROW_TILE, rows)
    group = n_out // LANES
    return pl.pallas_call(
        _mem_proj_kernel,
        grid=(layers, rows // tm),
        in_specs=[pl.BlockSpec((tm, k), lambda l, i: (i, 0)),
                  pl.BlockSpec((None, k, n_out), lambda l, i: (l, 0, 0))],
        out_specs=[pl.BlockSpec((None, tm * group, LANES), lambda l, i: (l, i, 0)),
                   pl.BlockSpec((None, tm, n_out), lambda l, i: (l, i, 0))],
        out_shape=[jax.ShapeDtypeStruct((layers, rows * group, LANES), F32),
                   jax.ShapeDtypeStruct((layers, rows, n_out), BF16)],
        compiler_params=_cparams("parallel", "parallel"),
        name="mem_proj",
    )(x, w)


def _rownorm_kernel(x_ref, g_ref, o_ref):
    o_ref[...] = _rms(x_ref[...], g_ref[...])


def _rownorm(x, gain):
    rows, d = x.shape
    tm = min(ROW_TILE, rows)
    return pl.pallas_call(
        _rownorm_kernel,
        grid=(rows // tm,),
        in_specs=[pl.BlockSpec((tm, d), lambda i: (i, 0)), _const_spec(gain.shape)],
        out_specs=pl.BlockSpec((tm, d), lambda i: (i, 0)),
        out_shape=jax.ShapeDtypeStruct((rows, d), F32),
        compiler_params=_cparams("parallel"),
        name="final_norm",
    )(x, gain)


def _softplus(z):
    return jnp.maximum(z, 0.0) + jnp.log1p(jnp.exp(-jnp.abs(z)))


def _prep_consts(lw, layer, has_vfirst):
    names = ["shift_mu", "rwkv_w0", "rwkv_w_w2p", "rwkv_a0", "rwkv_w_a2p", "rwkv_w_g2", "rwkv_k_k",
             "rwkv_k_a"]
    consts = [_Layer(lw[n], layer) for n in names] + [lw["pair_ones"]]
    if has_vfirst:
        consts += [_Layer(lw[n], layer - 1) for n in ("rwkv_v0", "rwkv_w_v1p", "rwkv_w_v2p")]
    return consts


def _token_shift(p, carry, batch):
    tm = p.shape[0]
    prev = jnp.concatenate([carry, p[:tm - batch]], axis=0) if tm > batch else carry
    return prev, p[tm - batch:]


def _prep_math(p, p_prev, v_first, consts):
    mu, w0, ww2, a0, wa2, wg2, k_k, k_a, pair_ones = [c[...] for c in consts[:9]]
    q = p + (p_prev - p) * mu
    rw = RWKV_WIDTH
    r = q[:, 0:rw]
    k = q[:, rw:2 * rw]
    v = q[:, 2 * rw:3 * rw]
    x_wa = q[:, 3 * rw:3 * rw + LORA_PAD]
    x_g = q[:, 3 * rw + LORA_PAD:3 * rw + 2 * LORA_PAD]
    w = -_softplus(-(w0 + _bdot(jnp.tanh(x_wa), ww2))) - 0.5
    log_decay = -jnp.exp(w)
    a = jax.nn.sigmoid(a0 + _bdot(x_wa, wa2))
    g = _bdot(jax.nn.sigmoid(x_g), wg2)
    if v_first is not None:
        v0, wv1, wv2 = [c[...] for c in consts[9:12]]
        mix = jax.nn.sigmoid(v0 + _bdot(_bdot(v, wv1), wv2))
        v = v + (v_first - v) * mix
    kk = k * k_k
    kk = kk * lax.rsqrt(jnp.maximum(_head_sums(kk * kk, pair_ones), 1e-24))
    k = k * (1.0 + (a - 1.0) * k_a)
    return r, log_decay, k, v, kk, a, g


def _rec_scratch(chunk, nseq):
    c2 = 2 * chunk
    per = (nseq, HEAD_PAIRS)
    return [
        pltpu.VMEM(per + (PAIR_W, PAIR_W), F32),
        pltpu.VMEM(per + (2 * c2, PAIR_W), BF16),
        pltpu.VMEM(per + (2 * c2, PAIR_W), BF16),
        pltpu.VMEM(per + (c2, PAIR_W), BF16),
        pltpu.VMEM(per + (2 * c2, PAIR_W), BF16),
        pltpu.VMEM(per + (c2, c2), BF16),
        pltpu.VMEM(per + (c2, c2), BF16),
        pltpu.VMEM(per + (c2, c2), BF16),
        pltpu.VMEM(per + (c2, c2), BF16),
        pltpu.VMEM(per + (c2, PAIR_W), BF16),
        pltpu.VMEM(per + (c2, PAIR_W), F32),
        pltpu.VMEM(per + (c2, PAIR_W), F32),
        pltpu.VMEM((HEAD_PAIRS, chunk * nseq, PAIR_W), F32),
        pltpu.VMEM((nseq, 1, RWKV_WIDTH), F32),
        pltpu.VMEM((7, HEAD_PAIRS, chunk * nseq, PAIR_W), F32),
    ]


def _rec_phases(vals, s0_ref, post_consts, s_out_ref, scratch, *, chunk, nseq):
    (s_scr, ar_scr, bk_scr, v_scr, tl_scr, p_scr, lak_scr, mrb_scr, mrk_scr, rb_scr, r32_scr,
     aro_scr, o_scr, gend_scr, stage_scr) = scratch
    r, lw, k, v, kk, a, g = vals
    lnw_ref, lnb_ref, rk_ref, ones_ref = post_consts
    c = pl.program_id(1)
    cc = chunk
    c2 = 2 * cc
    fused = c2 % 128 == 0
    problems = [(b, p) for b in range(nseq) for p in range(HEAD_PAIRS)]

    @pl.when(c == 0)
    def _():
        zero = jnp.zeros((RWKV_HEAD, RWKV_HEAD), F32)
        for b, p in problems:
            top = jnp.concatenate([s0_ref[b, 2 * p], zero], axis=1)
            bottom = jnp.concatenate([zero, s0_ref[b, 2 * p + 1]], axis=1)
            s_scr[b, p] = jnp.concatenate([top, bottom], axis=0)

    lane = lax.broadcasted_iota(jnp.int32, (1, PAIR_W), 1)
    first = lane < RWKV_HEAD
    row2 = lax.broadcasted_iota(jnp.int32, (c2, c2), 0)
    col2 = lax.broadcasted_iota(jnp.int32, (c2, c2), 1)
    same = (row2 >= cc) == (col2 >= cc)
    rr = jnp.where(row2 >= cc, row2 - cc, row2)
    cl = jnp.where(col2 >= cc, col2 - cc, col2)
    strict = same & (cl < rr)
    incl = same & (cl <= rr)
    n_factors = max(1, math.ceil(math.log2(cc)))

    def stack(xs):
        return jnp.concatenate([jnp.where(first, xs, 0.0), jnp.where(first, 0.0, xs)],
                               axis=0).astype(BF16)

    cum = lw
    shift = nseq
    while shift < cc * nseq:
        cum = cum + jnp.concatenate([jnp.zeros((shift, RWKV_WIDTH), F32), cum[:-shift]], axis=0)
        shift *= 2
    cum_end = cum[(cc - 1) * nseq:]
    for b in range(nseq):
        gend_scr[b] = jnp.exp(cum_end[b:b + 1])
    cum_end = jnp.broadcast_to(cum_end[None], (cc, nseq, RWKV_WIDTH)).reshape(cc * nseq, RWKV_WIDTH)
    g_inv = jnp.exp(-cum)
    g_tail = jnp.exp(cum_end - cum)
    kka = kk * a
    operands = [-kk * jnp.exp(cum - lw), r * jnp.exp(cum),
                kka * g_inv, k * g_inv,
                kka * g_tail, k * g_tail, v]
    for i, x in enumerate(operands):
        _stage(stage_scr.at[i], x)

    def operand(i, b, p):
        return stage_scr[i, p, _seq_rows(b, cc, nseq), :]

    for b, p in problems:
        ar_scr[b, p, :c2] = stack(operand(0, b, p))
        ar_scr[b, p, c2:] = stack(operand(1, b, p))
        bk_scr[b, p, :c2] = stack(operand(2, b, p))
        bk_scr[b, p, c2:] = stack(operand(3, b, p))
        tl_scr[b, p, :c2] = stack(operand(4, b, p))
        tl_scr[b, p, c2:] = stack(operand(5, b, p))
        v_scr[b, p] = stack(operand(6, b, p))

    for b, p in problems:
        ar = ar_scr[b, p]
        bk = bk_scr[b, p]
        if fused:
            gram = _bdot_nt(ar, bk)
            g_ab, g_ak = gram[:c2, :c2], gram[:c2, c2:]
            g_rb, g_rk = gram[c2:, :c2], gram[c2:, c2:]
        else:
            g_ab, g_ak = _bdot_nt(ar[:c2], bk[:c2]), _bdot_nt(ar[:c2], bk[c2:])
            g_rb, g_rk = _bdot_nt(ar[c2:], bk[:c2]), _bdot_nt(ar[c2:], bk[c2:])
        p_scr[b, p] = jnp.where(strict, g_ab, 0.0).astype(BF16)
        lak_scr[b, p] = jnp.where(strict, g_ak, 0.0).astype(BF16)
        mrb_scr[b, p] = jnp.where(incl, g_rb, 0.0).astype(BF16)
        mrk_scr[b, p] = jnp.where(incl, g_rk, 0.0).astype(BF16)
        ar_state = _bdot_nt(ar, s_scr[b, p])
        r32_scr[b, p] = ar_state[:c2]
        aro_scr[b, p] = ar_state[c2:]

    for b, p in problems:
        rhs = r32_scr[b, p] + jnp.dot(lak_scr[b, p], v_scr[b, p], preferred_element_type=F32)
        r32_scr[b, p] = rhs
        rb_scr[b, p] = rhs.astype(BF16)

    for m in range(n_factors):
        last = m == n_factors - 1
        for b, p in problems:
            pw = p_scr[b, p]
            rb = rb_scr[b, p]
            if last:
                delta = jnp.dot(pw, rb, preferred_element_type=F32)
            elif fused:
                both = jnp.dot(pw, jnp.concatenate([pw, rb], axis=1), preferred_element_type=F32)
                p_scr[b, p] = both[:, :c2].astype(BF16)
                delta = both[:, c2:]
            else:
                p_scr[b, p] = jnp.dot(pw, pw, preferred_element_type=F32).astype(BF16)
                delta = jnp.dot(pw, rb, preferred_element_type=F32)
            rhs = r32_scr[b, p] + delta
            r32_scr[b, p] = rhs
            rb_scr[b, p] = rhs.astype(BF16)

    for b, p in problems:
        sl = slice(p * PAIR_W, (p + 1) * PAIR_W)
        u_s = rb_scr[b, p]
        v_s = v_scr[b, p]
        tl = tl_scr[b, p]
        if fused:
            uv = jnp.concatenate([u_s, v_s], axis=0)
            mm = jnp.concatenate([mrb_scr[b, p], mrk_scr[b, p]], axis=1)
            o_st = aro_scr[b, p] + jnp.dot(mm, uv, preferred_element_type=F32)
            upd = _bdot_tn(uv, tl)
        else:
            o_st = (aro_scr[b, p] + jnp.dot(mrb_scr[b, p], u_s, preferred_element_type=F32)
                    + jnp.dot(mrk_scr[b, p], v_s, preferred_element_type=F32))
            upd = _bdot_tn(u_s, tl[:c2]) + _bdot_tn(v_s, tl[c2:])
        o_scr[p, _seq_rows(b, cc, nseq), :] = o_st[:cc] + o_st[cc:]
        s_scr[b, p] = s_scr[b, p] * gend_scr[b][:, sl] + upd

    @pl.when(c == pl.num_programs(1) - 1)
    def _():
        for b, p in problems:
            s_pair = s_scr[b, p]
            s_out_ref[b, 2 * p] = s_pair[:RWKV_HEAD, :RWKV_HEAD]
            s_out_ref[b, 2 * p + 1] = s_pair[RWKV_HEAD:, RWKV_HEAD:]

    ones = ones_ref[...]
    inv_n = 1.0 / RWKV_HEAD
    o = jnp.concatenate([o_scr[j] for j in range(HEAD_PAIRS)], axis=1)
    mean = _head_sums(o, ones) * inv_n
    d = o - mean
    var = _head_sums(d * d, ones) * inv_n
    on = d * lax.rsqrt(var + LNX_EPS) * lnw_ref[...] + lnb_ref[...]
    bonus = _head_sums(r * k * rk_ref[...], ones) * v
    return (on + bonus) * g


def _post_consts(lw, layer):
    return [_Layer(lw[n], layer) for n in ("rwkv_lnx_w", "rwkv_lnx_b", "rwkv_r_k")] + [lw["pair_ones"]]


def _rwkv_group_kernel(*refs, chunk, steps, nseq, has_vfirst, n_consts):
    p_ref, shift0_ref = refs[:2]
    consts = refs[2:2 + n_consts]
    pos = 2 + n_consts
    vf_ref = refs[pos] if has_vfirst else None
    pos += int(has_vfirst)
    s0_ref = refs[pos]
    post = refs[pos + 1:pos + 5]
    pos += 5
    o_ref, v_out, shift_out, s_out_ref, carry = refs[pos:pos + 5]
    scratch = refs[pos + 5:]
    rows = steps * nseq

    @pl.when(pl.program_id(1) == 0)
    def _():
        carry[...] = shift0_ref[...]

    p = p_ref[...].reshape(rows, RWKV_COLS)
    p_prev, new_carry = _token_shift(p, carry[...], nseq)
    carry[...] = new_carry
    shift_out[...] = new_carry
    v_first = vf_ref[...].reshape(rows, RWKV_WIDTH) if has_vfirst else None
    vals = _prep_math(p, p_prev, v_first, consts)
    v_out[...] = vals[3].reshape(steps, nseq, RWKV_WIDTH)
    if steps < chunk:
        pad = jnp.zeros(((chunk - steps) * nseq, RWKV_WIDTH), F32)
        vals = [jnp.concatenate([x, pad], axis=0) for x in vals]
    o = _rec_phases(vals, s0_ref, post, s_out_ref, scratch, chunk=chunk, nseq=nseq)
    o_ref[...] = o[:rows].reshape(steps, nseq, RWKV_WIDTH)


def _rwkv_group(proj, shift0, v_first, state, lw, layer, steps, batch):
    nseq = SEQ_GROUP
    chunk = LONG_CHUNK if steps >= LONG_CHUNK else SHORT_CHUNK
    tile_steps = min(chunk, steps)
    has_vfirst = v_first is not None
    consts = _prep_consts(lw, layer, has_vfirst)
    post = _post_consts(lw, layer)
    rows3 = lambda a: a.reshape(steps, batch, a.shape[-1])
    row_spec = pl.BlockSpec((tile_steps, nseq, RWKV_WIDTH), lambda j, c: (c, j, 0))
    shift_spec = pl.BlockSpec((nseq, RWKV_COLS), lambda j, c: (j, 0))
    st_spec = pl.BlockSpec((None, nseq, RWKV_HEADS, RWKV_HEAD, RWKV_HEAD), lambda j, c: (layer, j, 0, 0, 0))
    args = ([rows3(proj), shift0] + [_array(a) for a in consts]
            + ([rows3(v_first)] if has_vfirst else []) + [state] + [_array(a) for a in post])
    state_arg = 2 + len(consts) + int(has_vfirst)
    in_specs = ([pl.BlockSpec((tile_steps, nseq, RWKV_COLS), lambda j, c: (c, j, 0)),
                 pl.BlockSpec((None, nseq, RWKV_COLS), lambda j, c: (layer, j, 0))]
                + [_resident_spec(a) for a in consts] + ([row_spec] if has_vfirst else [])
                + [st_spec] + [_resident_spec(a) for a in post])
    rows_shape = jax.ShapeDtypeStruct((steps, batch, RWKV_WIDTH), F32)
    o, v, shift, s_new = pl.pallas_call(
        functools.partial(_rwkv_group_kernel, chunk=chunk, steps=tile_steps, nseq=nseq,
                          has_vfirst=has_vfirst, n_consts=len(consts)),
        grid=(batch // nseq, steps // tile_steps),
        in_specs=in_specs,
        out_specs=[row_spec, row_spec, shift_spec, st_spec],
        out_shape=[rows_shape, rows_shape, jax.ShapeDtypeStruct((batch, RWKV_COLS), F32),
                   jax.ShapeDtypeStruct(state.shape, F32)],
        input_output_aliases={state_arg: 3},
        scratch_shapes=[pltpu.VMEM((nseq, RWKV_COLS), F32)] + _rec_scratch(chunk, nseq),
        compiler_params=_cparams("parallel", "arbitrary"),
        name="rwkv_group",
    )(*args)
    return o.reshape(steps * batch, RWKV_WIDTH), v.reshape(steps * batch, RWKV_WIDTH), shift, s_new


def _s5_kernel(u0_ref, u1_ref, x_ref, orw_ref, h0r_ref, h0i_ref, are_ref, aim_ref, ldt_ref, bre_ref,
               bim_ref, cre_ref, cim_ref, d_ref, wglu_ref, bglu_ref, wrw_ref, ws5_ref,
               o_ref, hr_out, hi_out, hr_c, hi_c, hre, him, bbre, bbim, *, batch):
    a_re = are_ref[...]
    a_im = aim_ref[...]
    dt = jnp.exp(ldt_ref[...])
    mag = jnp.exp(a_re * dt)
    ab_re = mag * jnp.cos(a_im * dt)
    ab_im = mag * jnp.sin(a_im * dt)

    @pl.when(pl.program_id(0) == 0)
    def _():
        hr_c[...] = h0r_ref[...]
        hi_c[...] = h0i_ref[...]
        den = a_re * a_re + a_im * a_im
        nr = ab_re - 1.0
        cf_re = (nr * a_re + ab_im * a_im) / den
        cf_im = (ab_im * a_re - nr * a_im) / den
        for hf in range(2):
            ls = slice(hf * S5_HALF_L, (hf + 1) * S5_HALF_L)
            b_re = bre_ref[hf]
            b_im = bim_ref[hf]
            bbre[hf] = (cf_re[:, ls] * b_re - cf_im[:, ls] * b_im).astype(BF16)
            bbim[hf] = (cf_re[:, ls] * b_im + cf_im[:, ls] * b_re).astype(BF16)

    us = (u0_ref[...], u1_ref[...])
    tm = us[0].shape[0]
    for hf in range(2):
        ls = slice(hf * S5_HALF_L, (hf + 1) * S5_HALF_L)
        ub = us[hf].astype(BF16)
        hre[:, ls] = jnp.dot(ub, bbre[hf], preferred_element_type=F32)
        him[:, ls] = jnp.dot(ub, bbim[hf], preferred_element_type=F32)

    n_steps = tm // batch
    if n_steps <= 8:
        hr = hr_c[...]
        hi = hi_c[...]
        for s in range(n_steps):
            rows = slice(s * batch, (s + 1) * batch)
            nhr = ab_re * hr - ab_im * hi + hre[rows, :]
            nhi = ab_re * hi + ab_im * hr + him[rows, :]
            hre[rows, :] = nhr
            him[rows, :] = nhi
            hr, hi = nhr, nhi
        hr_c[...] = hr
        hi_c[...] = hi
    else:
        lane_w = 512
        for lc in range(S5_LANES // lane_w):
            ls = slice(lc * lane_w, (lc + 1) * lane_w)
            abr = jnp.broadcast_to(ab_re[:, ls], (batch, lane_w))
            abi = jnp.broadcast_to(ab_im[:, ls], (batch, lane_w))

            def body(s, carry, ls=ls, abr=abr, abi=abi):
                hr, hi = carry
                rows = pl.ds(pl.multiple_of(s * batch, batch), batch)
                nhr = abr * hr - abi * hi + hre[rows, ls]
                nhi = abr * hi + abi * hr + him[rows, ls]
                hre[rows, ls] = nhr
                him[rows, ls] = nhi
                return nhr, nhi

            hr, hi = lax.fori_loop(0, n_steps, body, (hr_c[:, ls], hi_c[:, ls]), unroll=8)
            hr_c[:, ls] = hr
            hi_c[:, ls] = hi

    hr_out[...] = hr_c[...]
    hi_out[...] = hi_c[...]

    ys = []
    for hf in range(2):
        ls = slice(hf * S5_HALF_L, (hf + 1) * S5_HALF_L)
        cs = slice(hf * S5_HALF_W, (hf + 1) * S5_HALF_W)
        y = (jnp.dot(hre[:, ls].astype(BF16), cre_ref[hf], preferred_element_type=F32)
             - jnp.dot(him[:, ls].astype(BF16), cim_ref[hf], preferred_element_type=F32)
             + d_ref[:, cs] * us[hf])
        ys.append(jax.nn.gelu(y, approximate=True))
    y = jnp.concatenate(ys, axis=1)
    o_s5 = y * jax.nn.sigmoid(_bdot(y, wglu_ref[...]) + bglu_ref[...])
    o_ref[...] = (x_ref[...] + _bdot(orw_ref[...], wrw_ref[...]) + _bdot(o_s5, ws5_ref[...]))


def _s5_mix(proj, x, o_rw, h0r, h0i, lw, layer, batch):
    rows = proj.shape[0]
    tm = min(ROW_TILE, rows)
    u_blk = RWKV_COLS // S5_HALF_W
    consts = [_Layer(h0r, layer), _Layer(h0i, layer)] + [
        _Layer(lw[n], layer) for n in ("s5_a_re", "s5_a_im", "s5_log_dt", "s5_b_re", "s5_b_im", "s5_c_re",
                                       "s5_c_im", "s5_d", "s5_w_glu", "s5_b_glu", "w_out_rw", "w_out_s5")]
    st_shape = jax.ShapeDtypeStruct((batch, S5_LANES), F32)
    st_spec = pl.BlockSpec((batch, S5_LANES), lambda i: (0, 0))
    in_mats = pltpu.VMEM((2, S5_HALF_W, S5_HALF_L), BF16)
    return pl.pallas_call(
        functools.partial(_s5_kernel, batch=batch),
        grid=(rows // tm,),
        in_specs=[pl.BlockSpec((tm, S5_HALF_W), lambda i: (i, u_blk)),
                  pl.BlockSpec((tm, S5_HALF_W), lambda i: (i, u_blk + 1)),
                  pl.BlockSpec((tm, D_MODEL), lambda i: (i, 0)),
                  pl.BlockSpec((tm, RWKV_WIDTH), lambda i: (i, 0))]
                 + [_resident_spec(a) for a in consts],
        out_specs=[pl.BlockSpec((tm, D_MODEL), lambda i: (i, 0)), st_spec, st_spec],
        out_shape=[jax.ShapeDtypeStruct((rows, D_MODEL), F32), st_shape, st_shape],
        scratch_shapes=[pltpu.VMEM((batch, S5_LANES), F32), pltpu.VMEM((batch, S5_LANES), F32),
                        pltpu.VMEM((tm, S5_LANES), F32), pltpu.VMEM((tm, S5_LANES), F32),
                        in_mats, in_mats],
        compiler_params=_cparams("arbitrary"),
        name="s5_mix",
    )(proj, proj, x, o_rw, *[_array(a) for a in consts])


def _softmax_rows(s):
    e = jnp.exp(s - jnp.max(s, axis=-1, keepdims=True))
    return e / jnp.sum(e, axis=-1, keepdims=True)


CACHE_ROWS = N_MEM * X_HEADS * (X_HEAD_DIM // LANES)
CACHE_GROUP = X_HEADS * (X_HEAD_DIM // LANES)
Q_ROWS = 8


def _cache_view(cache):
    l, b = cache.shape[:2]
    halves = X_HEAD_DIM // LANES
    c = cache.reshape(l, b, N_MEM, X_HEADS, halves, LANES)
    return jnp.swapaxes(c, 3, 4).reshape(l, b, CACHE_ROWS, LANES)


def _cache_unview(z, batch):
    halves = X_HEAD_DIM // LANES
    c = z.reshape(z.shape[0], batch, N_MEM, halves, X_HEADS, LANES)
    return jnp.swapaxes(c, 3, 4).reshape(z.shape[0], batch, N_MEM, X_HEADS, X_HEAD_DIM)


def _attn_cache_kernel(q_ref, k_ref, v_ref, o_ref, q_scr, o_scr, *, steps, nseq):
    halves = X_HEAD_DIM // LANES
    scale = X_HEAD_DIM ** -0.5
    n_tiles = D_MODEL // LANES
    rows = steps * nseq
    q = q_ref[...].reshape(rows, D_MODEL)
    pad = jnp.zeros(((Q_ROWS - steps) * nseq, D_MODEL), F32)
    _stage(q_scr, jnp.concatenate([q, pad], axis=0))
    col = lax.broadcasted_iota(jnp.int32, (Q_ROWS, CACHE_ROWS), 1) % CACHE_GROUP
    for b in range(nseq):
        seq = _seq_rows(b, Q_ROWS, nseq)
        qx = jnp.concatenate([q_scr[n, seq, :] for n in range(n_tiles)], axis=0)
        e = _bdot_nt(qx, k_ref[b])
        probs = []
        for h in range(X_HEADS):
            base = h * halves * Q_ROWS
            valid = col == h
            s = jnp.where(valid, e[base:base + Q_ROWS], 0.0)
            for j in range(1, halves):
                part = jnp.where(col == j * X_HEADS + h,
                                 e[base + j * Q_ROWS:base + (j + 1) * Q_ROWS], 0.0)
                s = s + pltpu.roll(part, shift=CACHE_ROWS - j * X_HEADS, axis=1)
            pr = _softmax_rows(jnp.where(valid, s * scale, -1e30))
            probs.append(pr)
            for j in range(1, halves):
                probs.append(pltpu.roll(pr, shift=j * X_HEADS, axis=1))
        ox = _bdot(jnp.concatenate(probs, axis=0), v_ref[b])
        for n in range(n_tiles):
            o_scr[n, seq, :] = ox[n * Q_ROWS:(n + 1) * Q_ROWS]
    o = jnp.concatenate([o_scr[n, :rows, :] for n in range(n_tiles)], axis=1)
    o_ref[...] = o.reshape(steps, nseq, D_MODEL)


def _attn_cache(q, cache_k, cache_v, layer, steps, batch):
    nseq = SEQ_GROUP
    q_spec = pl.BlockSpec((steps, nseq, D_MODEL), lambda j: (0, j, 0))
    m_spec = pl.BlockSpec((None, nseq, CACHE_ROWS, LANES), lambda j: (layer, j, 0, 0))
    stage = pltpu.VMEM((D_MODEL // LANES, Q_ROWS * nseq, LANES), F32)
    out = pl.pallas_call(
        functools.partial(_attn_cache_kernel, steps=steps, nseq=nseq),
        grid=(batch // nseq,),
        in_specs=[q_spec, m_spec, m_spec],
        out_specs=q_spec,
        out_shape=jax.ShapeDtypeStruct((steps, batch, D_MODEL), F32),
        scratch_shapes=[stage, stage],
        compiler_params=_cparams("parallel"),
        name="mem_attn",
    )(q.reshape(steps, batch, D_MODEL), cache_k, cache_v)
    return out.reshape(steps * batch, D_MODEL)


def _cross_long_kernel(x_ref, gain_ref, wq_ref, k_ref, v_ref, wo_ref, o_ref,
                       q_scr, att_scr, s_scr, p_scr, *, batch):
    x = x_ref[...]
    steps = x.shape[0] // batch
    scale = X_HEAD_DIM ** -0.5
    tiles_per_head = X_HEAD_DIM // LANES
    n_tiles = D_MODEL // LANES
    blocks = [(b, h) for b in range(batch) for h in range(X_HEADS)]
    _stage(q_scr, jnp.dot(_rms(x, gain_ref[...]).astype(BF16), wq_ref[...],
                          preferred_element_type=F32))
    for i, (b, h) in enumerate(blocks):
        rows = _seq_rows(b, steps, batch)
        tiles = range(h * tiles_per_head, (h + 1) * tiles_per_head)
        q = jnp.concatenate([q_scr[j, rows, :] for j in tiles], axis=1)
        s_scr[i] = _bdot_nt(q, k_ref[b, :, h * X_HEAD_DIM:(h + 1) * X_HEAD_DIM]) * scale
    p_scr[...] = _softmax_rows(s_scr[...]).astype(BF16)
    for i, (b, h) in enumerate(blocks):
        rows = _seq_rows(b, steps, batch)
        o = jnp.dot(p_scr[i], v_ref[b, :, h * X_HEAD_DIM:(h + 1) * X_HEAD_DIM],
                    preferred_element_type=F32)
        for t in range(tiles_per_head):
            att_scr[h * tiles_per_head + t, rows, :] = o[:, t * LANES:(t + 1) * LANES]
    att = jnp.concatenate([att_scr[j] for j in range(n_tiles)], axis=1)
    o_ref[...] = x + jnp.dot(att.astype(BF16), wo_ref[...], preferred_element_type=F32)


def _cross_long(x, mem_k, mem_v, lw, layer, batch):
    rows = x.shape[0]
    tm = min(ATTN_ROW_TILE, rows)
    steps = tm // batch
    consts = [_Layer(lw["norm_cross"], layer), _Layer(lw["w_cq"], layer), _Layer(mem_k, layer),
              _Layer(mem_v, layer), _Layer(lw["w_co"], layer)]
    return pl.pallas_call(
        functools.partial(_cross_long_kernel, batch=batch),
        grid=(rows // tm,),
        in_specs=[pl.BlockSpec((tm, D_MODEL), lambda i: (i, 0))] + [_resident_spec(a) for a in consts],
        out_specs=pl.BlockSpec((tm, D_MODEL), lambda i: (i, 0)),
        out_shape=jax.ShapeDtypeStruct((rows, D_MODEL), F32),
        scratch_shapes=[pltpu.VMEM((D_MODEL // LANES, tm, LANES), F32),
                        pltpu.VMEM((D_MODEL // LANES, tm, LANES), F32),
                        pltpu.VMEM((batch * X_HEADS, steps, N_MEM), F32),
                        pltpu.VMEM((batch * X_HEADS, steps, N_MEM), BF16)],
        compiler_params=_cparams("parallel"),
        name="cross_attn",
    )(x, *[_array(a) for a in consts])


def _ffn_kernel(x_ref, buf0_ref, gain_ref, wg_ref, wu_ref, cw_ref, cb_ref, wd_ref,
                o_ref, buf_out, carry, *, batch):
    @pl.when(pl.program_id(0) == 0)
    def _():
        carry[...] = buf0_ref[...]

    x = x_ref[...]
    tm = x.shape[0]
    h = _rms(x, gain_ref[...]).astype(BF16)
    gt = jnp.dot(h, wg_ref[...], preferred_element_type=F32)
    up = jnp.dot(h, wu_ref[...], preferred_element_type=F32)
    padded = jnp.concatenate([carry[...], gt], axis=0)
    conv = cb_ref[...]
    for i in range(CONV_W):
        conv = conv + cw_ref[i:i + 1, :] * padded[i * batch:i * batch + tm]
    new_carry = padded[tm:]
    carry[...] = new_carry
    buf_out[...] = new_carry
    act = jax.nn.silu(conv) * up
    o_ref[...] = x + jnp.dot(act.astype(BF16), wd_ref[...], preferred_element_type=F32)


def _ffn(x, buf0, lw, layer, batch):
    rows = x.shape[0]
    tm = min(FFN_ROW_TILE, rows)
    tm = max(tm, (CONV_W - 1) * batch)
    consts = [_Layer(buf0, layer)] + [_Layer(lw[n], layer) for n in (
        "norm_ffn", "w_gate", "w_up", "ffn_conv_w", "ffn_conv_b", "w_down")]
    nbuf = (CONV_W - 1) * batch
    return pl.pallas_call(
        functools.partial(_ffn_kernel, batch=batch),
        grid=(rows // tm,),
        in_specs=[pl.BlockSpec((tm, D_MODEL), lambda i: (i, 0))] + [_resident_spec(a) for a in consts],
        out_specs=[pl.BlockSpec((tm, D_MODEL), lambda i: (i, 0)),
                   pl.BlockSpec((nbuf, D_FF), lambda i: (0, 0))],
        out_shape=[jax.ShapeDtypeStruct((rows, D_MODEL), F32), jax.ShapeDtypeStruct((nbuf, D_FF), F32)],
        scratch_shapes=[pltpu.VMEM((nbuf, D_FF), F32)],
        compiler_params=_cparams("arbitrary"),
        name="conv_ffn",
    )(x, *[_array(a) for a in consts])


def _s5_in_blockdiag(b):
    l = b.shape[0]
    b = b.reshape(l, 2, S5_GROUPS // 2, S5_STATE, S5_GROUP)
    eye = jnp.eye(S5_GROUPS // 2, dtype=b.dtype)
    m = jnp.einsum('lfgph,gk->lfghkp', b, eye)
    return m.reshape(l, 2, S5_HALF_W, S5_HALF_L)


def _s5_out_blockdiag(c):
    l = c.shape[0]
    c = c.reshape(l, 2, S5_GROUPS // 2, S5_GROUP, S5_STATE)
    eye = jnp.eye(S5_GROUPS // 2, dtype=c.dtype)
    m = jnp.einsum('lfgnp,gk->lfgpkn', c, eye)
    return m.reshape(l, 2, S5_HALF_L, S5_HALF_W).astype(BF16)


def _prep_weights(p):
    l = DEPTH
    row = lambda a: a.reshape(a.shape[0], 1, -1)
    lw = {}
    for name in ("norm_mix", "shift_mu", "rwkv_w0", "rwkv_a0", "rwkv_v0", "rwkv_k_k", "rwkv_k_a",
                 "rwkv_r_k", "rwkv_lnx_w", "rwkv_lnx_b", "s5_d", "s5_b_glu", "norm_cross", "norm_ffn",
                 "ffn_conv_b", "s5_a_re", "s5_a_im"):
        lw[name] = row(p[name])
    lw["s5_log_dt"] = row(jnp.repeat(p["s5_log_dt"], S5_STATE, axis=-1))
    lw["norm_final"] = p["norm_final"].reshape(1, -1)
    lw["ffn_conv_w"] = p["ffn_conv_w"]
    for name in ("w_in", "rwkv_w_g2", "s5_w_glu", "w_cq", "w_ck", "w_cv", "w_co", "w_gate", "w_up",
                 "w_down"):
        lw[name] = p[name].astype(BF16)
    w_out = p["w_out"].astype(BF16)
    lw["w_out_rw"] = w_out[:, :RWKV_WIDTH]
    lw["w_out_s5"] = w_out[:, RWKV_WIDTH:]
    z64 = jnp.zeros((l, LORA_PAD - 64, RWKV_WIDTH), F32)
    lw["rwkv_w_w2p"] = jnp.concatenate([p["rwkv_w_w2"], z64], axis=1).astype(BF16)
    lw["rwkv_w_a2p"] = jnp.concatenate([z64, p["rwkv_w_a2"]], axis=1).astype(BF16)
    v_lora = p["rwkv_w_v1"].shape[-1]
    lw["rwkv_w_v1p"] = jnp.pad(p["rwkv_w_v1"], ((0, 0), (0, 0), (0, LORA_PAD - v_lora))).astype(BF16)
    lw["rwkv_w_v2p"] = jnp.pad(p["rwkv_w_v2"], ((0, 0), (0, LORA_PAD - v_lora), (0, 0))).astype(BF16)
    head = jnp.arange(PAIR_W) // RWKV_HEAD
    lw["pair_ones"] = (head[:, None] == head[None, :]).astype(BF16)
    lw["s5_b_re"] = _s5_in_blockdiag(p["s5_b_re"])
    lw["s5_b_im"] = _s5_in_blockdiag(p["s5_b_im"])
    lw["s5_c_re"] = _s5_out_blockdiag(p["s5_c_re"])
    lw["s5_c_im"] = _s5_out_blockdiag(p["s5_c_im"])
    return lw


def _run_trunk(x, mem_k, mem_v, st_rwkv, st_shift, st_re, st_im, st_conv, lw, steps, batch):
    long_seq = steps >= LONG_SEQ
    if long_seq:
        mem_k = mem_k.astype(BF16)
        mem_v = mem_v.astype(BF16)
    else:
        mem_k = _cache_view(mem_k)
        mem_v = _cache_view(mem_v)
    v_first = None
    new_shift, new_re, new_im, new_conv = [], [], [], []
    for l in range(DEPTH):
        proj = _rowmm([x], [_Layer(lw["w_in"], l)], gain=_Layer(lw["norm_mix"], l), name="in_proj",
                      row_tile=IN_PROJ_ROW_TILE)
        o_rw, v_l, sh, st_rwkv = _rwkv_group(proj, st_shift, v_first, st_rwkv, lw, l, steps, batch)
        if l == 0:
            v_first = v_l
        x, hr, hi = _s5_mix(proj, x, o_rw, st_re, st_im, lw, l, batch)
        if long_seq:
            x = _cross_long(x, mem_k, mem_v, lw, l, batch)
        else:
            q = _rowmm([x], [_Layer(lw["w_cq"], l)], gain=_Layer(lw["norm_cross"], l), name="cross_q")
            att = _attn_cache(q, mem_k, mem_v, l, steps, batch)
            x = _rowmm([att], [_Layer(lw["w_co"], l)], resid=x, name="cross_o")
        x, cb = _ffn(x, st_conv, lw, l, batch)
        new_shift.append(sh)
        new_re.append(hr)
        new_im.append(hi)
        new_conv.append(cb)
    return (x, st_rwkv, jnp.stack(new_shift), jnp.stack(new_re), jnp.stack(new_im),
            jnp.stack(new_conv))


def _group(x, mem_k, mem_v, st_rwkv, st_shift, st_re, st_im, st_conv, lw):
    b, t, _ = x.shape
    long_seq = t >= LONG_SEQ
    if long_seq:
        xt = _to_time_major(x)
    else:
        xt = jnp.swapaxes(x, 0, 1).reshape(t * b, D_MODEL)
    conv_t = jnp.swapaxes(st_conv, 1, 2).reshape(DEPTH, (CONV_W - 1) * b, D_FF)
    y, rw, sh, re, im, cv = _run_trunk(
        xt, mem_k, mem_v, st_rwkv, st_shift, st_re.reshape(DEPTH, b, S5_LANES),
        st_im.reshape(DEPTH, b, S5_LANES), conv_t, lw, t, b)
    if long_seq:
        y = _final_norm_long(y, lw["norm_final"], b)
    else:
        y = jnp.swapaxes(_rownorm(y, lw["norm_final"]).reshape(t, b, D_MODEL), 0, 1)
    cv = jnp.swapaxes(cv.reshape(DEPTH, CONV_W - 1, b, D_FF), 1, 2)
    return (y, rw, sh, re.reshape(DEPTH, b, S5_GROUPS, S5_STATE),
            im.reshape(DEPTH, b, S5_GROUPS, S5_STATE), cv)


def kernel(x_prompt, x_sample, mem_prompt, state_rwkv, state_shift, state_s5_re, state_s5_im, state_ffn_conv, cache_mem_k, cache_mem_v, norm_mix, w_in, shift_mu, rwkv_w0, rwkv_w_w2, rwkv_a0, rwkv_w_a2, rwkv_v0, rwkv_w_v1, rwkv_w_v2, rwkv_w_g2, rwkv_k_k, rwkv_k_a, rwkv_r_k, rwkv_lnx_w, rwkv_lnx_b, s5_a_re, s5_a_im, s5_log_dt, s5_b_re, s5_b_im, s5_c_re, s5_c_im, s5_d, s5_w_glu, s5_b_glu, w_out, norm_cross, w_cq, w_ck, w_cv, w_co, norm_ffn, w_gate, w_up, ffn_conv_w, ffn_conv_b, w_down, norm_final):
    lw = _prep_weights(dict(
        norm_mix=norm_mix, w_in=w_in, shift_mu=shift_mu, rwkv_w0=rwkv_w0, rwkv_w_w2=rwkv_w_w2,
        rwkv_a0=rwkv_a0, rwkv_w_a2=rwkv_w_a2, rwkv_v0=rwkv_v0, rwkv_w_v1=rwkv_w_v1,
        rwkv_w_v2=rwkv_w_v2, rwkv_w_g2=rwkv_w_g2, rwkv_k_k=rwkv_k_k, rwkv_k_a=rwkv_k_a,
        rwkv_r_k=rwkv_r_k, rwkv_lnx_w=rwkv_lnx_w, rwkv_lnx_b=rwkv_lnx_b, s5_a_re=s5_a_re,
        s5_a_im=s5_a_im, s5_log_dt=s5_log_dt, s5_b_re=s5_b_re, s5_b_im=s5_b_im, s5_c_re=s5_c_re,
        s5_c_im=s5_c_im, s5_d=s5_d, s5_w_glu=s5_w_glu, s5_b_glu=s5_b_glu, w_out=w_out,
        norm_cross=norm_cross, w_cq=w_cq, w_ck=w_ck, w_cv=w_cv, w_co=w_co, norm_ffn=norm_ffn,
        w_gate=w_gate, w_up=w_up, ffn_conv_w=ffn_conv_w, ffn_conv_b=ffn_conv_b, w_down=w_down,
        norm_final=norm_final))
    bp = x_prompt.shape[0]
    mem_rows = mem_prompt.reshape(bp * N_MEM, D_MODEL)
    p_mem_k, k_rows = _mem_proj(mem_rows, lw["w_ck"])
    p_mem_v, v_rows = _mem_proj(mem_rows, lw["w_cv"])
    z_rw = jnp.zeros((DEPTH, bp, RWKV_HEADS, RWKV_HEAD, RWKV_HEAD), F32)
    z_shift = jnp.zeros((DEPTH, bp, RWKV_COLS), F32)
    z_s5 = jnp.zeros((DEPTH, bp, S5_GROUPS, S5_STATE), F32)
    z_conv = jnp.zeros((DEPTH, bp, CONV_W - 1, D_FF), F32)
    y_prompt, p_rwkv, p_shift, p_re, p_im, p_conv = _group(
        x_prompt, k_rows.reshape(DEPTH, bp, N_MEM, D_MODEL), v_rows.reshape(DEPTH, bp, N_MEM, D_MODEL),
        z_rw, z_shift, z_s5, z_s5, z_conv, lw)
    y_sample, s_rwkv, s_shift, s_re, s_im, s_conv = _group(
        x_sample, cache_mem_k, cache_mem_v, state_rwkv, state_shift, state_s5_re, state_s5_im,
        state_ffn_conv, lw)
    return (y_prompt, y_sample, p_rwkv, p_shift, p_re, p_im, p_conv,
            _cache_unview(p_mem_k, bp), _cache_unview(p_mem_v, bp),
            s_rwkv, s_shift, s_re, s_im, s_conv)
```

```python
import functools
import math

import jax
import jax.numpy as jnp
from jax import lax
from jax.experimental import pallas as pl
from jax.experimental.pallas import tpu as pltpu

F32 = jnp.float32
BF16 = jnp.bfloat16

D_MODEL = 1024
DEPTH = 4
RWKV_WIDTH = 512
RWKV_HEAD = 64
RWKV_HEADS = 8
HEAD_PAIRS = RWKV_HEADS // 2
LANES = 128
PAIR_W = 2 * RWKV_HEAD
LORA_PAD = 128
RWKV_COLS = 3 * RWKV_WIDTH + 64 + 64 + 128
S5_WIDTH = 512
S5_GROUP = 16
S5_GROUPS = 32
S5_STATE = 64
S5_LANES = S5_GROUPS * S5_STATE
S5_HALF_W = S5_WIDTH // 2
S5_HALF_L = S5_LANES // 2
IN_COLS = RWKV_COLS + S5_WIDTH
N_MEM = 256
X_HEADS = 4
X_HEAD_DIM = 256
D_FF = 2816
CONV_W = 3
RMS_EPS = 1e-6
LNX_EPS = 64e-5

ROW_TILE = 512
FFN_ROW_TILE = 512
IN_PROJ_ROW_TILE = 1024
LAYOUT_ROW_TILE = 1024
ATTN_ROW_TILE = 1024
LONG_SEQ = 64
LONG_CHUNK = 64
SHORT_CHUNK = 8
SEQ_GROUP = 8
VMEM_LIMIT = 56 * 1024 * 1024


def _cparams(*sem):
    return pltpu.CompilerParams(dimension_semantics=sem, vmem_limit_bytes=VMEM_LIMIT)


def _const_spec(shape):
    nd = len(shape)
    return pl.BlockSpec(shape, lambda *_: (0,) * nd, pipeline_mode=pl.Buffered(1))


class _Layer:
    def __init__(self, stacked, layer):
        self.array = stacked
        self.layer = layer
        self.shape = stacked.shape[1:]


def _resident_spec(op):
    if isinstance(op, _Layer):
        nd = len(op.shape)
        layer = op.layer
        return pl.BlockSpec((None,) + tuple(op.shape), lambda *_: (layer,) + (0,) * nd,
                            pipeline_mode=pl.Buffered(1))
    return _const_spec(op.shape)


def _array(op):
    return op.array if isinstance(op, _Layer) else op


def _bdot(a, b):
    return jnp.dot(a.astype(BF16), b.astype(BF16), preferred_element_type=F32)


def _bdot_nt(a, b):
    return lax.dot_general(a.astype(BF16), b.astype(BF16), (((1,), (1,)), ((), ())),
                           preferred_element_type=F32)


def _bdot_tn(a, b):
    return lax.dot_general(a.astype(BF16), b.astype(BF16), (((0,), (0,)), ((), ())),
                           preferred_element_type=F32)


def _head_sums(x, pair_ones):
    outs = []
    for sl in _lane_tiles(x.shape[1]):
        hi = x[:, sl].astype(BF16)
        lo = (x[:, sl] - hi.astype(F32)).astype(BF16)
        outs.append(jnp.dot(hi, pair_ones, preferred_element_type=F32)
                    + jnp.dot(lo, pair_ones, preferred_element_type=F32))
    return jnp.concatenate(outs, axis=1)


def _rms(x, gain):
    return x * lax.rsqrt(jnp.mean(x * x, axis=-1, keepdims=True) + RMS_EPS) * gain


def _seq_rows(seq, steps, batch):
    return pl.ds(seq, steps, stride=batch)


def _lane_tiles(width):
    return [slice(j * LANES, (j + 1) * LANES) for j in range(width // LANES)]


def _stage(scr, x):
    for j, sl in enumerate(_lane_tiles(x.shape[1])):
        scr[j] = x[:, sl]


def _to_time_major_kernel(x_ref, o_ref, scr, *, batch, steps):
    tiles = _lane_tiles(x_ref.shape[2])
    for b in range(batch):
        for j, sl in enumerate(tiles):
            scr[j, _seq_rows(b, steps, batch), :] = x_ref[b, :, sl]
    for j, sl in enumerate(tiles):
        o_ref[:, sl] = scr[j]


def _to_time_major(x):
    b, t, d = x.shape
    tt = LAYOUT_ROW_TILE // b
    return pl.pallas_call(
        functools.partial(_to_time_major_kernel, batch=b, steps=tt),
        grid=(t // tt,),
        in_specs=[pl.BlockSpec((b, tt, d), lambda i: (0, i, 0))],
        out_specs=pl.BlockSpec((tt * b, d), lambda i: (i, 0)),
        out_shape=jax.ShapeDtypeStruct((t * b, d), F32),
        scratch_shapes=[pltpu.VMEM((d // LANES, tt * b, LANES), F32)],
        compiler_params=_cparams("parallel"),
        name="to_time_major",
    )(x)


def _final_norm_long_kernel(x_ref, g_ref, o_ref, scr, *, batch, steps):
    _stage(scr, _rms(x_ref[...], g_ref[...]))
    for b in range(batch):
        for j, sl in enumerate(_lane_tiles(x_ref.shape[1])):
            o_ref[b, :, sl] = scr[j, _seq_rows(b, steps, batch), :]


def _final_norm_long(x, gain, batch):
    rows, d = x.shape
    t = rows // batch
    tt = LAYOUT_ROW_TILE // batch
    return pl.pallas_call(
        functools.partial(_final_norm_long_kernel, batch=batch, steps=tt),
        grid=(t // tt,),
        in_specs=[pl.BlockSpec((tt * batch, d), lambda i: (i, 0)), _const_spec(gain.shape)],
        out_specs=pl.BlockSpec((batch, tt, d), lambda i: (0, i, 0)),
        out_shape=jax.ShapeDtypeStruct((batch, t, d), F32),
        scratch_shapes=[pltpu.VMEM((d // LANES, tt * batch, LANES), F32)],
        compiler_params=_cparams("parallel"),
        name="final_norm",
    )(x, gain)


def _rowmm_kernel(*refs, n_x, has_gain, has_resid):
    xs = refs[:n_x]
    ws = refs[n_x:2 * n_x]
    pos = 2 * n_x
    gain = refs[pos] if has_gain else None
    pos += int(has_gain)
    resid = refs[pos] if has_resid else None
    pos += int(has_resid)
    o_ref = refs[pos]
    acc = None
    for x_ref, w_ref in zip(xs, ws):
        x = x_ref[...]
        if has_gain:
            x = _rms(x, gain[...])
        d = jnp.dot(x.astype(BF16), w_ref[...], preferred_element_type=F32)
        acc = d if acc is None else acc + d
    if has_resid:
        acc = acc + resid[...]
    o_ref[...] = acc


def _rowmm(xs, ws, gain=None, resid=None, name="rowmm", row_tile=ROW_TILE):
    rows = xs[0].shape[0]
    n_out = ws[0].shape[1]
    tm = min(row_tile, rows)
    in_specs = [pl.BlockSpec((tm, x.shape[1]), lambda i: (i, 0)) for x in xs]
    in_specs += [_resident_spec(w) for w in ws]
    args = list(xs) + [_array(w) for w in ws]
    if gain is not None:
        in_specs.append(_resident_spec(gain))
        args.append(_array(gain))
    if resid is not None:
        in_specs.append(pl.BlockSpec((tm, n_out), lambda i: (i, 0)))
        args.append(resid)
    return pl.pallas_call(
        functools.partial(_rowmm_kernel, n_x=len(xs), has_gain=gain is not None,
                          has_resid=resid is not None),
        grid=(rows // tm,),
        in_specs=in_specs,
        out_specs=pl.BlockSpec((tm, n_out), lambda i: (i, 0)),
        out_shape=jax.ShapeDtypeStruct((rows, n_out), F32),
        compiler_params=_cparams("parallel"),
        name=name,
    )(*args)


def _mem_proj_kernel(x_ref, w_ref, z_ref, rows_ref):
    y = jnp.dot(x_ref[...].astype(BF16), w_ref[...], preferred_element_type=F32)
    rows_ref[...] = y.astype(BF16)
    halves = X_HEAD_DIM // LANES
    group = X_HEADS * halves
    for n, sl in enumerate(_lane_tiles(D_MODEL)):
        head, half = divmod(n, halves)
        z_ref[pl.ds(half * X_HEADS + head, y.shape[0], stride=group), :] = y[:, sl]


def _mem_proj(x, w):
    rows, k = x.shape
    layers, _, n_out = w.shape
    tm = min(ROW_TILE, rows)
    group = n_out // LANES
    return pl.pallas_call(
        _mem_proj_kernel,
        grid=(layers, rows // tm),
        in_specs=[pl.BlockSpec((tm, k), lambda l, i: (i, 0)),
                  pl.BlockSpec((None, k, n_out), lambda l, i: (l, 0, 0))],
        out_specs=[pl.BlockSpec((None, tm * group, LANES), lambda l, i: (l, i, 0)),
                   pl.BlockSpec((None, tm, n_out), lambda l, i: (l, i, 0))],
        out_shape=[jax.ShapeDtypeStruct((layers, rows * group, LANES), F32),
                   jax.ShapeDtypeStruct((layers, rows, n_out), BF16)],
        compiler_params=_cparams("parallel", "parallel"),
        name="mem_proj",
    )(x, w)


def _rownorm_kernel(x_ref, g_ref, o_ref):
    o_ref[...] = _rms(x_ref[...], g_ref[...])


def _rownorm(x, gain):
    rows, d = x.shape
    tm = min(ROW_TILE, rows)
    return pl.pallas_call(
        _rownorm_kernel,
        grid=(rows // tm,),
        in_specs=[pl.BlockSpec((tm, d), lambda i: (i, 0)), _const_spec(gain.shape)],
        out_specs=pl.BlockSpec((tm, d), lambda i: (i, 0)),
        out_shape=jax.ShapeDtypeStruct((rows, d), F32),
        compiler_params=_cparams("parallel"),
        name="final_norm",
    )(x, gain)


def _softplus(z):
    return jnp.maximum(z, 0.0) + jnp.log1p(jnp.exp(-jnp.abs(z)))


def _prep_consts(lw, layer, has_vfirst):
    names = ["shift_mu", "rwkv_w0", "rwkv_w_w2p", "rwkv_a0", "rwkv_w_a2p", "rwkv_w_g2", "rwkv_k_k",
             "rwkv_k_a"]
    consts = [_Layer(lw[n], layer) for n in names] + [lw["pair_ones"]]
    if has_vfirst:
        consts += [_Layer(lw[n], layer - 1) for n in ("rwkv_v0", "rwkv_w_v1p", "rwkv_w_v2p")]
    return consts


def _token_shift(p, carry, batch):
    tm = p.shape[0]
    prev = jnp.concatenate([carry, p[:tm - batch]], axis=0) if tm > batch else carry
    return prev, p[tm - batch:]


def _prep_math(p, p_prev, v_first, consts):
    mu, w0, ww2, a0, wa2, wg2, k_k, k_a, pair_ones = [c[...] for c in consts[:9]]
    q = p + (p_prev - p) * mu
    rw = RWKV_WIDTH
    r = q[:, 0:rw]
    k = q[:, rw:2 * rw]
    v = q[:, 2 * rw:3 * rw]
    x_wa = q[:, 3 * rw:3 * rw + LORA_PAD]
    x_g = q[:, 3 * rw + LORA_PAD:3 * rw + 2 * LORA_PAD]
    w = -_softplus(-(w0 + _bdot(jnp.tanh(x_wa), ww2))) - 0.5
    log_decay = -jnp.exp(w)
    a = jax.nn.sigmoid(a0 + _bdot(x_wa, wa2))
    g = _bdot(jax.nn.sigmoid(x_g), wg2)
    if v_first is not None:
        v0, wv1, wv2 = [c[...] for c in consts[9:12]]
        mix = jax.nn.sigmoid(v0 + _bdot(_bdot(v, wv1), wv2))
        v = v + (v_first - v) * mix
    kk = k * k_k
    kk = kk * lax.rsqrt(jnp.maximum(_head_sums(kk * kk, pair_ones), 1e-24))
    k = k * (1.0 + (a - 1.0) * k_a)
    return r, log_decay, k, v, kk, a, g


def _rec_scratch(chunk, nseq):
    c2 = 2 * chunk
    per = (nseq, HEAD_PAIRS)
    return [
        pltpu.VMEM(per + (PAIR_W, PAIR_W), F32),
        pltpu.VMEM(per + (2 * c2, PAIR_W), BF16),
        pltpu.VMEM(per + (2 * c2, PAIR_W), BF16),
        pltpu.VMEM(per + (c2, PAIR_W), BF16),
        pltpu.VMEM(per + (2 * c2, PAIR_W), BF16),
        pltpu.VMEM(per + (c2, c2), BF16),
        pltpu.VMEM(per + (c2, c2), BF16),
        pltpu.VMEM(per + (c2, c2), BF16),
        pltpu.VMEM(per + (c2, c2), BF16),
        pltpu.VMEM(per + (c2, PAIR_W), BF16),
        pltpu.VMEM(per + (c2, PAIR_W), F32),
        pltpu.VMEM(per + (c2, PAIR_W), F32),
        pltpu.VMEM((HEAD_PAIRS, chunk * nseq, PAIR_W), F32),
        pltpu.VMEM((nseq, 1, RWKV_WIDTH), F32),
        pltpu.VMEM((7, HEAD_PAIRS, chunk * nseq, PAIR_W), F32),
    ]


def _rec_phases(vals, s0_ref, post_consts, s_out_ref, scratch, *, chunk, nseq):
    (s_scr, ar_scr, bk_scr, v_scr, tl_scr, p_scr, lak_scr, mrb_scr, mrk_scr, rb_scr, r32_scr,
     aro_scr, o_scr, gend_scr, stage_scr) = scratch
    r, lw, k, v, kk, a, g = vals
    lnw_ref, lnb_ref, rk_ref, ones_ref = post_consts
    c = pl.program_id(1)
    cc = chunk
    c2 = 2 * cc
    fused = c2 % 128 == 0
    problems = [(b, p) for b in range(nseq) for p in range(HEAD_PAIRS)]

    @pl.when(c == 0)
    def _():
        zero = jnp.zeros((RWKV_HEAD, RWKV_HEAD), F32)
        for b, p in problems:
            top = jnp.concatenate([s0_ref[b, 2 * p], zero], axis=1)
            bottom = jnp.concatenate([zero, s0_ref[b, 2 * p + 1]], axis=1)
            s_scr[b, p] = jnp.concatenate([top, bottom], axis=0)

    lane = lax.broadcasted_iota(jnp.int32, (1, PAIR_W), 1)
    first = lane < RWKV_HEAD
    row2 = lax.broadcasted_iota(jnp.int32, (c2, c2), 0)
    col2 = lax.broadcasted_iota(jnp.int32, (c2, c2), 1)
    same = (row2 >= cc) == (col2 >= cc)
    rr = jnp.where(row2 >= cc, row2 - cc, row2)
    cl = jnp.where(col2 >= cc, col2 - cc, col2)
    strict = same & (cl < rr)
    incl = same & (cl <= rr)
    n_factors = max(1, math.ceil(math.log2(cc)))

    def stack(xs):
        return jnp.concatenate([jnp.where(first, xs, 0.0), jnp.where(first, 0.0, xs)],
                               axis=0).astype(BF16)

    cum = lw
    shift = nseq
    while shift < cc * nseq:
        cum = cum + jnp.concatenate([jnp.zeros((shift, RWKV_WIDTH), F32), cum[:-shift]], axis=0)
        shift *= 2
    cum_end = cum[(cc - 1) * nseq:]
    for b in range(nseq):
        gend_scr[b] = jnp.exp(cum_end[b:b + 1])
    cum_end = jnp.broadcast_to(cum_end[None], (cc, nseq, RWKV_WIDTH)).reshape(cc * nseq, RWKV_WIDTH)
    g_inv = jnp.exp(-cum)
    g_tail = jnp.exp(cum_end - cum)
    kka = kk * a
    operands = [-kk * jnp.exp(cum - lw), r * jnp.exp(cum),
                kka * g_inv, k * g_inv,
                kka * g_tail, k * g_tail, v]
    for i, x in enumerate(operands):
        _stage(stage_scr.at[i], x)

    def operand(i, b, p):
        return stage_scr[i, p, _seq_rows(b, cc, nseq), :]

    for b, p in problems:
        ar_scr[b, p, :c2] = stack(operand(0, b, p))
        ar_scr[b, p, c2:] = stack(operand(1, b, p))
        bk_scr[b, p, :c2] = stack(operand(2, b, p))
        bk_scr[b, p, c2:] = stack(operand(3, b, p))
        tl_scr[b, p, :c2] = stack(operand(4, b, p))
        tl_scr[b, p, c2:] = stack(operand(5, b, p))
        v_scr[b, p] = stack(operand(6, b, p))

    for b, p in problems:
        ar = ar_scr[b, p]
        bk = bk_scr[b, p]
        if fused:
            gram = _bdot_nt(ar, bk)
            g_ab, g_ak = gram[:c2, :c2], gram[:c2, c2:]
            g_rb, g_rk = gram[c2:, :c2], gram[c2:, c2:]
        else:
            g_ab, g_ak = _bdot_nt(ar[:c2], bk[:c2]), _bdot_nt(ar[:c2], bk[c2:])
            g_rb, g_rk = _bdot_nt(ar[c2:], bk[:c2]), _bdot_nt(ar[c2:], bk[c2:])
        p_scr[b, p] = jnp.where(strict, g_ab, 0.0).astype(BF16)
        lak_scr[b, p] = jnp.where(strict, g_ak, 0.0).astype(BF16)
        mrb_scr[b, p] = jnp.where(incl, g_rb, 0.0).astype(BF16)
        mrk_scr[b, p] = jnp.where(incl, g_rk, 0.0).astype(BF16)
        ar_state = _bdot_nt(ar, s_scr[b, p])
        r32_scr[b, p] = ar_state[:c2]
        aro_scr[b, p] = ar_state[c2:]

    for b, p in problems:
        rhs = r32_scr[b, p] + jnp.dot(lak_scr[b, p], v_scr[b, p], preferred_element_type=F32)
        r32_scr[b, p] = rhs
        rb_scr[b, p] = rhs.astype(BF16)

    for m in range(n_factors):
        last = m == n_factors - 1
        for b, p in problems:
            pw = p_scr[b, p]
            rb = rb_scr[b, p]
            if last:
                delta = jnp.dot(pw, rb, preferred_element_type=F32)
            elif fused:
                both = jnp.dot(pw, jnp.concatenate([pw, rb], axis=1), preferred_element_type=F32)
                p_scr[b, p] = both[:, :c2].astype(BF16)
                delta = both[:, c2:]
            else:
                p_scr[b, p] = jnp.dot(pw, pw, preferred_element_type=F32).astype(BF16)
                delta = jnp.dot(pw, rb, preferred_element_type=F32)
            rhs = r32_scr[b, p] + delta
            r32_scr[b, p] = rhs
            rb_scr[b, p] = rhs.astype(BF16)

    for b, p in problems:
        sl = slice(p * PAIR_W, (p + 1) * PAIR_W)
        u_s = rb_scr[b, p]
        v_s = v_scr[b, p]
        tl = tl_scr[b, p]
        if fused:
            uv = jnp.concatenate([u_s, v_s], axis=0)
            mm = jnp.concatenate([mrb_scr[b, p], mrk_scr[b, p]], axis=1)
            o_st = aro_scr[b, p] + jnp.dot(mm, uv, preferred_element_type=F32)
            upd = _bdot_tn(uv, tl)
        else:
            o_st = (aro_scr[b, p] + jnp.dot(mrb_scr[b, p], u_s, preferred_element_type=F32)
                    + jnp.dot(mrk_scr[b, p], v_s, preferred_element_type=F32))
            upd = _bdot_tn(u_s, tl[:c2]) + _bdot_tn(v_s, tl[c2:])
        o_scr[p, _seq_rows(b, cc, nseq), :] = o_st[:cc] + o_st[cc:]
        s_scr[b, p] = s_scr[b, p] * gend_scr[b][:, sl] + upd

    @pl.when(c == pl.num_programs(1) - 1)
    def _():
        for b, p in problems:
            s_pair = s_scr[b, p]
            s_out_ref[b, 2 * p] = s_pair[:RWKV_HEAD, :RWKV_HEAD]
            s_out_ref[b, 2 * p + 1] = s_pair[RWKV_HEAD:, RWKV_HEAD:]

    ones = ones_ref[...]
    inv_n = 1.0 / RWKV_HEAD
    o = jnp.concatenate([o_scr[j] for j in range(HEAD_PAIRS)], axis=1)
    mean = _head_sums(o, ones) * inv_n
    d = o - mean
    var = _head_sums(d * d, ones) * inv_n
    on = d * lax.rsqrt(var + LNX_EPS) * lnw_ref[...] + lnb_ref[...]
    bonus = _head_sums(r * k * rk_ref[...], ones) * v
    return (on + bonus) * g


def _post_consts(lw, layer):
    return [_Layer(lw[n], layer) for n in ("rwkv_lnx_w", "rwkv_lnx_b", "rwkv_r_k")] + [lw["pair_ones"]]


def _rwkv_group_kernel(*refs, chunk, steps, nseq, has_vfirst, n_consts):
    p_ref, shift0_ref = refs[:2]
    consts = refs[2:2 + n_consts]
    pos = 2 + n_consts
    vf_ref = refs[pos] if has_vfirst else None
    pos += int(has_vfirst)
    s0_ref = refs[pos]
    post = refs[pos + 1:pos + 5]
    pos += 5
    o_ref, v_out, shift_out, s_out_ref, carry = refs[pos:pos + 5]
    scratch = refs[pos + 5:]
    rows = steps * nseq

    @pl.when(pl.program_id(1) == 0)
    def _():
        carry[...] = shift0_ref[...]

    p = p_ref[...].reshape(rows, RWKV_COLS)
    p_prev, new_carry = _token_shift(p, carry[...], nseq)
    carry[...] = new_carry
    shift_out[...] = new_carry
    v_first = vf_ref[...].reshape(rows, RWKV_WIDTH) if has_vfirst else None
    vals = _prep_math(p, p_prev, v_first, consts)
    v_out[...] = vals[3].reshape(steps, nseq, RWKV_WIDTH)
    if steps < chunk:
        pad = jnp.zeros(((chunk - steps) * nseq, RWKV_WIDTH), F32)
        vals = [jnp.concatenate([x, pad], axis=0) for x in vals]
    o = _rec_phases(vals, s0_ref, post, s_out_ref, scratch, chunk=chunk, nseq=nseq)
    o_ref[...] = o[:rows].reshape(steps, nseq, RWKV_WIDTH)


def _rwkv_group(proj, shift0, v_first, state, lw, layer, steps, batch):
    nseq = SEQ_GROUP
    chunk = LONG_CHUNK if steps >= LONG_CHUNK else SHORT_CHUNK
    tile_steps = min(chunk, steps)
    has_vfirst = v_first is not None
    consts = _prep_consts(lw, layer, has_vfirst)
    post = _post_consts(lw, layer)
    rows3 = lambda a: a.reshape(steps, batch, a.shape[-1])
    row_spec = pl.BlockSpec((tile_steps, nseq, RWKV_WIDTH), lambda j, c: (c, j, 0))
    shift_spec = pl.BlockSpec((nseq, RWKV_COLS), lambda j, c: (j, 0))
    st_spec = pl.BlockSpec((None, nseq, RWKV_HEADS, RWKV_HEAD, RWKV_HEAD), lambda j, c: (layer, j, 0, 0, 0))
    args = ([rows3(proj), shift0] + [_array(a) for a in consts]
            + ([rows3(v_first)] if has_vfirst else []) + [state] + [_array(a) for a in post])
    state_arg = 2 + len(consts) + int(has_vfirst)
    in_specs = ([pl.BlockSpec((tile_steps, nseq, RWKV_COLS), lambda j, c: (c, j, 0)),
                 pl.BlockSpec((None, nseq, RWKV_COLS), lambda j, c: (layer, j, 0))]
                + [_resident_spec(a) for a in consts] + ([row_spec] if has_vfirst else [])
                + [st_spec] + [_resident_spec(a) for a in post])
    rows_shape = jax.ShapeDtypeStruct((steps, batch, RWKV_WIDTH), F32)
    o, v, shift, s_new = pl.pallas_call(
        functools.partial(_rwkv_group_kernel, chunk=chunk, steps=tile_steps, nseq=nseq,
                          has_vfirst=has_vfirst, n_consts=len(consts)),
        grid=(batch // nseq, steps // tile_steps),
        in_specs=in_specs,
        out_specs=[row_spec, row_spec, shift_spec, st_spec],
        out_shape=[rows_shape, rows_shape, jax.ShapeDtypeStruct((batch, RWKV_COLS), F32),
                   jax.ShapeDtypeStruct(state.shape, F32)],
        input_output_aliases={state_arg: 3},
        scratch_shapes=[pltpu.VMEM((nseq, RWKV_COLS), F32)] + _rec_scratch(chunk, nseq),
        compiler_params=_cparams("parallel", "arbitrary"),
        name="rwkv_group",
    )(*args)
    return o.reshape(steps * batch, RWKV_WIDTH), v.reshape(steps * batch, RWKV_WIDTH), shift, s_new


def _s5_kernel(u0_ref, u1_ref, x_ref, orw_ref, h0r_ref, h0i_ref, are_ref, aim_ref, ldt_ref, bre_ref,
               bim_ref, cre_ref, cim_ref, d_ref, wglu_ref, bglu_ref, wrw_ref, ws5_ref,
               o_ref, hr_out, hi_out, hr_c, hi_c, hre, him, bbre, bbim, *, batch):
    a_re = are_ref[...]
    a_im = aim_ref[...]
    dt = jnp.exp(ldt_ref[...])
    mag = jnp.exp(a_re * dt)
    ab_re = mag * jnp.cos(a_im * dt)
    ab_im = mag * jnp.sin(a_im * dt)

    @pl.when(pl.program_id(0) == 0)
    def _():
        hr_c[...] = h0r_ref[...]
        hi_c[...] = h0i_ref[...]
        den = a_re * a_re + a_im * a_im
        nr = ab_re - 1.0
        cf_re = (nr * a_re + ab_im * a_im) / den
        cf_im = (ab_im * a_re - nr * a_im) / den
        for hf in range(2):
            ls = slice(hf * S5_HALF_L, (hf + 1) * S5_HALF_L)
            b_re = bre_ref[hf]
            b_im = bim_ref[hf]
            bbre[hf] = (cf_re[:, ls] * b_re - cf_im[:, ls] * b_im).astype(BF16)
            bbim[hf] = (cf_re[:, ls] * b_im + cf_im[:, ls] * b_re).astype(BF16)

    us = (u0_ref[...], u1_ref[...])
    tm = us[0].shape[0]
    for hf in range(2):
        ls = slice(hf * S5_HALF_L, (hf + 1) * S5_HALF_L)
        ub = us[hf].astype(BF16)
        hre[:, ls] = jnp.dot(ub, bbre[hf], preferred_element_type=F32)
        him[:, ls] = jnp.dot(ub, bbim[hf], preferred_element_type=F32)

    n_steps = tm // batch
    if n_steps <= 8:
        hr = hr_c[...]
        hi = hi_c[...]
        for s in range(n_steps):
            rows = slice(s * batch, (s + 1) * batch)
            nhr = ab_re * hr - ab_im * hi + hre[rows, :]
            nhi = ab_re * hi + ab_im * hr + him[rows, :]
            hre[rows, :] = nhr
            him[rows, :] = nhi
            hr, hi = nhr, nhi
        hr_c[...] = hr
        hi_c[...] = hi
    else:
        lane_w = 512
        for lc in range(S5_LANES // lane_w):
            ls = slice(lc * lane_w, (lc + 1) * lane_w)
            abr = jnp.broadcast_to(ab_re[:, ls], (batch, lane_w))
            abi = jnp.broadcast_to(ab_im[:, ls], (batch, lane_w))

            def body(s, carry, ls=ls, abr=abr, abi=abi):
                hr, hi = carry
                rows = pl.ds(pl.multiple_of(s * batch, batch), batch)
                nhr = abr * hr - abi * hi + hre[rows, ls]
                nhi = abr * hi + abi * hr + him[rows, ls]
                hre[rows, ls] = nhr
                him[rows, ls] = nhi
                return nhr, nhi

            hr, hi = lax.fori_loop(0, n_steps, body, (hr_c[:, ls], hi_c[:, ls]), unroll=8)
            hr_c[:, ls] = hr
            hi_c[:, ls] = hi

    hr_out[...] = hr_c[...]
    hi_out[...] = hi_c[...]

    ys = []
    for hf in range(2):
        ls = slice(hf * S5_HALF_L, (hf + 1) * S5_HALF_L)
        cs = slice(hf * S5_HALF_W, (hf + 1) * S5_HALF_W)
        y = (jnp.dot(hre[:, ls].astype(BF16), cre_ref[hf], preferred_element_type=F32)
             - jnp.dot(him[:, ls].astype(BF16), cim_ref[hf], preferred_element_type=F32)
             + d_ref[:, cs] * us[hf])
        ys.append(jax.nn.gelu(y, approximate=True))
    y = jnp.concatenate(ys, axis=1)
    o_s5 = y * jax.nn.sigmoid(_bdot(y, wglu_ref[...]) + bglu_ref[...])
    o_ref[...] = (x_ref[...] + _bdot(orw_ref[...], wrw_ref[...]) + _bdot(o_s5, ws5_ref[...]))


def _s5_mix(proj, x, o_rw, h0r, h0i, lw, layer, batch):
    rows = proj.shape[0]
    tm = min(ROW_TILE, rows)
    u_blk = RWKV_COLS // S5_HALF_W
    consts = [_Layer(h0r, layer), _Layer(h0i, layer)] + [
        _Layer(lw[n], layer) for n in ("s5_a_re", "s5_a_im", "s5_log_dt", "s5_b_re", "s5_b_im", "s5_c_re",
                                       "s5_c_im", "s5_d", "s5_w_glu", "s5_b_glu", "w_out_rw", "w_out_s5")]
    st_shape = jax.ShapeDtypeStruct((batch, S5_LANES), F32)
    st_spec = pl.BlockSpec((batch, S5_LANES), lambda i: (0, 0))
    in_mats = pltpu.VMEM((2, S5_HALF_W, S5_HALF_L), BF16)
    return pl.pallas_call(
        functools.partial(_s5_kernel, batch=batch),
        grid=(rows // tm,),
        in_specs=[pl.BlockSpec((tm, S5_HALF_W), lambda i: (i, u_blk)),
                  pl.BlockSpec((tm, S5_HALF_W), lambda i: (i, u_blk + 1)),
                  pl.BlockSpec((tm, D_MODEL), lambda i: (i, 0)),
                  pl.BlockSpec((tm, RWKV_WIDTH), lambda i: (i, 0))]
                 + [_resident_spec(a) for a in consts],
        out_specs=[pl.BlockSpec((tm, D_MODEL), lambda i: (i, 0)), st_spec, st_spec],
        out_shape=[jax.ShapeDtypeStruct((rows, D_MODEL), F32), st_shape, st_shape],
        scratch_shapes=[pltpu.VMEM((batch, S5_LANES), F32), pltpu.VMEM((batch, S5_LANES), F32),
                        pltpu.VMEM((tm, S5_LANES), F32), pltpu.VMEM((tm, S5_LANES), F32),
                        in_mats, in_mats],
        compiler_params=_cparams("arbitrary"),
        name="s5_mix",
    )(proj, proj, x, o_rw, *[_array(a) for a in consts])


def _softmax_rows(s):
    e = jnp.exp(s - jnp.max(s, axis=-1, keepdims=True))
    return e / jnp.sum(e, axis=-1, keepdims=True)


CACHE_ROWS = N_MEM * X_HEADS * (X_HEAD_DIM // LANES)
CACHE_GROUP = X_HEADS * (X_HEAD_DIM // LANES)
Q_ROWS = 8


def _cache_view(cache):
    l, b = cache.shape[:2]
    halves = X_HEAD_DIM // LANES
    c = cache.reshape(l, b, N_MEM, X_HEADS, halves, LANES)
    return jnp.swapaxes(c, 3, 4).reshape(l, b, CACHE_ROWS, LANES)


def _cache_unview(z, batch):
    halves = X_HEAD_DIM // LANES
    c = z.reshape(z.shape[0], batch, N_MEM, halves, X_HEADS, LANES)
    return jnp.swapaxes(c, 3, 4).reshape(z.shape[0], batch, N_MEM, X_HEADS, X_HEAD_DIM)


def _attn_cache_kernel(q_ref, k_ref, v_ref, o_ref, q_scr, o_scr, *, steps, nseq):
    halves = X_HEAD_DIM // LANES
    scale = X_HEAD_DIM ** -0.5
    n_tiles = D_MODEL // LANES
    rows = steps * nseq
    q = q_ref[...].reshape(rows, D_MODEL)
    pad = jnp.zeros(((Q_ROWS - steps) * nseq, D_MODEL), F32)
    _stage(q_scr, jnp.concatenate([q, pad], axis=0))
    col = lax.broadcasted_iota(jnp.int32, (Q_ROWS, CACHE_ROWS), 1) % CACHE_GROUP
    for b in range(nseq):
        seq = _seq_rows(b, Q_ROWS, nseq)
        qx = jnp.concatenate([q_scr[n, seq, :] for n in range(n_tiles)], axis=0)
        e = _bdot_nt(qx, k_ref[b])
        probs = []
        for h in range(X_HEADS):
            base = h * halves * Q_ROWS
            valid = col == h
            s = jnp.where(valid, e[base:base + Q_ROWS], 0.0)
            for j in range(1, halves):
                part = jnp.where(col == j * X_HEADS + h,
                                 e[base + j * Q_ROWS:base + (j + 1) * Q_ROWS], 0.0)
                s = s + pltpu.roll(part, shift=CACHE_ROWS - j * X_HEADS, axis=1)
            pr = _softmax_rows(jnp.where(valid, s * scale, -1e30))
            probs.append(pr)
            for j in range(1, halves):
                probs.append(pltpu.roll(pr, shift=j * X_HEADS, axis=1))
        ox = _bdot(jnp.concatenate(probs, axis=0), v_ref[b])
        for n in range(n_tiles):
            o_scr[n, seq, :] = ox[n * Q_ROWS:(n + 1) * Q_ROWS]
    o = jnp.concatenate([o_scr[n, :rows, :] for n in range(n_tiles)], axis=1)
    o_ref[...] = o.reshape(steps, nseq, D_MODEL)


def _attn_cache(q, cache_k, cache_v, layer, steps, batch):
    nseq = SEQ_GROUP
    q_spec = pl.BlockSpec((steps, nseq, D_MODEL), lambda j: (0, j, 0))
    m_spec = pl.BlockSpec((None, nseq, CACHE_ROWS, LANES), lambda j: (layer, j, 0, 0))
    stage = pltpu.VMEM((D_MODEL // LANES, Q_ROWS * nseq, LANES), F32)
    out = pl.pallas_call(
        functools.partial(_attn_cache_kernel, steps=steps, nseq=nseq),
        grid=(batch // nseq,),
        in_specs=[q_spec, m_spec, m_spec],
        out_specs=q_spec,
        out_shape=jax.ShapeDtypeStruct((steps, batch, D_MODEL), F32),
        scratch_shapes=[stage, stage],
        compiler_params=_cparams("parallel"),
        name="mem_attn",
    )(q.reshape(steps, batch, D_MODEL), cache_k, cache_v)
    return out.reshape(steps * batch, D_MODEL)


def _cross_long_kernel(x_ref, gain_ref, wq_ref, k_ref, v_ref, wo_ref, o_ref,
                       q_scr, att_scr, s_scr, p_scr, *, batch):
    x = x_ref[...]
    steps = x.shape[0] // batch
    scale = X_HEAD_DIM ** -0.5
    tiles_per_head = X_HEAD_DIM // LANES
    n_tiles = D_MODEL // LANES
    blocks = [(b, h) for b in range(batch) for h in range(X_HEADS)]
    _stage(q_scr, jnp.dot(_rms(x, gain_ref[...]).astype(BF16), wq_ref[...],
                          preferred_element_type=F32))
    for i, (b, h) in enumerate(blocks):
        rows = _seq_rows(b, steps, batch)
        tiles = range(h * tiles_per_head, (h + 1) * tiles_per_head)
        q = jnp.concatenate([q_scr[j, rows, :] for j in tiles], axis=1)
        s_scr[i] = _bdot_nt(q, k_ref[b, :, h * X_HEAD_DIM:(h + 1) * X_HEAD_DIM]) * scale
    p_scr[...] = _softmax_rows(s_scr[...]).astype(BF16)
    for i, (b, h) in enumerate(blocks):
        rows = _seq_rows(b, steps, batch)
        o = jnp.dot(p_scr[i], v_ref[b, :, h * X_HEAD_DIM:(h + 1) * X_HEAD_DIM],
                    preferred_element_type=F32)
        for t in range(tiles_per_head):
            att_scr[h * tiles_per_head + t, rows, :] = o[:, t * LANES:(t + 1) * LANES]
    att = jnp.concatenate([att_scr[j] for j in range(n_tiles)], axis=1)
    o_ref[...] = x + jnp.dot(att.astype(BF16), wo_ref[...], preferred_element_type=F32)


def _cross_long(x, mem_k, mem_v, lw, layer, batch):
    rows = x.shape[0]
    tm = min(ATTN_ROW_TILE, rows)
    steps = tm // batch
    consts = [_Layer(lw["norm_cross"], layer), _Layer(lw["w_cq"], layer), _Layer(mem_k, layer),
              _Layer(mem_v, layer), _Layer(lw["w_co"], layer)]
    return pl.pallas_call(
        functools.partial(_cross_long_kernel, batch=batch),
        grid=(rows // tm,),
        in_specs=[pl.BlockSpec((tm, D_MODEL), lambda i: (i, 0))] + [_resident_spec(a) for a in consts],
        out_specs=pl.BlockSpec((tm, D_MODEL), lambda i: (i, 0)),
        out_shape=jax.ShapeDtypeStruct((rows, D_MODEL), F32),
        scratch_shapes=[pltpu.VMEM((D_MODEL // LANES, tm, LANES), F32),
                        pltpu.VMEM((D_MODEL // LANES, tm, LANES), F32),
                        pltpu.VMEM((batch * X_HEADS, steps, N_MEM), F32),
                        pltpu.VMEM((batch * X_HEADS, steps, N_MEM), BF16)],
        compiler_params=_cparams("parallel"),
        name="cross_attn",
    )(x, *[_array(a) for a in consts])


def _ffn_kernel(x_ref, buf0_ref, gain_ref, wg_ref, wu_ref, cw_ref, cb_ref, wd_ref,
                o_ref, buf_out, carry, *, batch):
    @pl.when(pl.program_id(0) == 0)
    def _():
        carry[...] = buf0_ref[...]

    x = x_ref[...]
    tm = x.shape[0]
    h = _rms(x, gain_ref[...]).astype(BF16)
    gt = jnp.dot(h, wg_ref[...], preferred_element_type=F32)
    up = jnp.dot(h, wu_ref[...], preferred_element_type=F32)
    padded = jnp.concatenate([carry[...], gt], axis=0)
    conv = cb_ref[...]
    for i in range(CONV_W):
        conv = conv + cw_ref[i:i + 1, :] * padded[i * batch:i * batch + tm]
    new_carry = padded[tm:]
    carry[...] = new_carry
    buf_out[...] = new_carry
    act = jax.nn.silu(conv) * up
    o_ref[...] = x + jnp.dot(act.astype(BF16), wd_ref[...], preferred_element_type=F32)


def _ffn(x, buf0, lw, layer, batch):
    rows = x.shape[0]
    tm = min(FFN_ROW_TILE, rows)
    tm = max(tm, (CONV_W - 1) * batch)
    consts = [_Layer(buf0, layer)] + [_Layer(lw[n], layer) for n in (
        "norm_ffn", "w_gate", "w_up", "ffn_conv_w", "ffn_conv_b", "w_down")]
    nbuf = (CONV_W - 1) * batch
    return pl.pallas_call(
        functools.partial(_ffn_kernel, batch=batch),
        grid=(rows // tm,),
        in_specs=[pl.BlockSpec((tm, D_MODEL), lambda i: (i, 0))] + [_resident_spec(a) for a in consts],
        out_specs=[pl.BlockSpec((tm, D_MODEL), lambda i: (i, 0)),
                   pl.BlockSpec((nbuf, D_FF), lambda i: (0, 0))],
        out_shape=[jax.ShapeDtypeStruct((rows, D_MODEL), F32), jax.ShapeDtypeStruct((nbuf, D_FF), F32)],
        scratch_shapes=[pltpu.VMEM((nbuf, D_FF), F32)],
        compiler_params=_cparams("arbitrary"),
        name="conv_ffn",
    )(x, *[_array(a) for a in consts])


def _s5_in_blockdiag(b):
    l = b.shape[0]
    b = b.reshape(l, 2, S5_GROUPS // 2, S5_STATE, S5_GROUP)
    eye = jnp.eye(S5_GROUPS // 2, dtype=b.dtype)
    m = jnp.einsum('lfgph,gk->lfghkp', b, eye)
    return m.reshape(l, 2, S5_HALF_W, S5_HALF_L)


def _s5_out_blockdiag(c):
    l = c.shape[0]
    c = c.reshape(l, 2, S5_GROUPS // 2, S5_GROUP, S5_STATE)
    eye = jnp.eye(S5_GROUPS // 2, dtype=c.dtype)
    m = jnp.einsum('lfgnp,gk->lfgpkn', c, eye)
    return m.reshape(l, 2, S5_HALF_L, S5_HALF_W).astype(BF16)


def _prep_weights(p):
    l = DEPTH
    row = lambda a: a.reshape(a.shape[0], 1, -1)
    lw = {}
    for name in ("norm_mix", "shift_mu", "rwkv_w0", "rwkv_a0", "rwkv_v0", "rwkv_k_k", "rwkv_k_a",
                 "rwkv_r_k", "rwkv_lnx_w", "rwkv_lnx_b", "s5_d", "s5_b_glu", "norm_cross", "norm_ffn",
                 "ffn_conv_b", "s5_a_re", "s5_a_im"):
        lw[name] = row(p[name])
    lw["s5_log_dt"] = row(jnp.repeat(p["s5_log_dt"], S5_STATE, axis=-1))
    lw["norm_final"] = p["norm_final"].reshape(1, -1)
    lw["ffn_conv_w"] = p["ffn_conv_w"]
    for name in ("w_in", "rwkv_w_g2", "s5_w_glu", "w_cq", "w_ck", "w_cv", "w_co", "w_gate", "w_up",
                 "w_down"):
        lw[name] = p[name].astype(BF16)
    w_out = p["w_out"].astype(BF16)
    lw["w_out_rw"] = w_out[:, :RWKV_WIDTH]
    lw["w_out_s5"] = w_out[:, RWKV_WIDTH:]
    z64 = jnp.zeros((l, LORA_PAD - 64, RWKV_WIDTH), F32)
    lw["rwkv_w_w2p"] = jnp.concatenate([p["rwkv_w_w2"], z64], axis=1).astype(BF16)
    lw["rwkv_w_a2p"] = jnp.concatenate([z64, p["rwkv_w_a2"]], axis=1).astype(BF16)
    v_lora = p["rwkv_w_v1"].shape[-1]
    lw["rwkv_w_v1p"] = jnp.pad(p["rwkv_w_v1"], ((0, 0), (0, 0), (0, LORA_PAD - v_lora))).astype(BF16)
    lw["rwkv_w_v2p"] = jnp.pad(p["rwkv_w_v2"], ((0, 0), (0, LORA_PAD - v_lora), (0, 0))).astype(BF16)
    head = jnp.arange(PAIR_W) // RWKV_HEAD
    lw["pair_ones"] = (head[:, None] == head[None, :]).astype(BF16)
    lw["s5_b_re"] = _s5_in_blockdiag(p["s5_b_re"])
    lw["s5_b_im"] = _s5_in_blockdiag(p["s5_b_im"])
    lw["s5_c_re"] = _s5_out_blockdiag(p["s5_c_re"])
    lw["s5_c_im"] = _s5_out_blockdiag(p["s5_c_im"])
    return lw


def _run_trunk(x, mem_k, mem_v, st_rwkv, st_shift, st_re, st_im, st_conv, lw, steps, batch):
    long_seq = steps >= LONG_SEQ
    if long_seq:
        mem_k = mem_k.astype(BF16)
        mem_v = mem_v.astype(BF16)
    else:
        mem_k = _cache_view(mem_k)
        mem_v = _cache_view(mem_v)
    v_first = None
    new_shift, new_re, new_im, new_conv = [], [], [], []
    for l in range(DEPTH):
        proj = _rowmm([x], [_Layer(lw["w_in"], l)], gain=_Layer(lw["norm_mix"], l), name="in_proj",
                      row_tile=IN_PROJ_ROW_TILE)
        o_rw, v_l, sh, st_rwkv = _rwkv_group(proj, st_shift, v_first, st_rwkv, lw, l, steps, batch)
        if l == 0:
            v_first = v_l
        x, hr, hi = _s5_mix(proj, x, o_rw, st_re, st_im, lw, l, batch)
        if long_seq:
            x = _cross_long(x, mem_k, mem_v, lw, l, batch)
        else:
            q = _rowmm([x], [_Layer(lw["w_cq"], l)], gain=_Layer(lw["norm_cross"], l), name="cross_q")
            att = _attn_cache(q, mem_k, mem_v, l, steps, batch)
            x = _rowmm([att], [_Layer(lw["w_co"], l)], resid=x, name="cross_o")
        x, cb = _ffn(x, st_conv, lw, l, batch)
        new_shift.append(sh)
        new_re.append(hr)
        new_im.append(hi)
        new_conv.append(cb)
    return (x, st_rwkv, jnp.stack(new_shift), jnp.stack(new_re), jnp.stack(new_im),
            jnp.stack(new_conv))


def _group(x, mem_k, mem_v, st_rwkv, st_shift, st_re, st_im, st_conv, lw):
    b, t, _ = x.shape
    long_seq = t >= LONG_SEQ
    if long_seq:
        xt = _to_time_major(x)
    else:
        xt = jnp.swapaxes(x, 0, 1).reshape(t * b, D_MODEL)
    conv_t = jnp.swapaxes(st_conv, 1, 2).reshape(DEPTH, (CONV_W - 1) * b, D_FF)
    y, rw, sh, re, im, cv = _run_trunk(
        xt, mem_k, mem_v, st_rwkv, st_shift, st_re.reshape(DEPTH, b, S5_LANES),
        st_im.reshape(DEPTH, b, S5_LANES), conv_t, lw, t, b)
    if long_seq:
        y = _final_norm_long(y, lw["norm_final"], b)
    else:
        y = jnp.swapaxes(_rownorm(y, lw["norm_final"]).reshape(t, b, D_MODEL), 0, 1)
    cv = jnp.swapaxes(cv.reshape(DEPTH, CONV_W - 1, b, D_FF), 1, 2)
    return (y, rw, sh, re.reshape(DEPTH, b, S5_GROUPS, S5_STATE),
            im.reshape(DEPTH, b, S5_GROUPS, S5_STATE), cv)


def kernel(x_prompt, x_sample, mem_prompt, state_rwkv, state_shift, state_s5_re, state_s5_im, state_ffn_conv, cache_mem_k, cache_mem_v, norm_mix, w_in, shift_mu, rwkv_w0, rwkv_w_w2, rwkv_a0, rwkv_w_a2, rwkv_v0, rwkv_w_v1, rwkv_w_v2, rwkv_w_g2, rwkv_k_k, rwkv_k_a, rwkv_r_k, rwkv_lnx_w, rwkv_lnx_b, s5_a_re, s5_a_im, s5_log_dt, s5_b_re, s5_b_im, s5_c_re, s5_c_im, s5_d, s5_w_glu, s5_b_glu, w_out, norm_cross, w_cq, w_ck, w_cv, w_co, norm_ffn, w_gate, w_up, ffn_conv_w, ffn_conv_b, w_down, norm_final):
    lw = _prep_weights(dict(
        norm_mix=norm_mix, w_in=w_in, shift_mu=shift_mu, rwkv_w0=rwkv_w0, rwkv_w_w2=rwkv_w_w2,
        rwkv_a0=rwkv_a0, rwkv_w_a2=rwkv_w_a2, rwkv_v0=rwkv_v0, rwkv_w_v1=rwkv_w_v1,
        rwkv_w_v2=rwkv_w_v2, rwkv_w_g2=rwkv_w_g2, rwkv_k_k=rwkv_k_k, rwkv_k_a=rwkv_k_a,
        rwkv_r_k=rwkv_r_k, rwkv_lnx_w=rwkv_lnx_w, rwkv_lnx_b=rwkv_lnx_b, s5_a_re=s5_a_re,
        s5_a_im=s5_a_im, s5_log_dt=s5_log_dt, s5_b_re=s5_b_re, s5_b_im=s5_b_im, s5_c_re=s5_c_re,
        s5_c_im=s5_c_im, s5_d=s5_d, s5_w_glu=s5_w_glu, s5_b_glu=s5_b_glu, w_out=w_out,
        norm_cross=norm_cross, w_cq=w_cq, w_ck=w_ck, w_cv=w_cv, w_co=w_co, norm_ffn=norm_ffn,
        w_gate=w_gate, w_up=w_up, ffn_conv_w=ffn_conv_w, ffn_conv_b=ffn_conv_b, w_down=w_down,
        norm_final=norm_final))
    bp = x_prompt.shape[0]
    mem_rows = mem_prompt.reshape(bp * N_MEM, D_MODEL)
    p_mem_k, k_rows = _mem_proj(mem_rows, lw["w_ck"])
    p_mem_v, v_rows = _mem_proj(mem_rows, lw["w_cv"])
    z_rw = jnp.zeros((DEPTH, bp, RWKV_HEADS, RWKV_HEAD, RWKV_HEAD), F32)
    z_shift = jnp.zeros((DEPTH, bp, RWKV_COLS), F32)
    z_s5 = jnp.zeros((DEPTH, bp, S5_GROUPS, S5_STATE), F32)
    z_conv = jnp.zeros((DEPTH, bp, CONV_W - 1, D_FF), F32)
    y_prompt, p_rwkv, p_shift, p_re, p_im, p_conv = _group(
        x_prompt, k_rows.reshape(DEPTH, bp, N_MEM, D_MODEL), v_rows.reshape(DEPTH, bp, N_MEM, D_MODEL),
        z_rw, z_shift, z_s5, z_s5, z_conv, lw)
    y_sample, s_rwkv, s_shift, s_re, s_im, s_conv = _group(
        x_sample, cache_mem_k, cache_mem_v, state_rwkv, state_shift, state_s5_re, state_s5_im,
        state_ffn_conv, lw)
    return (y_prompt, y_sample, p_rwkv, p_shift, p_re, p_im, p_conv,
            _cache_unview(p_mem_k, bp), _cache_unview(p_mem_v, bp),
            s_rwkv, s_shift, s_re, s_im, s_conv)
```
